```python
import math
import jax
import jax.numpy as jnp
from jax import lax
import numpy as np


D_MODEL = 2048
BATCH = 4
SEQ = 2048
DEPTH = 4
DEC_BATCH = 8
DEC_SEQ = 1
PAST_LEN = 16384
PAGE_SIZE = 128

RET_HEADS = 6
RET_DIM = 128
RET_WIDTH = RET_HEADS * RET_DIM
RET_CHUNK = 128
RET_THETA = 10000.0
ATT_HEADS = 4
ATT_DIM = 64
ATT_PATTERNS = ((128, 1), (512, 4), (2048, 16))
N_PAT = len(ATT_PATTERNS)
ATT_WIDTH = N_PAT * ATT_HEADS * ATT_DIM
ATT_OUT = ATT_HEADS * ATT_DIM
ROPE_THETA = 500000.0
ROPE_DIMS = ATT_DIM // 4
QUERY_BLOCK = 128
POOL_WINDOWS = (2, 4, 8, 16)
POOL_GROUP = 128
POOL_WIDTH = len(POOL_WINDOWS) * POOL_GROUP
POOL_BUF = max(POOL_WINDOWS) - 1
MIX_WIDTH = RET_WIDTH + ATT_WIDTH + POOL_WIDTH
IN_WIDTH = 4 * RET_WIDTH + 3 * ATT_WIDTH + POOL_WIDTH
OUT_WIDTH = RET_WIDTH + ATT_OUT + POOL_WIDTH
D_FF = 5632
HALF_STEP = 0.5
N_SUB = 3
EPS = 1e-6

kernel_name = 'hybrid_retention_dilated_pool_decoder_step'


def _rmsnorm(x, g):
    xf = x.astype(jnp.float32)
    y = xf * lax.rsqrt(jnp.mean(xf * xf, axis=-1, keepdims=True) + EPS)
    return (y * g.astype(jnp.float32)).astype(x.dtype)


def _head_norm(o, g):
    mu = jnp.mean(o, axis=-1, keepdims=True)
    var = jnp.mean(jnp.square(o - mu), axis=-1, keepdims=True)
    return (o - mu) * lax.rsqrt(var + EPS) * g.astype(jnp.float32)


def _rope(x, pos, n_rot, theta):
    half = n_rot // 2
    freq = jnp.power(jnp.float32(theta), -jnp.arange(half, dtype=jnp.float32) / half)
    ang = pos[:, None] * freq[None, :]
    cos = jnp.cos(ang)[:, None, :]
    sin = jnp.sin(ang)[:, None, :]
    xf = x.astype(jnp.float32)
    x1 = xf[..., :half]
    x2 = xf[..., half:n_rot]
    out = jnp.concatenate([x1 * cos - x2 * sin, x1 * sin + x2 * cos, xf[..., n_rot:]], axis=-1)
    return out.astype(x.dtype)


def _swiglu(h, wg, wu, wd):
    return (jax.nn.silu(h @ wg) * (h @ wu)) @ wd


def _retention(q, k, v, s0):
    B, S, H, D = q.shape
    C = math.gcd(S, RET_CHUNK)
    n = S // C
    lg = jnp.log1p(-jnp.exp2(-5.0 - jnp.arange(H, dtype=jnp.float32)))
    idx = jnp.arange(C, dtype=jnp.float32)
    diff = idx[:, None] - idx[None, :]
    decay_in = jnp.where(diff[None] >= 0, jnp.exp(jnp.maximum(diff, 0.0)[None] * lg[:, None, None]), 0.0)
    decay_q = jnp.exp((idx[:, None] + 1.0) * lg[None, :])
    decay_k = jnp.exp((C - 1.0 - idx[:, None]) * lg[None, :])
    decay_c = jnp.exp(C * lg)

    def to_chunks(t):
        return t.astype(jnp.float32).reshape(B, n, C, H, D).transpose(1, 0, 2, 3, 4)

    def step(s, inp):
        qc, kc, vc = inp
        a = jnp.einsum('bnhd,bmhd->bhnm', qc, kc) * decay_in[None]
        o = jnp.einsum('bhnm,bmhd->bnhd', a, vc)
        o = o + jnp.einsum('bnhd,bhde->bnhe', qc, s) * decay_q[None, :, :, None]
        s = decay_c[None, :, None, None] * s + jnp.einsum('bmhd,bmhe->bhde', kc * decay_k[None, :, :, None], vc)
        return s, o

    s, o = lax.scan(step, s0.astype(jnp.float32), (to_chunks(q), to_chunks(k), to_chunks(v)))
    o = o.transpose(1, 0, 2, 3, 4).reshape(B, S, H, D)
    return o, s


def _dilated_attention(q, k_all, v_all, n_past, dilation, n_keys):
    B, S, H, dh = q.shape
    qb = math.gcd(S, QUERY_BLOCK)
    nb = S // qb
    offs = dilation * jnp.arange(n_keys + 1)
    scale = dh ** -0.5

    def block(i):
        qi = lax.dynamic_slice_in_dim(q, i * qb, qb, axis=1)
        rows = n_past + i * qb + jnp.arange(qb)
        kidx = rows[:, None] - offs[None, :]
        valid = kidx >= 0
        kidx = jnp.maximum(kidx, 0)
        kg = k_all[:, kidx]
        vg = v_all[:, kidx]
        s = jnp.einsum('bqhd,bqjhd->bqhj', qi, kg).astype(jnp.float32) * scale
        s = jnp.where(valid[None, :, None, :], s, -1e30)
        lse = jax.nn.logsumexp(s, axis=-1)
        p = jnp.exp(s - lse[..., None])
        o = jnp.einsum('bqhj,bqjhd->bqhd', p.astype(vg.dtype), vg)
        return o, lse

    o, lse = lax.map(block, jnp.arange(nb))
    o = o.transpose(1, 0, 2, 3, 4).reshape(B, S, H, dh)
    lse = lse.transpose(1, 0, 2, 3).reshape(B, S, H)
    return o, lse


def _pool_mixer(u, buf, w_pool, scale):
    B, S, _ = u.shape
    P = buf.shape[1]
    full = jnp.concatenate([buf, u], axis=1)
    ff = full.astype(jnp.float32)
    cs = jnp.concatenate([jnp.zeros((B, 1, POOL_WIDTH), jnp.float32), jnp.cumsum(ff, axis=1)], axis=1)
    t = P + jnp.arange(S)
    outs = []
    for g, win in enumerate(POOL_WINDOWS):
        lo = jnp.maximum(t + 1 - win, 0)
        cnt = (t + 1 - lo).astype(jnp.float32)[None, :, None]
        c0, c1 = g * POOL_GROUP, (g + 1) * POOL_GROUP
        mean = (cs[:, t + 1][..., c0:c1] - cs[:, lo][..., c0:c1]) / cnt
        outs.append(mean - ff[:, P:, c0:c1])
    pooled = jnp.stack(outs, axis=2)
    y = jnp.einsum('bsgc,gcd->bsgd', pooled, w_pool.astype(jnp.float32)).reshape(B, S, POOL_WIDTH)
    y = y * scale.astype(jnp.float32)
    return y.astype(u.dtype), full[:, full.shape[1] - POOL_BUF:]


def _mixer(h, pos0, ret_s0, bufs_k, bufs_v, pool_b, keep, p):
    B, S, _ = h.shape
    z = h @ p['w_in']
    sizes = (RET_WIDTH,) * 4 + (ATT_WIDTH,) * 3 + (POOL_WIDTH,)
    cuts = []
    acc = 0
    for sz in sizes[:-1]:
        acc += sz
        cuts.append(acc)
    rq, rk, rv, rg, aq, ak, av, pu = jnp.split(z, cuts, axis=-1)
    pos = pos0 + jnp.arange(S, dtype=jnp.float32)

    rshape = (B, S, RET_HEADS, RET_DIM)
    rq = _rope(rq.reshape(rshape), pos, RET_DIM, RET_THETA)
    rk = _rope(rk.reshape(rshape), pos, RET_DIM, RET_THETA) * (RET_DIM ** -0.5)
    ro, ret_s = _retention(rq, rk, rv.reshape(rshape), ret_s0)
    ro = _head_norm(ro, p['ret_norm_g'].reshape(RET_HEADS, RET_DIM)).reshape(B, S, RET_WIDTH)
    ro = jax.nn.silu(rg) * ro.astype(h.dtype)

    ashape = (B, S, N_PAT * ATT_HEADS, ATT_DIM)
    gshape = (B, S, N_PAT, ATT_HEADS, ATT_DIM)
    aq = _rope(aq.reshape(ashape), pos, ROPE_DIMS, ROPE_THETA).reshape(gshape)
    ak = _rope(ak.reshape(ashape), pos, ROPE_DIMS, ROPE_THETA).reshape(gshape)
    av = av.reshape(gshape)
    outs, lses, new_k, new_v = [], [], [], []
    for g, (win, dil) in enumerate(ATT_PATTERNS):
        k_all = jnp.concatenate([bufs_k[g], ak[:, :, g]], axis=1)
        v_all = jnp.concatenate([bufs_v[g], av[:, :, g]], axis=1)
        o, lse = _dilated_attention(aq[:, :, g], k_all, v_all, bufs_k[g].shape[1], dil, win // dil)
        outs.append(o)
        lses.append(lse)
        new_k.append(k_all[:, k_all.shape[1] - keep[g]:])
        new_v.append(v_all[:, v_all.shape[1] - keep[g]:])
    wts = jax.nn.softmax(jnp.stack(lses, axis=-1), axis=-1)
    ao = jnp.einsum('bshgd,bshg->bshd', jnp.stack(outs, axis=3), wts.astype(h.dtype)).reshape(B, S, ATT_OUT)

    po, pool_new = _pool_mixer(pu, pool_b, p['w_pool'], p['pool_scale'])

    y = jnp.concatenate([ro, ao, po], axis=-1) @ p['w_out']
    return y, (ret_s, tuple(new_k), tuple(new_v), pool_new)


def _layer(x, c, pos0, ret_s0, bufs_k, bufs_v, pool_b, keep, p):
    B = x.shape[0]
    mod = (jax.nn.silu(c) @ p['w_ada'] + p['b_ada']).reshape(B, N_SUB, 3, D_MODEL)

    def pre(x, i):
        return _rmsnorm(x, p['norm_g'][i]) * (1.0 + mod[:, i, 1, None, :]) + mod[:, i, 0, None, :]

    x = x + HALF_STEP * mod[:, 0, 2, None, :] * _swiglu(pre(x, 0), p['w1_gate'], p['w1_up'], p['w1_down'])
    y, st = _mixer(pre(x, 1), pos0, ret_s0, bufs_k, bufs_v, pool_b, keep, p)
    x = x + mod[:, 1, 2, None, :] * y
    x = x + HALF_STEP * mod[:, 2, 2, None, :] * _swiglu(pre(x, 2), p['w2_gate'], p['w2_up'], p['w2_down'])
    return x, st


def _trunk(x, c, pos0, caches, p, final_norm_g):
    B, S, _ = x.shape
    rets, pools = [], []
    kss = [[] for _ in ATT_PATTERNS]
    vss = [[] for _ in ATT_PATTERNS]
    for l in range(DEPTH):
        pl = {name: arr[l] for name, arr in p.items()}
        if caches is None:
            ret_s0 = jnp.zeros((B, RET_HEADS, RET_DIM, RET_DIM), jnp.float32)
            bufs_k = tuple(jnp.zeros((B, 0, ATT_HEADS, ATT_DIM), x.dtype) for _ in ATT_PATTERNS)
            bufs_v = tuple(jnp.zeros((B, 0, ATT_HEADS, ATT_DIM), x.dtype) for _ in ATT_PATTERNS)
            pool_b = jnp.zeros((B, 0, POOL_WIDTH), x.dtype)
            keep = tuple(min(win, S) for win, _ in ATT_PATTERNS)
        else:
            state_ret, cks, cvs, cpool = caches
            ret_s0 = state_ret[l]
            bufs_k = tuple(ck[l] for ck in cks)
            bufs_v = tuple(cv[l] for cv in cvs)
            pool_b = cpool[l]
            keep = tuple(ck.shape[2] for ck in cks)
        x, (ret_s, nk, nv, npool) = _layer(x, c, pos0, ret_s0, bufs_k, bufs_v, pool_b, keep, pl)
        rets.append(ret_s)
        pools.append(npool)
        for g in range(N_PAT):
            kss[g].append(nk[g])
            vss[g].append(nv[g])
    y = _rmsnorm(x, final_norm_g)
    ks = tuple(jnp.stack(k, axis=0) for k in kss)
    vs = tuple(jnp.stack(v, axis=0) for v in vss)
    return y, (jnp.stack(rets, axis=0), ks, vs, jnp.stack(pools, axis=0))


def setup_inputs(seed: int = 0) -> dict:
    key = jax.random.key(seed)
    ks = jax.random.split(key, 32)
    f32 = jnp.float32

    def nrm(k, shape, s):
        return jax.random.normal(k, shape, f32) * s

    inp = {}
    inp['x_prompt'] = nrm(ks[0], (BATCH, SEQ, D_MODEL), 1.0)
    inp['x_sample'] = nrm(ks[1], (DEC_BATCH, DEC_SEQ, D_MODEL), 1.0)
    inp['state_ret'] = nrm(ks[2], (DEPTH, DEC_BATCH, RET_HEADS, RET_DIM, RET_DIM), 0.1)
    for g, (win, dil) in enumerate(ATT_PATTERNS):
        rows = min(win, PAST_LEN)
        inp['cache_k_w%d' % win] = nrm(ks[3 + 2 * g], (DEPTH, DEC_BATCH, rows, ATT_HEADS, ATT_DIM), 1.0)
        inp['cache_v_w%d' % win] = nrm(ks[4 + 2 * g], (DEPTH, DEC_BATCH, rows, ATT_HEADS, ATT_DIM), 1.0)
    inp['cache_pool'] = nrm(ks[9], (DEPTH, DEC_BATCH, POOL_BUF, POOL_WIDTH), 1.0)
    inp['c_prompt'] = nrm(ks[10], (BATCH, D_MODEL), 1.0)
    inp['c_sample'] = nrm(ks[11], (DEC_BATCH, D_MODEL), 1.0)
    inp['w_ada'] = nrm(ks[12], (DEPTH, D_MODEL, N_SUB * 3 * D_MODEL), 0.5 * D_MODEL ** -0.5)
    inp['b_ada'] = nrm(ks[13], (DEPTH, N_SUB * 3 * D_MODEL), 0.02)
    inp['norm_g'] = 1.0 + nrm(ks[14], (DEPTH, N_SUB, D_MODEL), 0.02)
    inp['w_in'] = nrm(ks[15], (DEPTH, D_MODEL, IN_WIDTH), D_MODEL ** -0.5)
    inp['ret_norm_g'] = 1.0 + nrm(ks[16], (DEPTH, RET_WIDTH), 0.02)
    inp['w_pool'] = nrm(ks[17], (DEPTH, len(POOL_WINDOWS), POOL_GROUP, POOL_GROUP), POOL_GROUP ** -0.5)
    inp['pool_scale'] = 1.0 + nrm(ks[18], (DEPTH, POOL_WIDTH), 0.1)
    inp['w_out'] = nrm(ks[19], (DEPTH, OUT_WIDTH, D_MODEL), OUT_WIDTH ** -0.5)
    inp['w1_gate'] = nrm(ks[20], (DEPTH, D_MODEL, D_FF), D_MODEL ** -0.5)
    inp['w1_up'] = nrm(ks[21], (DEPTH, D_MODEL, D_FF), D_MODEL ** -0.5)
    inp['w1_down'] = nrm(ks[22], (DEPTH, D_FF, D_MODEL), D_FF ** -0.5)
    inp['w2_gate'] = nrm(ks[23], (DEPTH, D_MODEL, D_FF), D_MODEL ** -0.5)
    inp['w2_up'] = nrm(ks[24], (DEPTH, D_MODEL, D_FF), D_MODEL ** -0.5)
    inp['w2_down'] = nrm(ks[25], (DEPTH, D_FF, D_MODEL), D_FF ** -0.5)
    inp['final_norm_g'] = 1.0 + nrm(ks[26], (D_MODEL,), 0.02)
    return inp


def reference(x_prompt, x_sample, state_ret, cache_k_w128, cache_v_w128, cache_k_w512, cache_v_w512,
              cache_k_w2048, cache_v_w2048, cache_pool, c_prompt, c_sample, w_ada, b_ada, norm_g, w_in,
              ret_norm_g, w_pool, pool_scale, w_out, w1_gate, w1_up, w1_down, w2_gate, w2_up, w2_down,
              final_norm_g):
    p = {'w_ada': w_ada, 'b_ada': b_ada, 'norm_g': norm_g, 'w_in': w_in, 'ret_norm_g': ret_norm_g,
         'w_pool': w_pool, 'pool_scale': pool_scale, 'w_out': w_out,
         'w1_gate': w1_gate, 'w1_up': w1_up, 'w1_down': w1_down,
         'w2_gate': w2_gate, 'w2_up': w2_up, 'w2_down': w2_down}
    y_prompt, (ret_p, kp, vp, pool_p) = _trunk(x_prompt, c_prompt, 0, None, p, final_norm_g)
    caches = (state_ret,
              (cache_k_w128, cache_k_w512, cache_k_w2048),
              (cache_v_w128, cache_v_w512, cache_v_w2048),
              cache_pool)
    y_sample, (ret_s, ks, vs, pool_s) = _trunk(x_sample, c_sample, PAST_LEN, caches, p, final_norm_g)
    return (y_prompt, y_sample, ret_p, ret_s,
            kp[0], ks[0], vp[0], vs[0],
            kp[1], ks[1], vp[1], vs[1],
            kp[2], ks[2], vp[2], vs[2],
            pool_p, pool_s)
```

```python
import functools
import math

import jax
import jax.numpy as jnp
from jax import lax
from jax.experimental import pallas as pl
from jax.experimental.pallas import tpu as pltpu

F32 = jnp.float32
BF16 = jnp.bfloat16

RET_HEADS = 6
RET_DIM = 128
RET_WIDTH = RET_HEADS * RET_DIM
RET_CHUNK = 128
RET_THETA = 10000.0
ATT_HEADS = 4
ATT_DIM = 64
ATT_GROUP = ATT_HEADS * ATT_DIM
ATT_PATTERNS = ((128, 1), (512, 4), (2048, 16))
N_PAT = len(ATT_PATTERNS)
ATT_WIDTH = N_PAT * ATT_GROUP
ROPE_THETA = 500000.0
ROPE_DIMS = ATT_DIM // 4
ROPE_HALF = ROPE_DIMS // 2
QUERY_BLOCK = 128
POOL_WINDOWS = (2, 4, 8, 16)
POOL_GROUP = 128
POOL_WIDTH = len(POOL_WINDOWS) * POOL_GROUP
POOL_BUF = max(POOL_WINDOWS) - 1
POOL_PAD = 16
N_SUB = 3
HALF_STEP = 0.5
EPS = 1e-6
MASK_VALUE = -1e30

COL_RQ, COL_RK, COL_RV, COL_RG = 0, RET_WIDTH, 2 * RET_WIDTH, 3 * RET_WIDTH
COL_AQ = 4 * RET_WIDTH
COL_AK = COL_AQ + ATT_WIDTH
COL_AV = COL_AK + ATT_WIDTH
COL_PU = COL_AV + ATT_WIDTH
IN_WIDTH = COL_PU + POOL_WIDTH

V7X_VMEM_BYTES = 64 * 1024 * 1024
VMEM_LIMIT = V7X_VMEM_BYTES - 8 * 1024 * 1024
SUBLANES = 8

LOG_GAMMA = tuple(math.log1p(-(2.0 ** (-5.0 - h))) for h in range(RET_HEADS))


def _params(*sem):
    return pltpu.CompilerParams(dimension_semantics=sem, vmem_limit_bytes=VMEM_LIMIT)


def _dot(a, b):
    return jnp.dot(a, b, preferred_element_type=F32)


def _dot_nt(a, b):
    return lax.dot_general(a, b, (((1,), (1,)), ((), ())), preferred_element_type=F32)


def _bf16_round(x):
    return x.astype(BF16).astype(F32)


def _ada_kernel(c_ref, w_ref, b_ref, od_ref, op_ref, *, n_dec, n_pr):
    c = c_ref[...]
    a = (c * jax.nn.sigmoid(c)).astype(BF16)
    res = _dot(a, w_ref[...].astype(BF16)) + b_ref[...]
    od_ref[...] = res[0:n_dec]
    for b in range(n_pr):
        op_ref[b] = res[n_dec + b:n_dec + b + 1]


def _ada(c_all, n_dec, n_pr, w_ada, b_ada, tn=1024):
    depth, d, n = w_ada.shape
    rows = c_all.shape[0]
    per = d // tn
    return pl.pallas_call(
        functools.partial(_ada_kernel, n_dec=n_dec, n_pr=n_pr),
        out_shape=(jax.ShapeDtypeStruct((depth, N_SUB * 3, n_dec, d), F32),
                   jax.ShapeDtypeStruct((depth, N_SUB * 3, n_pr, 1, d), F32)),
        grid=(depth, n // tn),
        in_specs=[
            pl.BlockSpec((rows, d), lambda l, j: (0, 0)),
            pl.BlockSpec((None, d, tn), lambda l, j: (l, 0, j)),
            pl.BlockSpec((None, 1, tn), lambda l, j: (l, 0, j)),
        ],
        out_specs=(
            pl.BlockSpec((None, None, n_dec, tn), lambda l, j: (l, j // per, 0, j % per)),
            pl.BlockSpec((None, None, n_pr, 1, tn), lambda l, j: (l, j // per, 0, 0, j % per)),
        ),
        compiler_params=_params("arbitrary", "arbitrary"),
        name="ada_mod",
    )(c_all, w_ada, b_ada.reshape(depth, 1, n))


def _rmsnorm(x, g):
    ms = jnp.mean(x * x, axis=-1, keepdims=True)
    return x * lax.rsqrt(ms + EPS) * g


def _row_chunks(tm):
    rc = min(tm, 128)
    return rc, tm // rc


def _mod_rows(ref, r, rc):
    return ref[...] if ref.shape[0] == 1 else ref[pl.ds(r, rc), :]


def _prenorm_to(h_ref, x_ref, g_ref, sh_ref, sc_ref):
    rc, n = _row_chunks(x_ref.shape[0])

    def body(i, carry):
        r = pl.multiple_of(i * rc, rc)
        y = _rmsnorm(x_ref[pl.ds(r, rc), :], g_ref[...])
        h = y * (1.0 + _mod_rows(sc_ref, r, rc)) + _mod_rows(sh_ref, r, rc)
        h_ref[pl.ds(r, rc), :] = h.astype(BF16)
        return carry

    lax.fori_loop(0, n, body, 0)


def _mod_specs(mod, layer, sub, rows_per_batch, tm, grid_rank):
    d = mod.shape[-1]
    specs = []
    for k in range(3):
        j = sub * 3 + k
        if mod.ndim == 5:
            if grid_rank == 2:
                idx = (lambda j: lambda i, f: (layer, j, (i * tm) // rows_per_batch, 0, 0))(j)
            else:
                idx = (lambda j: lambda i: (layer, j, (i * tm) // rows_per_batch, 0, 0))(j)
            specs.append(pl.BlockSpec((None, None, None, 1, d), idx))
        else:
            if grid_rank == 2:
                idx = (lambda j: lambda i, f: (layer, j, 0, 0))(j)
            else:
                idx = (lambda j: lambda i: (layer, j, 0, 0))(j)
            specs.append(pl.BlockSpec((None, None, tm, d), idx))
    return specs


def _ffn_kernel(x_ref, sh_ref, sc_ref, gt_ref, g_ref, wg_ref, wu_ref, wd_ref, *rest,
                n_f, final_norm):
    if final_norm:
        fg_ref, o_ref, h_ref = rest
    else:
        o_ref, h_ref = rest
    f = pl.program_id(1)

    @pl.when(f == 0)
    def _():
        _prenorm_to(h_ref, x_ref, g_ref, sh_ref, sc_ref)

    h = h_ref[...]
    gate = _dot(h, wg_ref[...].astype(BF16))
    up = _dot(h, wu_ref[...].astype(BF16))
    act = (gate * jax.nn.sigmoid(gate) * up).astype(BF16)
    d = o_ref.shape[1]
    dc = min(d, 512)
    for c0 in range(0, d, dc):
        y = _dot(act, wd_ref[:, c0:c0 + dc].astype(BF16))

        @pl.when(f == 0)
        def _():
            o_ref[:, c0:c0 + dc] = y

        @pl.when(f > 0)
        def _():
            o_ref[:, c0:c0 + dc] += y

    @pl.when(f == n_f - 1)
    def _():
        rc, n = _row_chunks(x_ref.shape[0])

        def body(i, carry):
            r = pl.multiple_of(i * rc, rc)
            rows = pl.ds(r, rc)
            out = x_ref[rows, :] + HALF_STEP * _mod_rows(gt_ref, r, rc) * o_ref[rows, :]
            if final_norm:
                out = _rmsnorm(out, fg_ref[...])
            o_ref[rows, :] = out
            return carry

        lax.fori_loop(0, n, body, 0)


def _ffn(x, mod, layer, sub, rows_per_batch, norm_g, wg, wu, wd, final_g, tm, tf):
    m, d = x.shape
    d_ff = wg.shape[-1]
    assert m % tm == 0 and d_ff % tf == 0
    n_f = d_ff // tf
    final_norm = final_g is not None
    in_specs = [pl.BlockSpec((tm, d), lambda i, j: (i, 0))]
    in_specs += _mod_specs(mod, layer, sub, rows_per_batch, tm, 2)
    in_specs += [
        pl.BlockSpec((None, None, 1, d), lambda i, j: (layer, sub, 0, 0)),
        pl.BlockSpec((None, d, tf), lambda i, j: (layer, 0, j)),
        pl.BlockSpec((None, d, tf), lambda i, j: (layer, 0, j)),
        pl.BlockSpec((None, tf, d), lambda i, j: (layer, j, 0)),
    ]
    args = [x, mod, mod, mod, norm_g, wg, wu, wd]
    if final_norm:
        in_specs.append(pl.BlockSpec((1, d), lambda i, j: (0, 0)))
        args.append(final_g)
    return pl.pallas_call(
        functools.partial(_ffn_kernel, n_f=n_f, final_norm=final_norm),
        out_shape=jax.ShapeDtypeStruct((m, d), F32),
        grid=(m // tm, n_f),
        in_specs=in_specs,
        out_specs=pl.BlockSpec((tm, d), lambda i, j: (i, 0)),
        scratch_shapes=[pltpu.VMEM((tm, d), BF16)],
        compiler_params=_params("parallel", "arbitrary"),
        name="ffn",
    )(*args)


def _inproj_kernel(x_ref, sh_ref, sc_ref, g_ref, w_ref, o_ref, h_ref):
    @pl.when(pl.program_id(1) == 0)
    def _():
        _prenorm_to(h_ref, x_ref, g_ref, sh_ref, sc_ref)

    o_ref[...] = _dot(h_ref[...], w_ref[...].astype(BF16))


def _inproj(x, mod, layer, rows_per_batch, norm_g, w_in, tm, tn):
    m, d = x.shape
    n = w_in.shape[-1]
    assert m % tm == 0 and n % tn == 0
    sh, sc, _ = _mod_specs(mod, layer, 1, rows_per_batch, tm, 2)
    return pl.pallas_call(
        _inproj_kernel,
        out_shape=jax.ShapeDtypeStruct((m, n), F32),
        grid=(m // tm, n // tn),
        in_specs=[
            pl.BlockSpec((tm, d), lambda i, j: (i, 0)),
            sh, sc,
            pl.BlockSpec((None, None, 1, d), lambda i, j: (layer, 1, 0, 0)),
            pl.BlockSpec((None, d, tn), lambda i, j: (layer, 0, j)),
        ],
        out_specs=pl.BlockSpec((tm, tn), lambda i, j: (i, j)),
        scratch_shapes=[pltpu.VMEM((tm, d), BF16)],
        compiler_params=_params("parallel", "arbitrary"),
        name="in_proj",
    )(x, mod, mod, norm_g, w_in)


def _outproj_kernel(x_ref, gt_ref, ro_ref, ao_ref, po_ref, w_ref, o_ref, wb_ref):
    @pl.when(pl.program_id(0) == 0)
    def _():
        wb_ref[...] = w_ref[...].astype(BF16)

    r0, r1 = RET_WIDTH, RET_WIDTH + ATT_GROUP
    y = _dot(ro_ref[...].astype(BF16), wb_ref[0:r0, :])
    y += _dot(ao_ref[...].astype(BF16), wb_ref[r0:r1, :])
    y += _dot(po_ref[...].astype(BF16), wb_ref[r1:, :])
    o_ref[...] = x_ref[...] + gt_ref[...] * y


def _outproj(x, mod, layer, rows_per_batch, ro, ao, po, w_out, tm):
    m, d = x.shape
    k = w_out.shape[1]
    _, _, gt = _mod_specs(mod, layer, 1, rows_per_batch, tm, 1)
    return pl.pallas_call(
        _outproj_kernel,
        out_shape=jax.ShapeDtypeStruct((m, d), F32),
        grid=(m // tm,),
        in_specs=[
            pl.BlockSpec((tm, d), lambda i: (i, 0)),
            gt,
            pl.BlockSpec((tm, RET_WIDTH), lambda i: (i, 0)),
            pl.BlockSpec((tm, ATT_GROUP), lambda i: (i, 0)),
            pl.BlockSpec((tm, POOL_WIDTH), lambda i: (i, 0)),
            pl.BlockSpec((None, k, d), lambda i: (layer, 0, 0), pipeline_mode=pl.Buffered(1)),
        ],
        out_specs=pl.BlockSpec((tm, d), lambda i: (i, 0)),
        scratch_shapes=[pltpu.VMEM((k, d), BF16)],
        compiler_params=_params("arbitrary"),
        name="out_proj",
    )(x, mod, ro, ao, po, w_out)


def _ret_rope_tables(pos):
    half = RET_DIM // 2
    freq = jnp.power(jnp.float32(RET_THETA), -jnp.arange(half, dtype=F32) / half)
    ang = pos[:, None] * freq[None, :]
    cos, sin = jnp.cos(ang), jnp.sin(ang)
    return jnp.concatenate([cos, cos], axis=-1), jnp.concatenate([-sin, sin], axis=-1)


def _att_rope_tables(pos):
    freq = jnp.power(jnp.float32(ROPE_THETA), -jnp.arange(ROPE_HALF, dtype=F32) / ROPE_HALF)
    ang = pos[:, None] * freq[None, :]
    cos, sin = jnp.cos(ang), jnp.sin(ang)
    s = pos.shape[0]
    rest = ATT_DIM - ROPE_DIMS
    c = jnp.concatenate([cos, cos, jnp.ones((s, rest), F32)], axis=-1)
    a = jnp.concatenate([-sin, jnp.zeros((s, ATT_DIM - ROPE_HALF), F32)], axis=-1)
    b = jnp.concatenate([jnp.zeros((s, ROPE_HALF), F32), sin, jnp.zeros((s, rest), F32)], axis=-1)
    tile = lambda t: jnp.tile(t, (1, 128 // ATT_DIM))
    return tile(c), tile(a), tile(b)


def _ret_rope(x, cos_t, sin_t):
    return x * cos_t + pltpu.roll(x, RET_DIM // 2, axis=1) * sin_t


def _att_rope(x, c, a, b):
    n = x.shape[-1]
    return x * c + pltpu.roll(x, n - ROPE_HALF, axis=1) * a + pltpu.roll(x, ROPE_HALF, axis=1) * b


def _head_norm(o, g):
    mu = jnp.mean(o, axis=-1, keepdims=True)
    oc = o - mu
    var = jnp.mean(oc * oc, axis=-1, keepdims=True)
    return oc * lax.rsqrt(var + EPS) * g


def _ret_kernel(zq_ref, zk_ref, zv_ref, zg_ref, cos_ref, sin_ref, gn_ref,
                ro_ref, so_ref, s_ref, din_ref, dq_ref, dk_ref, *, n_chunks, chunk):
    c = pl.program_id(1)

    @pl.when(c == 0)
    def _():
        s_ref[...] = jnp.zeros_like(s_ref)
        row = lax.broadcasted_iota(jnp.int32, (chunk, chunk), 0).astype(F32)
        col = lax.broadcasted_iota(jnp.int32, (chunk, chunk), 1).astype(F32)
        diff = row - col
        rowd = lax.broadcasted_iota(jnp.int32, (chunk, RET_DIM), 0).astype(F32)
        for h in range(RET_HEADS):
            lg = LOG_GAMMA[h]
            din_ref[h] = jnp.where(diff >= 0, jnp.exp(jnp.maximum(diff, 0.0) * lg), 0.0)
            dq_ref[h] = jnp.exp((rowd + 1.0) * lg)
            dk_ref[h] = jnp.exp((chunk - 1.0 - rowd) * lg)

    cos_t = cos_ref[...]
    sin_t = sin_ref[...]
    for h in range(RET_HEADS):
        cols = slice(h * RET_DIM, (h + 1) * RET_DIM)
        q = _ret_rope(zq_ref[:, cols], cos_t, sin_t)
        k = _ret_rope(zk_ref[:, cols], cos_t, sin_t) * (RET_DIM ** -0.5)
        qb = q.astype(BF16)
        kb = k.astype(BF16)
        vb = zv_ref[:, cols].astype(BF16)
        s_old = s_ref[h]
        a = _dot_nt(qb, kb) * din_ref[h]
        o = _dot(a.astype(BF16), vb) + _dot(qb, s_old.astype(BF16)) * dq_ref[h]
        kd_t = (k * dk_ref[h]).T.astype(BF16)
        s_new = math.exp(chunk * LOG_GAMMA[h]) * s_old + _dot(kd_t, vb)
        s_ref[h] = s_new

        on = _head_norm(o, gn_ref[:, cols])
        g = zg_ref[:, cols]
        ro_ref[:, cols] = (g * jax.nn.sigmoid(g) * on).astype(BF16)

        @pl.when(c == n_chunks - 1)
        def _():
            so_ref[h] = s_new


def _retention_prompt(z3, cos_t, sin_t, ret_norm_g, layer):
    b, s, _ = z3.shape
    chunk = math.gcd(s, RET_CHUNK)
    n_chunks = s // chunk
    zspec = lambda cb: pl.BlockSpec((None, chunk, RET_WIDTH), lambda i, c: (i, c, cb))
    return pl.pallas_call(
        functools.partial(_ret_kernel, n_chunks=n_chunks, chunk=chunk),
        out_shape=(jax.ShapeDtypeStruct((b, s, RET_WIDTH), BF16),
                   jax.ShapeDtypeStruct((b, RET_HEADS, RET_DIM, RET_DIM), F32)),
        grid=(b, n_chunks),
        in_specs=[
            zspec(COL_RQ // RET_WIDTH), zspec(COL_RK // RET_WIDTH),
            zspec(COL_RV // RET_WIDTH), zspec(COL_RG // RET_WIDTH),
            pl.BlockSpec((chunk, RET_DIM), lambda i, c: (c, 0)),
            pl.BlockSpec((chunk, RET_DIM), lambda i, c: (c, 0)),
            pl.BlockSpec((None, 1, RET_WIDTH), lambda i, c: (layer, 0, 0)),
        ],
        out_specs=(
            pl.BlockSpec((None, chunk, RET_WIDTH), lambda i, c: (i, c, 0)),
            pl.BlockSpec((None, RET_HEADS, RET_DIM, RET_DIM), lambda i, c: (i, 0, 0, 0)),
        ),
        scratch_shapes=[
            pltpu.VMEM((RET_HEADS, RET_DIM, RET_DIM), F32),
            pltpu.VMEM((RET_HEADS, chunk, chunk), F32),
            pltpu.VMEM((RET_HEADS, chunk, RET_DIM), F32),
            pltpu.VMEM((RET_HEADS, chunk, RET_DIM), F32),
        ],
        compiler_params=_params("parallel", "arbitrary"),
        name="retention",
    )(z3, z3, z3, z3, cos_t, sin_t, ret_norm_g)


def _ret_dec_kernel(z_ref, s0_ref, cos_ref, sin_ref, gn_ref, ro_ref, so_ref, *, nb):
    cos_t = cos_ref[...]
    sin_t = sin_ref[...]
    row = lax.broadcasted_iota(jnp.int32, (nb, RET_DIM), 0)
    for h in range(RET_HEADS):
        gamma = math.exp(LOG_GAMMA[h])
        q = _ret_rope(z_ref[:, COL_RQ + h * RET_DIM:COL_RQ + (h + 1) * RET_DIM], cos_t, sin_t)
        k = _ret_rope(z_ref[:, COL_RK + h * RET_DIM:COL_RK + (h + 1) * RET_DIM], cos_t, sin_t)
        k = k * (RET_DIM ** -0.5)
        v = z_ref[:, COL_RV + h * RET_DIM:COL_RV + (h + 1) * RET_DIM]
        qr, kr, vr = _bf16_round(q), _bf16_round(k), _bf16_round(v)
        qk = jnp.sum(qr * kr, axis=-1, keepdims=True)
        o = _bf16_round(qk) * vr
        cross = jnp.zeros((nb, RET_DIM), F32)
        for b in range(nb):
            s_old = s0_ref[b, h]
            res = _dot(qr.astype(BF16), s_old.astype(BF16))
            cross = cross + jnp.where(row == b, res, 0.0)
            k_col = jnp.broadcast_to(kr[b:b + 1, :], (RET_DIM, RET_DIM)).T
            so_ref[b, h] = gamma * s_old + k_col * vr[b:b + 1, :]
        o = o + cross * gamma
        cols = slice(h * RET_DIM, (h + 1) * RET_DIM)
        on = _head_norm(o, gn_ref[:, cols])
        g = z_ref[:, COL_RG + h * RET_DIM:COL_RG + (h + 1) * RET_DIM]
        ro_ref[:, cols] = g * jax.nn.sigmoid(g) * on


def _retention_decode(z, state, cos_t, sin_t, ret_norm_g, layer):
    nb = z.shape[0]
    sshape = (nb, RET_HEADS, RET_DIM, RET_DIM)
    return pl.pallas_call(
        functools.partial(_ret_dec_kernel, nb=nb),
        out_shape=(jax.ShapeDtypeStruct((nb, RET_WIDTH), F32),
                   jax.ShapeDtypeStruct(sshape, F32)),
        grid=(1,),
        in_specs=[
            pl.BlockSpec(z.shape, lambda i: (0, 0)),
            pl.BlockSpec((None,) + sshape, lambda i: (layer, 0, 0, 0, 0)),
            pl.BlockSpec((1, RET_DIM), lambda i: (0, 0)),
            pl.BlockSpec((1, RET_DIM), lambda i: (0, 0)),
            pl.BlockSpec((None, 1, RET_WIDTH), lambda i: (layer, 0, 0)),
        ],
        out_specs=(
            pl.BlockSpec((nb, RET_WIDTH), lambda i: (0, 0)),
            pl.BlockSpec(sshape, lambda i: (0, 0, 0, 0)),
        ),
        compiler_params=_params("arbitrary"),
        name="retention_decode",
    )(z, state, cos_t, sin_t, ret_norm_g)


ATT_HALF = 128
N_HALF = ATT_GROUP // ATT_HALF
HEADS_PER_HALF = ATT_HALF // ATT_DIM


def _half_cols(hf):
    return slice(hf * ATT_HALF, (hf + 1) * ATT_HALF)


def _att_kernel(*refs, seq):
    zq = refs[0:N_HALF]
    zk = refs[N_HALF:2 * N_HALF]
    zv = refs[2 * N_HALF:3 * N_HALF]
    c_ref, a_ref, b_ref, ao_ref, ko_ref, q_ref, k_ref, o_ref, lse_ref = refs[3 * N_HALF:]
    g = pl.program_id(1)
    qb = QUERY_BLOCK
    n_blocks = seq // qb

    def rope_body(i, carry):
        r = pl.multiple_of(i * qb, qb)
        rows = pl.ds(r, qb)
        c, a, b = c_ref[rows, :], a_ref[rows, :], b_ref[rows, :]
        for hf in range(N_HALF):
            q_ref[hf, rows, :] = _att_rope(zq[hf][rows, :], c, a, b) * (ATT_DIM ** -0.5)
            k_rot = _att_rope(zk[hf][rows, :], c, a, b)
            k_ref[hf, rows, :] = k_rot
            ko_ref[rows, _half_cols(hf)] = k_rot
        return carry

    lax.fori_loop(0, n_blocks, rope_body, 0)

    tq = lax.broadcasted_iota(jnp.int32, (qb, qb), 0)
    tk = lax.broadcasted_iota(jnp.int32, (qb, qb), 1)
    cur_valid = tk <= tq
    prev_valid = tk >= tq
    lane = lax.broadcasted_iota(jnp.int32, (qb, ATT_HALF), 1)
    head_masks = [(lane // ATT_DIM) == hh for hh in range(HEADS_PER_HALF)]

    def group_body(gi, dil):
        blocks_per_class = seq // (dil * qb)
        has_prev = blocks_per_class > 1

        def rows_of(start):
            return pl.ds(start, qb, stride=dil) if dil > 1 else pl.ds(start, qb)

        def block_body(t, carry):
            cls = t % dil
            blk = t // dil
            rows = rows_of(cls + dil * qb * blk)
            if has_prev:
                prows = rows_of(cls + dil * qb * jnp.maximum(blk - 1, 0))
                pmask = prev_valid & (blk > 0)
            for hf in range(N_HALF):
                q = q_ref[hf, rows, :].astype(BF16)
                k_cur = k_ref[hf, rows, :].astype(BF16)
                v_cur = zv[hf][rows, :].astype(BF16)
                if has_prev:
                    k_prev = k_ref[hf, prows, :].astype(BF16)
                    v_prev = zv[hf][prows, :].astype(BF16)
                o_acc = jnp.zeros((qb, ATT_HALF), F32)
                lse_acc = jnp.zeros((qb, ATT_HALF), F32)
                for hm in head_masks:
                    qh = jnp.where(hm, q, jnp.zeros_like(q))
                    s_cur = jnp.where(cur_valid, _dot_nt(qh, k_cur), MASK_VALUE)
                    m = jnp.max(s_cur, axis=-1, keepdims=True)
                    if has_prev:
                        s_prev = jnp.where(pmask, _dot_nt(qh, k_prev), MASK_VALUE)
                        m = jnp.maximum(m, jnp.max(s_prev, axis=-1, keepdims=True))
                    e_cur = jnp.exp(s_cur - m)
                    l = jnp.sum(e_cur, axis=-1, keepdims=True)
                    pv = _dot(e_cur.astype(BF16), v_cur)
                    if has_prev:
                        e_prev = jnp.exp(s_prev - m)
                        l = l + jnp.sum(e_prev, axis=-1, keepdims=True)
                        pv = pv + _dot(e_prev.astype(BF16), v_prev)
                    o_acc = jnp.where(hm, pv / l, o_acc)
                    lse_acc = jnp.where(hm, m + jnp.log(l), lse_acc)
                o_ref[gi, hf, rows, :] = o_acc
                lse_ref[gi, hf, rows, :] = lse_acc
            return carry

        lax.fori_loop(0, n_blocks, block_body, 0)

    for gi, (_, dil) in enumerate(ATT_PATTERNS):
        @pl.when(g == gi)
        def _(gi=gi, dil=dil):
            group_body(gi, dil)

    @pl.when(g == N_PAT - 1)
    def _():
        def merge_body(i, carry):
            r = pl.multiple_of(i * qb, qb)
            rows = pl.ds(r, qb)
            for hf in range(N_HALF):
                lses = [lse_ref[gi, hf, rows, :] for gi in range(N_PAT)]
                m = functools.reduce(jnp.maximum, lses)
                ws = [jnp.exp(l - m) for l in lses]
                den = functools.reduce(lambda x, y: x + y, ws)
                num = functools.reduce(lambda x, y: x + y,
                                       [w * o_ref[gi, hf, rows, :] for gi, w in enumerate(ws)])
                ao_ref[rows, _half_cols(hf)] = (num / den).astype(BF16)
            return carry

        lax.fori_loop(0, n_blocks, merge_body, 0)


def _attention_prompt(z3, tabs):
    b, s, _ = z3.shape
    assert all(s % (dil * QUERY_BLOCK) == 0 for _, dil in ATT_PATTERNS)
    zspec = lambda col, hf: pl.BlockSpec(
        (None, s, ATT_HALF), lambda i, g: (i, 0, col // ATT_HALF + N_HALF * g + hf))
    zspecs = [zspec(col, hf) for col in (COL_AQ, COL_AK, COL_AV) for hf in range(N_HALF)]
    tspec = pl.BlockSpec((s, ATT_HALF), lambda i, g: (0, 0), pipeline_mode=pl.Buffered(1))
    return pl.pallas_call(
        functools.partial(_att_kernel, seq=s),
        out_shape=(jax.ShapeDtypeStruct((b, s, ATT_GROUP), BF16),
                   jax.ShapeDtypeStruct((b, N_PAT, s, ATT_GROUP), F32)),
        grid=(b, N_PAT),
        in_specs=zspecs + [tspec, tspec, tspec],
        out_specs=(
            pl.BlockSpec((None, s, ATT_GROUP), lambda i, g: (i, 0, 0)),
            pl.BlockSpec((None, None, s, ATT_GROUP), lambda i, g: (i, g, 0, 0)),
        ),
        scratch_shapes=[
            pltpu.VMEM((N_HALF, s, ATT_HALF), F32),
            pltpu.VMEM((N_HALF, s, ATT_HALF), F32),
            pltpu.VMEM((N_PAT, N_HALF, s, ATT_HALF), F32),
            pltpu.VMEM((N_PAT, N_HALF, s, ATT_HALF), F32),
        ],
        compiler_params=_params("parallel", "arbitrary"),
        name="dilated_attention",
    )(*([z3] * (3 * N_HALF)), *tabs)


def _shift_append(src_ref, dst_ref, cols, new_row):
    w = src_ref.shape[0]
    step = 512
    body = w - SUBLANES
    for a in range(0, body, step):
        n = min(step, body - a)
        dst_ref[a:a + n, cols] = src_ref[a + 1:a + 1 + n, :]
    tail = pltpu.roll(src_ref[body:w, :], SUBLANES - 1, axis=0)
    row = lax.broadcasted_iota(jnp.int32, tail.shape, 0)
    dst_ref[body:w, cols] = jnp.where(row == SUBLANES - 1, new_row, tail)


def _att_dec_kernel(z_ref, c_ref, a_ref, b_ref, *refs):
    n = N_PAT * N_HALF
    kc_refs = refs[0:n]
    vc_refs = refs[n:2 * n]
    ao_ref = refs[2 * n]
    ko_refs = refs[2 * n + 1:2 * n + 1 + N_PAT]
    vo_refs = refs[2 * n + 1 + N_PAT:2 * n + 1 + 2 * N_PAT]
    bi = pl.program_id(0)
    rope = (c_ref[...], a_ref[...], b_ref[...])
    shape8 = (SUBLANES, ATT_HALF)
    row8 = lax.broadcasted_iota(jnp.int32, shape8, 0)
    lane8 = lax.broadcasted_iota(jnp.int32, shape8, 1)
    own_head = (lane8 // ATT_DIM) == row8

    for hf in range(N_HALF):
        outs, lses = [], []
        for gi, (win, dil) in enumerate(ATT_PATTERNS):
            off = gi * ATT_GROUP + hf * ATT_HALF
            zrow = lambda col: z_ref[:, col + off:col + off + ATT_HALF]
            q = _att_rope(zrow(COL_AQ), *rope) * (ATT_DIM ** -0.5)
            k_new = _att_rope(zrow(COL_AK), *rope)
            v_new = zrow(COL_AV)
            n_keys = win // dil
            kc, vc = kc_refs[gi * N_HALF + hf], vc_refs[gi * N_HALF + hf]
            rows = pl.ds(0, n_keys, stride=dil) if dil > 1 else pl.ds(0, n_keys)
            kd = kc[rows, :].astype(BF16)
            vd = vc[rows, :].astype(BF16)
            q8 = _bf16_round(jnp.where(own_head, jnp.broadcast_to(q, shape8), 0.0))
            s_old = _dot_nt(q8.astype(BF16), kd)
            s_new = jnp.sum(q8 * _bf16_round(k_new), axis=-1, keepdims=True)
            m = jnp.maximum(jnp.max(s_old, axis=-1, keepdims=True), s_new)
            e_old = jnp.exp(s_old - m)
            e_new = jnp.exp(s_new - m)
            l = jnp.sum(e_old, axis=-1, keepdims=True) + e_new
            pv = _dot(e_old.astype(BF16), vd) + _bf16_round(e_new) * _bf16_round(v_new)
            outs.append(pv / l)
            lses.append(m + jnp.log(l))
            _shift_append(kc, ko_refs[gi], _half_cols(hf), k_new)
            _shift_append(vc, vo_refs[gi], _half_cols(hf), v_new)

        m = functools.reduce(jnp.maximum, lses)
        ws = [jnp.exp(l - m) for l in lses]
        den = functools.reduce(lambda x, y: x + y, ws)
        num = functools.reduce(lambda x, y: x + y, [w * o for w, o in zip(ws, outs)])
        merged = jnp.where(own_head, num / den, 0.0)
        ao_ref[:, _half_cols(hf)] = jnp.sum(merged, axis=0, keepdims=True)


def _attention_decode(z, tabs, k_caches, v_caches, layer):
    nb = z.shape[0]
    cspec = lambda w, hf: pl.BlockSpec((None, None, w, ATT_HALF), lambda i: (layer, i, 0, hf))
    ospec = lambda w: pl.BlockSpec((None, w, ATT_GROUP), lambda i: (i, 0, 0))
    tspec = pl.BlockSpec((1, ATT_HALF), lambda i: (0, 0))
    widths = [kc.shape[2] for kc in k_caches]
    cspecs = [cspec(w, hf) for w in widths for hf in range(N_HALF)]
    cache_shapes = [jax.ShapeDtypeStruct((nb, w, ATT_GROUP), F32) for w in widths]
    halves = lambda caches: [c for c in caches for _ in range(N_HALF)]
    return pl.pallas_call(
        _att_dec_kernel,
        out_shape=tuple([jax.ShapeDtypeStruct((nb, 1, ATT_GROUP), F32)] + cache_shapes + cache_shapes),
        grid=(nb,),
        in_specs=[pl.BlockSpec((None, 1, z.shape[1]), lambda i: (i, 0, 0)), tspec, tspec, tspec]
                 + cspecs + cspecs,
        out_specs=tuple([pl.BlockSpec((None, 1, ATT_GROUP), lambda i: (i, 0, 0))]
                        + [ospec(w) for w in widths] + [ospec(w) for w in widths]),
        compiler_params=_params("arbitrary"),
        name="dilated_attention_decode",
    )(z.reshape(nb, 1, z.shape[1]), *tabs, *halves(k_caches), *halves(v_caches))


def _pool_kernel(u_ref, w_ref, sc_ref, po_ref, a_ref, b_ref, *, seq):
    g = pl.program_id(1)
    body = pl.ds(POOL_PAD, seq)

    def window_mean_minus_token(win):
        x = u_ref[...]
        a_ref[0:POOL_PAD, :] = jnp.zeros((POOL_PAD, POOL_GROUP), F32)
        b_ref[0:POOL_PAD, :] = jnp.zeros((POOL_PAD, POOL_GROUP), F32)
        a_ref[body, :] = x
        src, dst = a_ref, b_ref
        k = 1
        while k < win:
            dst[body, :] = src[body, :] + src[pl.ds(POOL_PAD - k, seq), :]
            src, dst = dst, src
            k *= 2
        t = lax.broadcasted_iota(jnp.int32, (seq, POOL_GROUP), 0)
        cnt = jnp.minimum(t + 1, win).astype(F32)
        pooled = src[body, :] / cnt - x
        y = _dot(pooled.astype(BF16), w_ref[...].astype(BF16)) * sc_ref[...]
        po_ref[...] = y.astype(BF16)

    for gi, win in enumerate(POOL_WINDOWS):
        @pl.when(g == gi)
        def _(win=win):
            window_mean_minus_token(win)


def _pool_prompt(z3, w_pool, pool_scale, layer):
    b, s, _ = z3.shape
    ng = len(POOL_WINDOWS)
    return pl.pallas_call(
        functools.partial(_pool_kernel, seq=s),
        out_shape=jax.ShapeDtypeStruct((b, s, POOL_WIDTH), BF16),
        grid=(b, ng),
        in_specs=[
            pl.BlockSpec((None, s, POOL_GROUP), lambda i, g: (i, 0, COL_PU // POOL_GROUP + g)),
            pl.BlockSpec((None, None, POOL_GROUP, POOL_GROUP), lambda i, g: (layer, g, 0, 0)),
            pl.BlockSpec((None, 1, POOL_GROUP), lambda i, g: (layer, 0, g)),
        ],
        out_specs=pl.BlockSpec((None, s, POOL_GROUP), lambda i, g: (i, 0, g)),
        scratch_shapes=[pltpu.VMEM((POOL_PAD + s, POOL_GROUP), F32),
                        pltpu.VMEM((POOL_PAD + s, POOL_GROUP), F32)],
        compiler_params=_params("parallel", "arbitrary"),
        name="pool_mixer",
    )(z3, w_pool, pool_scale)


def _pool_dec_kernel(z_ref, buf_ref, w_ref, sc_ref, po_ref, bo_ref, pooled_ref, *, nb):
    row = lax.broadcasted_iota(jnp.int32, (POOL_BUF, POOL_GROUP), 0)
    for b in range(nb):
        u = z_ref[b:b + 1, COL_PU:COL_PU + POOL_WIDTH]
        old = buf_ref[b]
        for gi, win in enumerate(POOL_WINDOWS):
            cols = slice(gi * POOL_GROUP, (gi + 1) * POOL_GROUP)
            tail = jnp.where(row >= POOL_BUF - (win - 1), old[:, cols], 0.0)
            total = jnp.sum(tail, axis=0, keepdims=True) + u[:, cols]
            pooled_ref[b:b + 1, cols] = total / float(win) - u[:, cols]
        bo_ref[b, 0:POOL_BUF - 1, :] = old[1:POOL_BUF, :]
        bo_ref[b, POOL_BUF - 1:POOL_BUF, :] = u
    for gi in range(len(POOL_WINDOWS)):
        cols = slice(gi * POOL_GROUP, (gi + 1) * POOL_GROUP)
        y = _dot(pooled_ref[:, cols].astype(BF16), w_ref[gi].astype(BF16))
        po_ref[:, cols] = y * sc_ref[:, cols]


def _pool_decode(z, cache_pool, w_pool, pool_scale, layer):
    nb = z.shape[0]
    ng = len(POOL_WINDOWS)
    bshape = (nb, POOL_BUF, POOL_WIDTH)
    return pl.pallas_call(
        functools.partial(_pool_dec_kernel, nb=nb),
        out_shape=(jax.ShapeDtypeStruct((nb, POOL_WIDTH), F32),
                   jax.ShapeDtypeStruct(bshape, F32)),
        grid=(1,),
        in_specs=[
            pl.BlockSpec(z.shape, lambda i: (0, 0)),
            pl.BlockSpec((None,) + bshape, lambda i: (layer, 0, 0, 0)),
            pl.BlockSpec((None, ng, POOL_GROUP, POOL_GROUP), lambda i: (layer, 0, 0, 0)),
            pl.BlockSpec((None, 1, POOL_WIDTH), lambda i: (layer, 0, 0)),
        ],
        out_specs=(pl.BlockSpec((nb, POOL_WIDTH), lambda i: (0, 0)),
                   pl.BlockSpec(bshape, lambda i: (0, 0, 0))),
        scratch_shapes=[pltpu.VMEM((nb, POOL_WIDTH), F32)],
        compiler_params=_params("arbitrary"),
        name="pool_mixer_decode",
    )(z, cache_pool, w_pool, pool_scale)


def _trunk_prompt(x, mod, p, final_g):
    b, s, d = x.shape
    depth = p["w_in"].shape[0]
    pos = jnp.arange(s, dtype=F32)
    ret_tabs = _ret_rope_tables(pos)
    att_tabs = _att_rope_tables(pos)
    keep = tuple(min(win, s) for win, _ in ATT_PATTERNS)
    tm = math.gcd(s, 1024)
    xf = x.reshape(b * s, d)
    rets, pools = [], []
    ks = [[] for _ in ATT_PATTERNS]
    vs = [[] for _ in ATT_PATTERNS]
    for l in range(depth):
        xf = _ffn(xf, mod, l, 0, s, p["norm_g"], p["w1_gate"], p["w1_up"], p["w1_down"],
                  None, tm, 256)
        z = _inproj(xf, mod, l, s, p["norm_g"], p["w_in"], tm, 256)
        z3 = z.reshape(b, s, IN_WIDTH)
        ro, ret_s = _retention_prompt(z3, *ret_tabs, p["ret_norm_g"], l)
        ao, k_rot = _attention_prompt(z3, att_tabs)
        po = _pool_prompt(z3, p["w_pool"], p["pool_scale"], l)
        xf = _outproj(xf, mod, l, s, ro.reshape(b * s, -1), ao.reshape(b * s, -1),
                      po.reshape(b * s, -1), p["w_out"], math.gcd(s, 512))
        xf = _ffn(xf, mod, l, 2, s, p["norm_g"], p["w2_gate"], p["w2_up"], p["w2_down"],
                  final_g if l == depth - 1 else None, tm, 256)
        rets.append(ret_s)
        pools.append(z3[:, s - POOL_BUF:, COL_PU:])
        for g in range(N_PAT):
            c0 = COL_AV + g * ATT_GROUP
            ks[g].append(k_rot[:, g, s - keep[g]:].reshape(b, keep[g], ATT_HEADS, ATT_DIM))
            vs[g].append(z3[:, s - keep[g]:, c0:c0 + ATT_GROUP].reshape(b, keep[g], ATT_HEADS, ATT_DIM))
    y = xf.reshape(b, s, d)
    return (y, jnp.stack(rets), [jnp.stack(k) for k in ks], [jnp.stack(v) for v in vs],
            jnp.stack(pools))


def _trunk_decode(x, mod, pos0, caches, p, final_g):
    nb, s, d = x.shape
    assert s == 1, "decode trunk handles one new token per batch row"
    depth = p["w_in"].shape[0]
    state_ret, cks, cvs, cpool = caches
    pos = pos0 + jnp.arange(s, dtype=F32)
    ret_tabs = _ret_rope_tables(pos)
    att_tabs = _att_rope_tables(pos)
    cks = [c.reshape(c.shape[0], nb, c.shape[2], ATT_GROUP) for c in cks]
    cvs = [c.reshape(c.shape[0], nb, c.shape[2], ATT_GROUP) for c in cvs]
    xf = x.reshape(nb, d)
    rets, pools = [], []
    ks = [[] for _ in ATT_PATTERNS]
    vs = [[] for _ in ATT_PATTERNS]
    for l in range(depth):
        xf = _ffn(xf, mod, l, 0, 1, p["norm_g"], p["w1_gate"], p["w1_up"], p["w1_down"],
                  None, nb, 512)
        z = _inproj(xf, mod, l, 1, p["norm_g"], p["w_in"], nb, 256)
        ro, ret_s = _retention_decode(z, state_ret, *ret_tabs, p["ret_norm_g"], l)
        att = _attention_decode(z, att_tabs, cks, cvs, l)
        ao, k_new, v_new = att[0].reshape(nb, ATT_GROUP), att[1:1 + N_PAT], att[1 + N_PAT:]
        po, pool_new = _pool_decode(z, cpool, p["w_pool"], p["pool_scale"], l)
        xf = _outproj(xf, mod, l, 1, ro, ao, po, p["w_out"], nb)
        xf = _ffn(xf, mod, l, 2, 1, p["norm_g"], p["w2_gate"], p["w2_up"], p["w2_down"],
                  final_g if l == depth - 1 else None, nb, 512)
        rets.append(ret_s)
        pools.append(pool_new)
        for g in range(N_PAT):
            ks[g].append(k_new[g].reshape(nb, -1, ATT_HEADS, ATT_DIM))
            vs[g].append(v_new[g].reshape(nb, -1, ATT_HEADS, ATT_DIM))
    y = xf.reshape(nb, s, d)
    return (y, jnp.stack(rets), [jnp.stack(k) for k in ks], [jnp.stack(v) for v in vs],
            jnp.stack(pools))


def kernel(x_prompt, x_sample, state_ret, cache_k_w128, cache_v_w128, cache_k_w512, cache_v_w512,
           cache_k_w2048, cache_v_w2048, cache_pool, c_prompt, c_sample, w_ada, b_ada, norm_g, w_in,
           ret_norm_g, w_pool, pool_scale, w_out, w1_gate, w1_up, w1_down, w2_gate, w2_up, w2_down,
           final_norm_g):
    depth, d = norm_g.shape[0], norm_g.shape[-1]
    n_pr, n_dec = c_prompt.shape[0], c_sample.shape[0]
    past_len = 16384
    p = {
        "norm_g": norm_g.reshape(depth, N_SUB, 1, d),
        "w_in": w_in,
        "ret_norm_g": ret_norm_g.reshape(depth, 1, RET_WIDTH),
        "w_pool": w_pool,
        "pool_scale": pool_scale.reshape(depth, 1, POOL_WIDTH),
        "w_out": w_out,
        "w1_gate": w1_gate, "w1_up": w1_up, "w1_down": w1_down,
        "w2_gate": w2_gate, "w2_up": w2_up, "w2_down": w2_down,
    }
    final_g = final_norm_g.reshape(1, d)

    pad = (-(n_dec + n_pr)) % SUBLANES
    c_all = jnp.concatenate([c_sample, c_prompt, jnp.zeros((pad, d), F32)], axis=0)
    mod_dec, mod_pr = _ada(c_all, n_dec, n_pr, w_ada, b_ada)

    y_p, ret_p, kp, vp, pool_p = _trunk_prompt(x_prompt, mod_pr, p, final_g)
    caches = (state_ret,
              (cache_k_w128, cache_k_w512, cache_k_w2048),
              (cache_v_w128, cache_v_w512, cache_v_w2048),
              cache_pool)
    y_s, ret_s, ks, vs, pool_s = _trunk_decode(x_sample, mod_dec, float(past_len), caches, p, final_g)
    return (y_p, y_s, ret_p, ret_s,
            kp[0], ks[0], vp[0], vs[0],
            kp[1], ks[1], vp[1], vs[1],
            kp[2], ks[2], vp[2], vs[2],
            pool_p, pool_s)
```

```python
import functools
import math

import jax
import jax.numpy as jnp
from jax import lax
from jax.experimental import pallas as pl
from jax.experimental.pallas import tpu as pltpu

F32 = jnp.float32
BF16 = jnp.bfloat16

RET_HEADS = 6
RET_DIM = 128
RET_WIDTH = RET_HEADS * RET_DIM
RET_CHUNK = 128
RET_THETA = 10000.0
ATT_HEADS = 4
ATT_DIM = 64
ATT_GROUP = ATT_HEADS * ATT_DIM
ATT_PATTERNS = ((128, 1), (512, 4), (2048, 16))
N_PAT = len(ATT_PATTERNS)
ATT_WIDTH = N_PAT * ATT_GROUP
ROPE_THETA = 500000.0
ROPE_DIMS = ATT_DIM // 4
ROPE_HALF = ROPE_DIMS // 2
QUERY_BLOCK = 128
POOL_WINDOWS = (2, 4, 8, 16)
POOL_GROUP = 128
POOL_WIDTH = len(POOL_WINDOWS) * POOL_GROUP
POOL_BUF = max(POOL_WINDOWS) - 1
POOL_PAD = 16
N_SUB = 3
PAST_LEN = 16384
HALF_STEP = 0.5
EPS = 1e-6
MASK_VALUE = -1e30

COL_RQ, COL_RK, COL_RV, COL_RG = 0, RET_WIDTH, 2 * RET_WIDTH, 3 * RET_WIDTH
COL_AQ = 4 * RET_WIDTH
COL_AK = COL_AQ + ATT_WIDTH
COL_AV = COL_AK + ATT_WIDTH
COL_PU = COL_AV + ATT_WIDTH
IN_WIDTH = COL_PU + POOL_WIDTH

V7X_VMEM_BYTES = 64 * 1024 * 1024
VMEM_LIMIT = V7X_VMEM_BYTES - 8 * 1024 * 1024
SUBLANES = 8

LOG_GAMMA = tuple(math.log1p(-(2.0 ** (-5.0 - h))) for h in range(RET_HEADS))


def _params(*sem):
    return pltpu.CompilerParams(dimension_semantics=sem, vmem_limit_bytes=VMEM_LIMIT)


def _dot(a, b):
    return jnp.dot(a, b, preferred_element_type=F32)


def _dot_nt(a, b):
    return lax.dot_general(a, b, (((1,), (1,)), ((), ())), preferred_element_type=F32)


def _bf16_round(x):
    return x.astype(BF16).astype(F32)


def _ada_kernel(c_ref, w_ref, b_ref, od_ref, op_ref, *, n_dec, n_pr):
    c = c_ref[...]
    a = (c * jax.nn.sigmoid(c)).astype(BF16)
    res = _dot(a, w_ref[...].astype(BF16)) + b_ref[...]
    od_ref[...] = res[0:n_dec]
    for b in range(n_pr):
        op_ref[b] = res[n_dec + b:n_dec + b + 1]


def _ada(c_all, n_dec, n_pr, w_ada, b_ada, tn=1024):
    depth, d, n = w_ada.shape
    rows = c_all.shape[0]
    per = d // tn
    return pl.pallas_call(
        functools.partial(_ada_kernel, n_dec=n_dec, n_pr=n_pr),
        out_shape=(jax.ShapeDtypeStruct((depth, N_SUB * 3, n_dec, d), F32),
                   jax.ShapeDtypeStruct((depth, N_SUB * 3, n_pr, 1, d), F32)),
        grid=(depth, n // tn),
        in_specs=[
            pl.BlockSpec((rows, d), lambda l, j: (0, 0)),
            pl.BlockSpec((None, d, tn), lambda l, j: (l, 0, j)),
            pl.BlockSpec((None, 1, tn), lambda l, j: (l, 0, j)),
        ],
        out_specs=(
            pl.BlockSpec((None, None, n_dec, tn), lambda l, j: (l, j // per, 0, j % per)),
            pl.BlockSpec((None, None, n_pr, 1, tn), lambda l, j: (l, j // per, 0, 0, j % per)),
        ),
        compiler_params=_params("arbitrary", "arbitrary"),
        name="ada_mod",
    )(c_all, w_ada, b_ada.reshape(depth, 1, n))


def _rmsnorm(x, g):
    ms = jnp.mean(x * x, axis=-1, keepdims=True)
    return x * lax.rsqrt(ms + EPS) * g


def _row_chunks(tm):
    rc = min(tm, 128)
    return rc, tm // rc


def _mod_rows(ref, r, rc):
    return ref[...] if ref.shape[0] == 1 else ref[pl.ds(r, rc), :]


def _prenorm_to(h_ref, x_ref, g_ref, sh_ref, sc_ref):
    rc, n = _row_chunks(x_ref.shape[0])

    def body(i, carry):
        r = pl.multiple_of(i * rc, rc)
        y = _rmsnorm(x_ref[pl.ds(r, rc), :], g_ref[...])
        h = y * (1.0 + _mod_rows(sc_ref, r, rc)) + _mod_rows(sh_ref, r, rc)
        h_ref[pl.ds(r, rc), :] = h.astype(BF16)
        return carry

    lax.fori_loop(0, n, body, 0)


def _mod_specs(mod, layer, sub, rows_per_batch, tm, grid_rank):
    d = mod.shape[-1]
    specs = []
    for k in range(3):
        j = sub * 3 + k
        if mod.ndim == 5:
            if grid_rank == 2:
                idx = (lambda j: lambda i, f: (layer, j, (i * tm) // rows_per_batch, 0, 0))(j)
            else:
                idx = (lambda j: lambda i: (layer, j, (i * tm) // rows_per_batch, 0, 0))(j)
            specs.append(pl.BlockSpec((None, None, None, 1, d), idx))
        else:
            if grid_rank == 2:
                idx = (lambda j: lambda i, f: (layer, j, 0, 0))(j)
            else:
                idx = (lambda j: lambda i: (layer, j, 0, 0))(j)
            specs.append(pl.BlockSpec((None, None, tm, d), idx))
    return specs


def _ffn_kernel(x_ref, sh_ref, sc_ref, gt_ref, g_ref, wg_ref, wu_ref, wd_ref, *rest,
                n_f, final_norm, emit_bf16):
    rest = list(rest)
    fg_ref = rest.pop(0) if final_norm else None
    o_ref = rest.pop(0)
    wb_refs = [rest.pop(0) for _ in range(3)] if emit_bf16 else None
    h_ref, = rest
    f = pl.program_id(1)

    @pl.when(f == 0)
    def _():
        _prenorm_to(h_ref, x_ref, g_ref, sh_ref, sc_ref)
        o_ref[...] = jnp.zeros_like(o_ref)

    wg = wg_ref[...].astype(BF16)
    wu = wu_ref[...].astype(BF16)
    wd = wd_ref[...].astype(BF16)
    if emit_bf16:
        wb_refs[0][...] = wg
        wb_refs[1][...] = wu
        wb_refs[2][...] = wd
    h = h_ref[...]
    gate = _dot(h, wg)
    up = _dot(h, wu)
    act = (gate * jax.nn.sigmoid(gate) * up).astype(BF16)
    d = o_ref.shape[1]
    dc = min(d, 512)
    for c0 in range(0, d, dc):
        o_ref[:, c0:c0 + dc] += _dot(act, wd[:, c0:c0 + dc])

    @pl.when(f == n_f - 1)
    def _():
        rc, n = _row_chunks(x_ref.shape[0])

        def body(i, carry):
            r = pl.multiple_of(i * rc, rc)
            rows = pl.ds(r, rc)
            out = x_ref[rows, :] + HALF_STEP * _mod_rows(gt_ref, r, rc) * o_ref[rows, :]
            if final_norm:
                out = _rmsnorm(out, fg_ref[...])
            o_ref[rows, :] = out
            return carry

        lax.fori_loop(0, n, body, 0)


def _weight_spec(w, layer, block, index):
    if w.ndim == 3:
        return pl.BlockSpec((None,) + block, lambda i, j: (layer,) + index(i, j))
    return pl.BlockSpec(block, index)


def _ffn(x, mod, layer, sub, rows_per_batch, norm_g, wg, wu, wd, final_g, tm, tf, emit_bf16=False):
    m, d = x.shape
    d_ff = wg.shape[-1]
    assert m % tm == 0 and d_ff % tf == 0
    assert not emit_bf16 or m == tm, "each weight block must be visited exactly once"
    n_f = d_ff // tf
    final_norm = final_g is not None
    col_block = lambda i, j: (0, j)
    row_block = lambda i, j: (j, 0)
    in_specs = [pl.BlockSpec((tm, d), lambda i, j: (i, 0))]
    in_specs += _mod_specs(mod, layer, sub, rows_per_batch, tm, 2)
    in_specs += [
        pl.BlockSpec((None, None, 1, d), lambda i, j: (layer, sub, 0, 0)),
        _weight_spec(wg, layer, (d, tf), col_block),
        _weight_spec(wu, layer, (d, tf), col_block),
        _weight_spec(wd, layer, (tf, d), row_block),
    ]
    args = [x, mod, mod, mod, norm_g, wg, wu, wd]
    if final_norm:
        in_specs.append(pl.BlockSpec((1, d), lambda i, j: (0, 0)))
        args.append(final_g)
    out_shape = [jax.ShapeDtypeStruct((m, d), F32)]
    out_specs = [pl.BlockSpec((tm, d), lambda i, j: (i, 0))]
    if emit_bf16:
        out_shape += [jax.ShapeDtypeStruct((d, d_ff), BF16), jax.ShapeDtypeStruct((d, d_ff), BF16),
                      jax.ShapeDtypeStruct((d_ff, d), BF16)]
        out_specs += [pl.BlockSpec((d, tf), col_block), pl.BlockSpec((d, tf), col_block),
                      pl.BlockSpec((tf, d), row_block)]
    out = pl.pallas_call(
        functools.partial(_ffn_kernel, n_f=n_f, final_norm=final_norm, emit_bf16=emit_bf16),
        out_shape=tuple(out_shape),
        grid=(m // tm, n_f),
        in_specs=in_specs,
        out_specs=tuple(out_specs),
        scratch_shapes=[pltpu.VMEM((tm, d), BF16)],
        compiler_params=_params("parallel", "arbitrary"),
        name="ffn",
    )(*args)
    return out if emit_bf16 else out[0]


def _inproj_kernel(x_ref, sh_ref, sc_ref, g_ref, w_ref, o_ref, *rest, emit_bf16):
    wb_ref, h_ref = rest if emit_bf16 else (None,) + rest

    @pl.when(pl.program_id(1) == 0)
    def _():
        _prenorm_to(h_ref, x_ref, g_ref, sh_ref, sc_ref)

    if emit_bf16:
        wb_ref[...] = w_ref[...].astype(BF16)
    h = h_ref[...]
    tn = o_ref.shape[1]
    nc = min(tn, 512)
    for c0 in range(0, tn, nc):
        c1 = min(c0 + nc, tn)
        o_ref[:, c0:c1] = _dot(h, w_ref[:, c0:c1].astype(BF16))


def _inproj(x, mod, layer, rows_per_batch, norm_g, w_in, tm, tn, emit_bf16=False):
    m, d = x.shape
    n = w_in.shape[-1]
    assert m % tm == 0 and n % tn == 0
    assert not emit_bf16 or m == tm, "each weight block must be visited exactly once"
    sh, sc, _ = _mod_specs(mod, layer, 1, rows_per_batch, tm, 2)
    col_block = lambda i, j: (0, j)
    out_shape = [jax.ShapeDtypeStruct((m, n), F32)]
    out_specs = [pl.BlockSpec((tm, tn), lambda i, j: (i, j))]
    if emit_bf16:
        out_shape.append(jax.ShapeDtypeStruct((d, n), BF16))
        out_specs.append(pl.BlockSpec((d, tn), col_block))
    out = pl.pallas_call(
        functools.partial(_inproj_kernel, emit_bf16=emit_bf16),
        out_shape=tuple(out_shape),
        grid=(m // tm, n // tn),
        in_specs=[
            pl.BlockSpec((tm, d), lambda i, j: (i, 0)),
            sh, sc,
            pl.BlockSpec((None, None, 1, d), lambda i, j: (layer, 1, 0, 0)),
            _weight_spec(w_in, layer, (d, tn), col_block),
        ],
        out_specs=tuple(out_specs),
        scratch_shapes=[pltpu.VMEM((tm, d), BF16)],
        compiler_params=_params("parallel", "arbitrary"),
        name="in_proj",
    )(x, mod, mod, norm_g, w_in)
    return out if emit_bf16 else out[0]


def _outproj_kernel(x_ref, gt_ref, ro_ref, ao_ref, po_ref, w_ref, o_ref, wb_ref):
    @pl.when(pl.program_id(0) == 0)
    def _():
        wb_ref[...] = w_ref[...].astype(BF16)

    r0, r1 = RET_WIDTH, RET_WIDTH + ATT_GROUP
    y = _dot(ro_ref[...].astype(BF16), wb_ref[0:r0, :])
    y += _dot(ao_ref[...].astype(BF16), wb_ref[r0:r1, :])
    y += _dot(po_ref[...].astype(BF16), wb_ref[r1:, :])
    o_ref[...] = x_ref[...] + gt_ref[...] * y


def _outproj(x, mod, layer, rows_per_batch, ro, ao, po, w_out, tm):
    m, d = x.shape
    k = w_out.shape[1]
    _, _, gt = _mod_specs(mod, layer, 1, rows_per_batch, tm, 1)
    return pl.pallas_call(
        _outproj_kernel,
        out_shape=jax.ShapeDtypeStruct((m, d), F32),
        grid=(m // tm,),
        in_specs=[
            pl.BlockSpec((tm, d), lambda i: (i, 0)),
            gt,
            pl.BlockSpec((tm, RET_WIDTH), lambda i: (i, 0)),
            pl.BlockSpec((tm, ATT_GROUP), lambda i: (i, 0)),
            pl.BlockSpec((tm, POOL_WIDTH), lambda i: (i, 0)),
            pl.BlockSpec((None, k, d), lambda i: (layer, 0, 0), pipeline_mode=pl.Buffered(1)),
        ],
        out_specs=pl.BlockSpec((tm, d), lambda i: (i, 0)),
        scratch_shapes=[pltpu.VMEM((k, d), BF16)],
        compiler_params=_params("arbitrary"),
        name="out_proj",
    )(x, mod, ro, ao, po, w_out)


def _ret_rope_tables(pos):
    half = RET_DIM // 2
    freq = jnp.power(jnp.float32(RET_THETA), -jnp.arange(half, dtype=F32) / half)
    ang = pos[:, None] * freq[None, :]
    cos, sin = jnp.cos(ang), jnp.sin(ang)
    return jnp.concatenate([cos, cos], axis=-1), jnp.concatenate([-sin, sin], axis=-1)


def _att_rope_tables(pos):
    freq = jnp.power(jnp.float32(ROPE_THETA), -jnp.arange(ROPE_HALF, dtype=F32) / ROPE_HALF)
    ang = pos[:, None] * freq[None, :]
    cos, sin = jnp.cos(ang), jnp.sin(ang)
    s = pos.shape[0]
    rest = ATT_DIM - ROPE_DIMS
    c = jnp.concatenate([cos, cos, jnp.ones((s, rest), F32)], axis=-1)
    a = jnp.concatenate([-sin, jnp.zeros((s, ATT_DIM - ROPE_HALF), F32)], axis=-1)
    b = jnp.concatenate([jnp.zeros((s, ROPE_HALF), F32), sin, jnp.zeros((s, rest), F32)], axis=-1)
    tile = lambda t: jnp.tile(t, (1, 128 // ATT_DIM))
    return tile(c), tile(a), tile(b)


def _ret_rope(x, cos_t, sin_t):
    return x * cos_t + pltpu.roll(x, RET_DIM // 2, axis=1) * sin_t


def _att_rope(x, c, a, b):
    n = x.shape[-1]
    return x * c + pltpu.roll(x, n - ROPE_HALF, axis=1) * a + pltpu.roll(x, ROPE_HALF, axis=1) * b


def _head_norm(o, g):
    mu = jnp.mean(o, axis=-1, keepdims=True)
    oc = o - mu
    var = jnp.mean(oc * oc, axis=-1, keepdims=True)
    return oc * lax.rsqrt(var + EPS) * g


def _ret_kernel(zq_ref, zk_ref, zv_ref, zg_ref, cos_ref, sin_ref, gn_ref,
                ro_ref, so_ref, s_ref, din_ref, dq_ref, dk_ref, *, n_chunks, chunk):
    c = pl.program_id(1)

    @pl.when(c == 0)
    def _():
        s_ref[...] = jnp.zeros_like(s_ref)
        row = lax.broadcasted_iota(jnp.int32, (chunk, chunk), 0).astype(F32)
        col = lax.broadcasted_iota(jnp.int32, (chunk, chunk), 1).astype(F32)
        diff = row - col
        rowd = lax.broadcasted_iota(jnp.int32, (chunk, RET_DIM), 0).astype(F32)
        for h in range(RET_HEADS):
            lg = LOG_GAMMA[h]
            din_ref[h] = jnp.where(diff >= 0, jnp.exp(jnp.maximum(diff, 0.0) * lg), 0.0)
            dq_ref[h] = jnp.exp((rowd + 1.0) * lg)
            dk_ref[h] = jnp.exp((chunk - 1.0 - rowd) * lg)

    cos_t = cos_ref[...]
    sin_t = sin_ref[...]
    for h in range(RET_HEADS):
        cols = slice(h * RET_DIM, (h + 1) * RET_DIM)
        q = _ret_rope(zq_ref[:, cols], cos_t, sin_t)
        k = _ret_rope(zk_ref[:, cols], cos_t, sin_t) * (RET_DIM ** -0.5)
        qb = q.astype(BF16)
        kb = k.astype(BF16)
        vb = zv_ref[:, cols].astype(BF16)
        s_old = s_ref[h]
        a = _dot_nt(qb, kb) * din_ref[h]
        o = _dot(a.astype(BF16), vb) + _dot(qb, s_old.astype(BF16)) * dq_ref[h]
        kd_t = (k * dk_ref[h]).T.astype(BF16)
        s_new = math.exp(chunk * LOG_GAMMA[h]) * s_old + _dot(kd_t, vb)
        s_ref[h] = s_new

        on = _head_norm(o, gn_ref[:, cols])
        g = zg_ref[:, cols]
        ro_ref[:, cols] = (g * jax.nn.sigmoid(g) * on).astype(BF16)
        so_ref[h] = s_new


def _retention_prompt(z3, cos_t, sin_t, ret_norm_g, layer):
    b, s, _ = z3.shape
    chunk = math.gcd(s, RET_CHUNK)
    n_chunks = s // chunk
    zspec = lambda cb: pl.BlockSpec((None, chunk, RET_WIDTH), lambda i, c: (i, c, cb))
    return pl.pallas_call(
        functools.partial(_ret_kernel, n_chunks=n_chunks, chunk=chunk),
        out_shape=(jax.ShapeDtypeStruct((b, s, RET_WIDTH), BF16),
                   jax.ShapeDtypeStruct((b, RET_HEADS, RET_DIM, RET_DIM), F32)),
        grid=(b, n_chunks),
        in_specs=[
            zspec(COL_RQ // RET_WIDTH), zspec(COL_RK // RET_WIDTH),
            zspec(COL_RV // RET_WIDTH), zspec(COL_RG // RET_WIDTH),
            pl.BlockSpec((chunk, RET_DIM), lambda i, c: (c, 0)),
            pl.BlockSpec((chunk, RET_DIM), lambda i, c: (c, 0)),
            pl.BlockSpec((None, 1, RET_WIDTH), lambda i, c: (layer, 0, 0)),
        ],
        out_specs=(
            pl.BlockSpec((None, chunk, RET_WIDTH), lambda i, c: (i, c, 0)),
            pl.BlockSpec((None, RET_HEADS, RET_DIM, RET_DIM), lambda i, c: (i, 0, 0, 0)),
        ),
        scratch_shapes=[
            pltpu.VMEM((RET_HEADS, RET_DIM, RET_DIM), F32),
            pltpu.VMEM((RET_HEADS, chunk, chunk), F32),
            pltpu.VMEM((RET_HEADS, chunk, RET_DIM), F32),
            pltpu.VMEM((RET_HEADS, chunk, RET_DIM), F32),
        ],
        compiler_params=_params("parallel", "arbitrary"),
        name="retention",
    )(z3, z3, z3, z3, cos_t, sin_t, ret_norm_g)


def _ret_dec_kernel(z_ref, s0_ref, cos_ref, sin_ref, gn_ref, ro_ref, so_ref, *, nb):
    cos_t = cos_ref[...]
    sin_t = sin_ref[...]
    row = lax.broadcasted_iota(jnp.int32, (nb, RET_DIM), 0)
    for h in range(RET_HEADS):
        gamma = math.exp(LOG_GAMMA[h])
        q = _ret_rope(z_ref[:, COL_RQ + h * RET_DIM:COL_RQ + (h + 1) * RET_DIM], cos_t, sin_t)
        k = _ret_rope(z_ref[:, COL_RK + h * RET_DIM:COL_RK + (h + 1) * RET_DIM], cos_t, sin_t)
        k = k * (RET_DIM ** -0.5)
        v = z_ref[:, COL_RV + h * RET_DIM:COL_RV + (h + 1) * RET_DIM]
        qr, kr, vr = _bf16_round(q), _bf16_round(k), _bf16_round(v)
        qk = jnp.sum(qr * kr, axis=-1, keepdims=True)
        o = _bf16_round(qk) * vr
        cross = jnp.zeros((nb, RET_DIM), F32)
        for b in range(nb):
            s_old = s0_ref[b, h]
            res = _dot(qr.astype(BF16), s_old.astype(BF16))
            cross = cross + jnp.where(row == b, res, 0.0)
            k_col = jnp.broadcast_to(kr[b:b + 1, :], (RET_DIM, RET_DIM)).T
            so_ref[b, h] = gamma * s_old + k_col * vr[b:b + 1, :]
        o = o + cross * gamma
        cols = slice(h * RET_DIM, (h + 1) * RET_DIM)
        on = _head_norm(o, gn_ref[:, cols])
        g = z_ref[:, COL_RG + h * RET_DIM:COL_RG + (h + 1) * RET_DIM]
        ro_ref[:, cols] = g * jax.nn.sigmoid(g) * on


def _retention_decode(z, state, cos_t, sin_t, ret_norm_g, layer):
    nb = z.shape[0]
    sshape = (nb, RET_HEADS, RET_DIM, RET_DIM)
    return pl.pallas_call(
        functools.partial(_ret_dec_kernel, nb=nb),
        out_shape=(jax.ShapeDtypeStruct((nb, RET_WIDTH), F32),
                   jax.ShapeDtypeStruct(sshape, F32)),
        grid=(1,),
        in_specs=[
            pl.BlockSpec(z.shape, lambda i: (0, 0)),
            pl.BlockSpec((None,) + sshape, lambda i: (layer, 0, 0, 0, 0)),
            pl.BlockSpec((1, RET_DIM), lambda i: (0, 0)),
            pl.BlockSpec((1, RET_DIM), lambda i: (0, 0)),
            pl.BlockSpec((None, 1, RET_WIDTH), lambda i: (layer, 0, 0)),
        ],
        out_specs=(
            pl.BlockSpec((nb, RET_WIDTH), lambda i: (0, 0)),
            pl.BlockSpec(sshape, lambda i: (0, 0, 0, 0)),
        ),
        compiler_params=_params("arbitrary"),
        name="retention_decode",
    )(z, state, cos_t, sin_t, ret_norm_g)


ATT_HALF = 128
N_HALF = ATT_GROUP // ATT_HALF
HEADS_PER_HALF = ATT_HALF // ATT_DIM


def _half_cols(hf):
    return slice(hf * ATT_HALF, (hf + 1) * ATT_HALF)


def _att_kernel(*refs, seq):
    zq = refs[0:N_HALF]
    zk = refs[N_HALF:2 * N_HALF]
    zv = refs[2 * N_HALF:3 * N_HALF]
    c_ref, a_ref, b_ref, ao_ref, ko_ref, q_ref, k_ref, o_ref, lse_ref = refs[3 * N_HALF:]
    g = pl.program_id(1)
    qb = QUERY_BLOCK
    n_blocks = seq // qb

    def rope_body(i, carry):
        r = pl.multiple_of(i * qb, qb)
        rows = pl.ds(r, qb)
        c, a, b = c_ref[rows, :], a_ref[rows, :], b_ref[rows, :]
        for hf in range(N_HALF):
            q_ref[hf, rows, :] = _att_rope(zq[hf][rows, :], c, a, b) * (ATT_DIM ** -0.5)
            k_rot = _att_rope(zk[hf][rows, :], c, a, b)
            k_ref[hf, rows, :] = k_rot
            ko_ref[rows, _half_cols(hf)] = k_rot
        return carry

    lax.fori_loop(0, n_blocks, rope_body, 0)

    tq = lax.broadcasted_iota(jnp.int32, (qb, qb), 0)
    tk = lax.broadcasted_iota(jnp.int32, (qb, qb), 1)
    cur_valid = tk <= tq
    prev_valid = tk >= tq
    lane = lax.broadcasted_iota(jnp.int32, (qb, ATT_HALF), 1)
    head_masks = [(lane // ATT_DIM) == hh for hh in range(HEADS_PER_HALF)]

    def group_body(gi, dil):
        blocks_per_class = seq // (dil * qb)
        has_prev = blocks_per_class > 1

        def rows_of(start):
            return pl.ds(start, qb, stride=dil) if dil > 1 else pl.ds(start, qb)

        def block_body(t, carry):
            cls = t % dil
            blk = t // dil
            rows = rows_of(cls + dil * qb * blk)
            if has_prev:
                prows = rows_of(cls + dil * qb * jnp.maximum(blk - 1, 0))
                pmask = prev_valid & (blk > 0)
            for hf in range(N_HALF):
                q = q_ref[hf, rows, :].astype(BF16)
                k_cur = k_ref[hf, rows, :].astype(BF16)
                v_cur = zv[hf][rows, :].astype(BF16)
                if has_prev:
                    k_prev = k_ref[hf, prows, :].astype(BF16)
                    v_prev = zv[hf][prows, :].astype(BF16)
                o_acc = jnp.zeros((qb, ATT_HALF), F32)
                lse_acc = jnp.zeros((qb, ATT_HALF), F32)
                for hm in head_masks:
                    qh = jnp.where(hm, q, jnp.zeros_like(q))
                    s_cur = jnp.where(cur_valid, _dot_nt(qh, k_cur), MASK_VALUE)
                    m = jnp.max(s_cur, axis=-1, keepdims=True)
                    if has_prev:
                        s_prev = jnp.where(pmask, _dot_nt(qh, k_prev), MASK_VALUE)
                        m = jnp.maximum(m, jnp.max(s_prev, axis=-1, keepdims=True))
                    e_cur = jnp.exp(s_cur - m)
                    l = jnp.sum(e_cur, axis=-1, keepdims=True)
                    pv = _dot(e_cur.astype(BF16), v_cur)
                    if has_prev:
                        e_prev = jnp.exp(s_prev - m)
                        l = l + jnp.sum(e_prev, axis=-1, keepdims=True)
                        pv = pv + _dot(e_prev.astype(BF16), v_prev)
                    o_acc = jnp.where(hm, pv / l, o_acc)
                    lse_acc = jnp.where(hm, m + jnp.log(l), lse_acc)
                o_ref[gi, hf, rows, :] = o_acc
                lse_ref[gi, hf, rows, :] = lse_acc
            return carry

        lax.fori_loop(0, n_blocks, block_body, 0, unroll=2)

    for gi, (_, dil) in enumerate(ATT_PATTERNS):
        @pl.when(g == gi)
        def _(gi=gi, dil=dil):
            group_body(gi, dil)

    @pl.when(g == N_PAT - 1)
    def _():
        def merge_body(i, carry):
            r = pl.multiple_of(i * qb, qb)
            rows = pl.ds(r, qb)
            for hf in range(N_HALF):
                lses = [lse_ref[gi, hf, rows, :] for gi in range(N_PAT)]
                m = functools.reduce(jnp.maximum, lses)
                ws = [jnp.exp(l - m) for l in lses]
                den = functools.reduce(lambda x, y: x + y, ws)
                num = functools.reduce(lambda x, y: x + y,
                                       [w * o_ref[gi, hf, rows, :] for gi, w in enumerate(ws)])
                ao_ref[rows, _half_cols(hf)] = (num / den).astype(BF16)
            return carry

        lax.fori_loop(0, n_blocks, merge_body, 0)


def _attention_prompt(z3, tabs):
    b, s, _ = z3.shape
    assert all(s % (dil * QUERY_BLOCK) == 0 for _, dil in ATT_PATTERNS)
    zspec = lambda col, hf: pl.BlockSpec(
        (None, s, ATT_HALF), lambda i, g: (i, 0, col // ATT_HALF + N_HALF * g + hf))
    zspecs = [zspec(col, hf) for col in (COL_AQ, COL_AK, COL_AV) for hf in range(N_HALF)]
    tspec = pl.BlockSpec((s, ATT_HALF), lambda i, g: (0, 0), pipeline_mode=pl.Buffered(1))
    return pl.pallas_call(
        functools.partial(_att_kernel, seq=s),
        out_shape=(jax.ShapeDtypeStruct((b, s, ATT_GROUP), BF16),
                   jax.ShapeDtypeStruct((b, N_PAT, s, ATT_GROUP), F32)),
        grid=(b, N_PAT),
        in_specs=zspecs + [tspec, tspec, tspec],
        out_specs=(
            pl.BlockSpec((None, s, ATT_GROUP), lambda i, g: (i, 0, 0)),
            pl.BlockSpec((None, None, s, ATT_GROUP), lambda i, g: (i, g, 0, 0)),
        ),
        scratch_shapes=[
            pltpu.VMEM((N_HALF, s, ATT_HALF), F32),
            pltpu.VMEM((N_HALF, s, ATT_HALF), F32),
            pltpu.VMEM((N_PAT, N_HALF, s, ATT_HALF), F32),
            pltpu.VMEM((N_PAT, N_HALF, s, ATT_HALF), F32),
        ],
        compiler_params=_params("parallel", "arbitrary"),
        name="dilated_attention",
    )(*([z3] * (3 * N_HALF)), *tabs)


def _shift_append(src_ref, dst_ref, cols, new_row):
    w = src_ref.shape[0]
    step = 512
    body = w - SUBLANES
    for a in range(0, body, step):
        n = min(step, body - a)
        dst_ref[a:a + n, cols] = src_ref[a + 1:a + 1 + n, :]
    tail = pltpu.roll(src_ref[body:w, :], SUBLANES - 1, axis=0)
    row = lax.broadcasted_iota(jnp.int32, tail.shape, 0)
    dst_ref[body:w, cols] = jnp.where(row == SUBLANES - 1, new_row, tail)


def _att_dec_kernel(z_ref, c_ref, a_ref, b_ref, *refs):
    n = N_PAT * N_HALF
    kc_refs = refs[0:n]
    vc_refs = refs[n:2 * n]
    ao_ref = refs[2 * n]
    ko_refs = refs[2 * n + 1:2 * n + 1 + N_PAT]
    vo_refs = refs[2 * n + 1 + N_PAT:2 * n + 1 + 2 * N_PAT]
    bi = pl.program_id(0)
    rope = (c_ref[...], a_ref[...], b_ref[...])
    shape8 = (SUBLANES, ATT_HALF)
    row8 = lax.broadcasted_iota(jnp.int32, shape8, 0)
    lane8 = lax.broadcasted_iota(jnp.int32, shape8, 1)
    own_head = (lane8 // ATT_DIM) == row8

    for hf in range(N_HALF):
        outs, lses = [], []
        for gi, (win, dil) in enumerate(ATT_PATTERNS):
            off = gi * ATT_GROUP + hf * ATT_HALF
            zrow = lambda col: z_ref[:, col + off:col + off + ATT_HALF]
            q = _att_rope(zrow(COL_AQ), *rope) * (ATT_DIM ** -0.5)
            k_new = _att_rope(zrow(COL_AK), *rope)
            v_new = zrow(COL_AV)
            n_keys = win // dil
            kc, vc = kc_refs[gi * N_HALF + hf], vc_refs[gi * N_HALF + hf]
            rows = pl.ds(0, n_keys, stride=dil) if dil > 1 else pl.ds(0, n_keys)
            kd = kc[rows, :].astype(BF16)
            vd = vc[rows, :].astype(BF16)
            q8 = _bf16_round(jnp.where(own_head, jnp.broadcast_to(q, shape8), 0.0))
            s_old = _dot_nt(q8.astype(BF16), kd)
            s_new = jnp.sum(q8 * _bf16_round(k_new), axis=-1, keepdims=True)
            m = jnp.maximum(jnp.max(s_old, axis=-1, keepdims=True), s_new)
            e_old = jnp.exp(s_old - m)
            e_new = jnp.exp(s_new - m)
            l = jnp.sum(e_old, axis=-1, keepdims=True) + e_new
            pv = _dot(e_old.astype(BF16), vd) + _bf16_round(e_new) * _bf16_round(v_new)
            outs.append(pv / l)
            lses.append(m + jnp.log(l))
            _shift_append(kc, ko_refs[gi], _half_cols(hf), k_new)
            _shift_append(vc, vo_refs[gi], _half_cols(hf), v_new)

        m = functools.reduce(jnp.maximum, lses)
        ws = [jnp.exp(l - m) for l in lses]
        den = functools.reduce(lambda x, y: x + y, ws)
        num = functools.reduce(lambda x, y: x + y, [w * o for w, o in zip(ws, outs)])
        merged = jnp.where(own_head, num / den, 0.0)
        ao_ref[:, _half_cols(hf)] = jnp.sum(merged, axis=0, keepdims=True)


def _attention_decode(z, tabs, k_caches, v_caches, layer):
    nb = z.shape[0]
    cspec = lambda w, hf: pl.BlockSpec((None, None, w, ATT_HALF), lambda i: (layer, i, 0, hf))
    ospec = lambda w: pl.BlockSpec((None, w, ATT_GROUP), lambda i: (i, 0, 0))
    tspec = pl.BlockSpec((1, ATT_HALF), lambda i: (0, 0))
    widths = [kc.shape[2] for kc in k_caches]
    cspecs = [cspec(w, hf) for w in widths for hf in range(N_HALF)]
    cache_shapes = [jax.ShapeDtypeStruct((nb, w, ATT_GROUP), F32) for w in widths]
    halves = lambda caches: [c for c in caches for _ in range(N_HALF)]
    return pl.pallas_call(
        _att_dec_kernel,
        out_shape=tuple([jax.ShapeDtypeStruct((nb, 1, ATT_GROUP), F32)] + cache_shapes + cache_shapes),
        grid=(nb,),
        in_specs=[pl.BlockSpec((None, 1, z.shape[1]), lambda i: (i, 0, 0)), tspec, tspec, tspec]
                 + cspecs + cspecs,
        out_specs=tuple([pl.BlockSpec((None, 1, ATT_GROUP), lambda i: (i, 0, 0))]
                        + [ospec(w) for w in widths] + [ospec(w) for w in widths]),
        compiler_params=_params("arbitrary"),
        name="dilated_attention_decode",
    )(z.reshape(nb, 1, z.shape[1]), *tabs, *halves(k_caches), *halves(v_caches))


def _pool_kernel(u_ref, w_ref, sc_ref, po_ref, a_ref, b_ref, *, seq):
    g = pl.program_id(1)
    body = pl.ds(POOL_PAD, seq)

    def window_mean_minus_token(win):
        x = u_ref[...]
        a_ref[0:POOL_PAD, :] = jnp.zeros((POOL_PAD, POOL_GROUP), F32)
        b_ref[0:POOL_PAD, :] = jnp.zeros((POOL_PAD, POOL_GROUP), F32)
        a_ref[body, :] = x
        src, dst = a_ref, b_ref
        k = 1
        while k < win:
            dst[body, :] = src[body, :] + src[pl.ds(POOL_PAD - k, seq), :]
            src, dst = dst, src
            k *= 2
        t = lax.broadcasted_iota(jnp.int32, (seq, POOL_GROUP), 0)
        cnt = jnp.minimum(t + 1, win).astype(F32)
        pooled = src[body, :] / cnt - x
        y = _dot(pooled.astype(BF16), w_ref[...].astype(BF16)) * sc_ref[...]
        po_ref[...] = y.astype(BF16)

    for gi, win in enumerate(POOL_WINDOWS):
        @pl.when(g == gi)
        def _(win=win):
            window_mean_minus_token(win)


def _pool_prompt(z3, w_pool, pool_scale, layer):
    b, s, _ = z3.shape
    ng = len(POOL_WINDOWS)
    return pl.pallas_call(
        functools.partial(_pool_kernel, seq=s),
        out_shape=jax.ShapeDtypeStruct((b, s, POOL_WIDTH), BF16),
        grid=(b, ng),
        in_specs=[
            pl.BlockSpec((None, s, POOL_GROUP), lambda i, g: (i, 0, COL_PU // POOL_GROUP + g)),
            pl.BlockSpec((None, None, POOL_GROUP, POOL_GROUP), lambda i, g: (layer, g, 0, 0)),
            pl.BlockSpec((None, 1, POOL_GROUP), lambda i, g: (layer, 0, g)),
        ],
        out_specs=pl.BlockSpec((None, s, POOL_GROUP), lambda i, g: (i, 0, g)),
        scratch_shapes=[pltpu.VMEM((POOL_PAD + s, POOL_GROUP), F32),
                        pltpu.VMEM((POOL_PAD + s, POOL_GROUP), F32)],
        compiler_params=_params("parallel", "arbitrary"),
        name="pool_mixer",
    )(z3, w_pool, pool_scale)


def _pool_dec_kernel(z_ref, buf_ref, w_ref, sc_ref, po_ref, bo_ref, pooled_ref, *, nb):
    row = lax.broadcasted_iota(jnp.int32, (POOL_BUF, POOL_GROUP), 0)
    for b in range(nb):
        u = z_ref[b:b + 1, COL_PU:COL_PU + POOL_WIDTH]
        old = buf_ref[b]
        for gi, win in enumerate(POOL_WINDOWS):
            cols = slice(gi * POOL_GROUP, (gi + 1) * POOL_GROUP)
            tail = jnp.where(row >= POOL_BUF - (win - 1), old[:, cols], 0.0)
            total = jnp.sum(tail, axis=0, keepdims=True) + u[:, cols]
            pooled_ref[b:b + 1, cols] = total / float(win) - u[:, cols]
        bo_ref[b, 0:POOL_BUF - 1, :] = old[1:POOL_BUF, :]
        bo_ref[b, POOL_BUF - 1:POOL_BUF, :] = u
    for gi in range(len(POOL_WINDOWS)):
        cols = slice(gi * POOL_GROUP, (gi + 1) * POOL_GROUP)
        y = _dot(pooled_ref[:, cols].astype(BF16), w_ref[gi].astype(BF16))
        po_ref[:, cols] = y * sc_ref[:, cols]


def _pool_decode(z, cache_pool, w_pool, pool_scale, layer):
    nb = z.shape[0]
    ng = len(POOL_WINDOWS)
    bshape = (nb, POOL_BUF, POOL_WIDTH)
    return pl.pallas_call(
        functools.partial(_pool_dec_kernel, nb=nb),
        out_shape=(jax.ShapeDtypeStruct((nb, POOL_WIDTH), F32),
                   jax.ShapeDtypeStruct(bshape, F32)),
        grid=(1,),
        in_specs=[
            pl.BlockSpec(z.shape, lambda i: (0, 0)),
            pl.BlockSpec((None,) + bshape, lambda i: (layer, 0, 0, 0)),
            pl.BlockSpec((None, ng, POOL_GROUP, POOL_GROUP), lambda i: (layer, 0, 0, 0)),
            pl.BlockSpec((None, 1, POOL_WIDTH), lambda i: (layer, 0, 0)),
        ],
        out_specs=(pl.BlockSpec((nb, POOL_WIDTH), lambda i: (0, 0)),
                   pl.BlockSpec(bshape, lambda i: (0, 0, 0))),
        scratch_shapes=[pltpu.VMEM((nb, POOL_WIDTH), F32)],
        compiler_params=_params("arbitrary"),
        name="pool_mixer_decode",
    )(z, cache_pool, w_pool, pool_scale)


def _trunk_prompt(x, mod, p, wb, final_g):
    b, s, d = x.shape
    depth = p["w_in"].shape[0]
    pos = jnp.arange(s, dtype=F32)
    ret_tabs = _ret_rope_tables(pos)
    att_tabs = _att_rope_tables(pos)
    keep = tuple(min(win, s) for win, _ in ATT_PATTERNS)
    tm = math.gcd(s, 1024)
    xf = x.reshape(b * s, d)
    rets, pools = [], []
    ks = [[] for _ in ATT_PATTERNS]
    vs = [[] for _ in ATT_PATTERNS]
    for l in range(depth):
        xf = _ffn(xf, mod, l, 0, s, p["norm_g"], *wb[l]["ffn1"], None, tm, 512)
        z = _inproj(xf, mod, l, s, p["norm_g"], wb[l]["w_in"], math.gcd(s, 512), IN_WIDTH // 2)
        z3 = z.reshape(b, s, IN_WIDTH)
        ro, ret_s = _retention_prompt(z3, *ret_tabs, p["ret_norm_g"], l)
        ao, k_rot = _attention_prompt(z3, att_tabs)
        po = _pool_prompt(z3, p["w_pool"], p["pool_scale"], l)
        xf = _outproj(xf, mod, l, s, ro.reshape(b * s, -1), ao.reshape(b * s, -1),
                      po.reshape(b * s, -1), p["w_out"], math.gcd(s, 512))
        xf = _ffn(xf, mod, l, 2, s, p["norm_g"], *wb[l]["ffn2"],
                  final_g if l == depth - 1 else None, tm, 512)
        rets.append(ret_s)
        pools.append(z3[:, s - POOL_BUF:, COL_PU:])
        for g in range(N_PAT):
            c0 = COL_AV + g * ATT_GROUP
            ks[g].append(k_rot[:, g, s - keep[g]:].reshape(b, keep[g], ATT_HEADS, ATT_DIM))
            vs[g].append(z3[:, s - keep[g]:, c0:c0 + ATT_GROUP].reshape(b, keep[g], ATT_HEADS, ATT_DIM))
    y = xf.reshape(b, s, d)
    return (y, jnp.stack(rets), [jnp.stack(k) for k in ks], [jnp.stack(v) for v in vs],
            jnp.stack(pools))


def _trunk_decode(x, mod, pos0, caches, p, final_g):
    nb, s, d = x.shape
    assert s == 1, "decode trunk handles one new token per batch row"
    depth = p["w_in"].shape[0]
    state_ret, cks, cvs, cpool = caches
    pos = pos0 + jnp.arange(s, dtype=F32)
    ret_tabs = _ret_rope_tables(pos)
    att_tabs = _att_rope_tables(pos)
    cks = [c.reshape(c.shape[0], nb, c.shape[2], ATT_GROUP) for c in cks]
    cvs = [c.reshape(c.shape[0], nb, c.shape[2], ATT_GROUP) for c in cvs]
    xf = x.reshape(nb, d)
    rets, pools, wb = [], [], []
    ks = [[] for _ in ATT_PATTERNS]
    vs = [[] for _ in ATT_PATTERNS]
    for l in range(depth):
        xf, *ffn1_bf16 = _ffn(xf, mod, l, 0, 1, p["norm_g"], p["w1_gate"], p["w1_up"], p["w1_down"],
                              None, nb, 512, emit_bf16=True)
        z, w_in_bf16 = _inproj(xf, mod, l, 1, p["norm_g"], p["w_in"], nb, 256, emit_bf16=True)
        ro, ret_s = _retention_decode(z, state_ret, *ret_tabs, p["ret_norm_g"], l)
        att = _attention_decode(z, att_tabs, cks, cvs, l)
        ao, k_new, v_new = att[0].reshape(nb, ATT_GROUP), att[1:1 + N_PAT], att[1 + N_PAT:]
        po, pool_new = _pool_decode(z, cpool, p["w_pool"], p["pool_scale"], l)
        xf = _outproj(xf, mod, l, 1, ro, ao, po, p["w_out"], nb)
        xf, *ffn2_bf16 = _ffn(xf, mod, l, 2, 1, p["norm_g"], p["w2_gate"], p["w2_up"], p["w2_down"],
                              final_g if l == depth - 1 else None, nb, 512, emit_bf16=True)
        wb.append({"ffn1": ffn1_bf16, "w_in": w_in_bf16, "ffn2": ffn2_bf16})
        rets.append(ret_s)
        pools.append(pool_new)
        for g in range(N_PAT):
            ks[g].append(k_new[g].reshape(nb, -1, ATT_HEADS, ATT_DIM))
            vs[g].append(v_new[g].reshape(nb, -1, ATT_HEADS, ATT_DIM))
    y = xf.reshape(nb, s, d)
    return (y, jnp.stack(rets), [jnp.stack(k) for k in ks], [jnp.stack(v) for v in vs],
            jnp.stack(pools), wb)


def kernel(x_prompt, x_sample, state_ret, cache_k_w128, cache_v_w128, cache_k_w512, cache_v_w512,
           cache_k_w2048, cache_v_w2048, cache_pool, c_prompt, c_sample, w_ada, b_ada, norm_g, w_in,
           ret_norm_g, w_pool, pool_scale, w_out, w1_gate, w1_up, w1_down, w2_gate, w2_up, w2_down,
           final_norm_g):
    depth, d = norm_g.shape[0], norm_g.shape[-1]
    n_pr, n_dec = c_prompt.shape[0], c_sample.shape[0]
    p = {
        "norm_g": norm_g.reshape(depth, N_SUB, 1, d),
        "w_in": w_in,
        "ret_norm_g": ret_norm_g.reshape(depth, 1, RET_WIDTH),
        "w_pool": w_pool,
        "pool_scale": pool_scale.reshape(depth, 1, POOL_WIDTH),
        "w_out": w_out,
        "w1_gate": w1_gate, "w1_up": w1_up, "w1_down": w1_down,
        "w2_gate": w2_gate, "w2_up": w2_up, "w2_down": w2_down,
    }
    final_g = final_norm_g.reshape(1, d)

    pad = (-(n_dec + n_pr)) % SUBLANES
    c_all = jnp.concatenate([c_sample, c_prompt, jnp.zeros((pad, d), F32)], axis=0)
    mod_dec, mod_pr = _ada(c_all, n_dec, n_pr, w_ada, b_ada)

    caches = (state_ret,
              (cache_k_w128, cache_k_w512, cache_k_w2048),
              (cache_v_w128, cache_v_w512, cache_v_w2048),
              cache_pool)
    y_s, ret_s, ks, vs, pool_s, wb = _trunk_decode(x_sample, mod_dec, float(PAST_LEN), caches, p, final_g)
    y_p, ret_p, kp, vp, pool_p = _trunk_prompt(x_prompt, mod_pr, p, wb, final_g)
    return (y_p, y_s, ret_p, ret_s,
            kp[0], ks[0], vp[0], vs[0],
            kp[1], ks[1], vp[1], vs[1],
            kp[2], ks[2], vp[2], vs[2],
            pool_p, pool_s)
```

```python
import functools
import math

import jax
import jax.numpy as jnp
from jax import lax
from jax.experimental import pallas as pl
from jax.experimental.pallas import tpu as pltpu

F32 = jnp.float32
BF16 = jnp.bfloat16

RET_HEADS = 6
RET_DIM = 128
RET_WIDTH = RET_HEADS * RET_DIM
RET_CHUNK = 128
RET_THETA = 10000.0
ATT_HEADS = 4
ATT_DIM = 64
ATT_GROUP = ATT_HEADS * ATT_DIM
ATT_PATTERNS = ((128, 1), (512, 4), (2048, 16))
N_PAT = len(ATT_PATTERNS)
ATT_WIDTH = N_PAT * ATT_GROUP
ROPE_THETA = 500000.0
ROPE_DIMS = ATT_DIM // 4
ROPE_HALF = ROPE_DIMS // 2
QUERY_BLOCK = 128
POOL_WINDOWS = (2, 4, 8, 16)
POOL_GROUP = 128
POOL_WIDTH = len(POOL_WINDOWS) * POOL_GROUP
POOL_BUF = max(POOL_WINDOWS) - 1
POOL_PAD = 16
N_SUB = 3
PAST_LEN = 16384
HALF_STEP = 0.5
EPS = 1e-6
MASK_VALUE = -1e30

COL_RQ, COL_RK, COL_RV, COL_RG = 0, RET_WIDTH, 2 * RET_WIDTH, 3 * RET_WIDTH
COL_AQ = 4 * RET_WIDTH
COL_AK = COL_AQ + ATT_WIDTH
COL_AV = COL_AK + ATT_WIDTH
COL_PU = COL_AV + ATT_WIDTH
IN_WIDTH = COL_PU + POOL_WIDTH

V7X_VMEM_BYTES = 64 * 1024 * 1024
VMEM_LIMIT = V7X_VMEM_BYTES - 8 * 1024 * 1024
SUBLANES = 8

LOG_GAMMA = tuple(math.log1p(-(2.0 ** (-5.0 - h))) for h in range(RET_HEADS))


def _params(*sem):
    return pltpu.CompilerParams(dimension_semantics=sem, vmem_limit_bytes=VMEM_LIMIT)


def _dot(a, b):
    return jnp.dot(a, b, preferred_element_type=F32)


def _dot_nt(a, b):
    return lax.dot_general(a, b, (((1,), (1,)), ((), ())), preferred_element_type=F32)


def _bf16_round(x):
    return x.astype(BF16).astype(F32)


def _ada_kernel(c_ref, w_ref, b_ref, od_ref, op_ref, *, n_dec, n_pr):
    c = c_ref[...]
    a = (c * jax.nn.sigmoid(c)).astype(BF16)
    res = _dot(a, w_ref[...].astype(BF16)) + b_ref[...]
    od_ref[...] = res[0:n_dec]
    for b in range(n_pr):
        op_ref[b] = res[n_dec + b:n_dec + b + 1]


def _ada(c_all, n_dec, n_pr, w_ada, b_ada, tn=1024):
    depth, d, n = w_ada.shape
    rows = c_all.shape[0]
    per = d // tn
    return pl.pallas_call(
        functools.partial(_ada_kernel, n_dec=n_dec, n_pr=n_pr),
        out_shape=(jax.ShapeDtypeStruct((depth, N_SUB * 3, n_dec, d), F32),
                   jax.ShapeDtypeStruct((depth, N_SUB * 3, n_pr, 1, d), F32)),
        grid=(depth, n // tn),
        in_specs=[
            pl.BlockSpec((rows, d), lambda l, j: (0, 0)),
            pl.BlockSpec((None, d, tn), lambda l, j: (l, 0, j)),
            pl.BlockSpec((None, 1, tn), lambda l, j: (l, 0, j)),
        ],
        out_specs=(
            pl.BlockSpec((None, None, n_dec, tn), lambda l, j: (l, j // per, 0, j % per)),
            pl.BlockSpec((None, None, n_pr, 1, tn), lambda l, j: (l, j // per, 0, 0, j % per)),
        ),
        compiler_params=_params("arbitrary", "arbitrary"),
        name="ada_mod",
    )(c_all, w_ada, b_ada.reshape(depth, 1, n))


def _rmsnorm(x, g):
    ms = jnp.mean(x * x, axis=-1, keepdims=True)
    return x * lax.rsqrt(ms + EPS) * g


def _row_chunks(tm):
    rc = min(tm, 128)
    return rc, tm // rc


def _mod_rows(ref, r, rc):
    return ref[...] if ref.shape[0] == 1 else ref[pl.ds(r, rc), :]


def _prenorm_to(h_ref, x_ref, g_ref, sh_ref, sc_ref):
    rc, n = _row_chunks(x_ref.shape[0])

    def body(i, carry):
        r = pl.multiple_of(i * rc, rc)
        y = _rmsnorm(x_ref[pl.ds(r, rc), :], g_ref[...])
        h = y * (1.0 + _mod_rows(sc_ref, r, rc)) + _mod_rows(sh_ref, r, rc)
        h_ref[pl.ds(r, rc), :] = h.astype(BF16)
        return carry

    lax.fori_loop(0, n, body, 0)


def _mod_specs(mod, layer, sub, rows_per_batch, tm, grid_rank):
    d = mod.shape[-1]
    specs = []
    for k in range(3):
        j = sub * 3 + k
        if mod.ndim == 5:
            if grid_rank == 2:
                idx = (lambda j: lambda i, f: (layer, j, (i * tm) // rows_per_batch, 0, 0))(j)
            else:
                idx = (lambda j: lambda i: (layer, j, (i * tm) // rows_per_batch, 0, 0))(j)
            specs.append(pl.BlockSpec((None, None, None, 1, d), idx))
        else:
            if grid_rank == 2:
                idx = (lambda j: lambda i, f: (layer, j, 0, 0))(j)
            else:
                idx = (lambda j: lambda i: (layer, j, 0, 0))(j)
            specs.append(pl.BlockSpec((None, None, tm, d), idx))
    return specs


def _ffn_kernel(x_ref, sh_ref, sc_ref, gt_ref, g_ref, wg_ref, wu_ref, wd_ref, *rest,
                n_f, final_norm, emit_bf16):
    rest = list(rest)
    fg_ref = rest.pop(0) if final_norm else None
    o_ref = rest.pop(0)
    wb_refs = [rest.pop(0) for _ in range(3)] if emit_bf16 else None
    h_ref, = rest
    f = pl.program_id(1)

    @pl.when(f == 0)
    def _():
        _prenorm_to(h_ref, x_ref, g_ref, sh_ref, sc_ref)
        o_ref[...] = jnp.zeros_like(o_ref)

    wg = wg_ref[...].astype(BF16)
    wu = wu_ref[...].astype(BF16)
    wd = wd_ref[...].astype(BF16)
    if emit_bf16:
        wb_refs[0][...] = wg
        wb_refs[1][...] = wu
        wb_refs[2][...] = wd
    h = h_ref[...]
    gate = _dot(h, wg)
    up = _dot(h, wu)
    act = (gate * jax.nn.sigmoid(gate) * up).astype(BF16)
    d = o_ref.shape[1]
    dc = min(d, 512)
    for c0 in range(0, d, dc):
        o_ref[:, c0:c0 + dc] += _dot(act, wd[:, c0:c0 + dc])

    @pl.when(f == n_f - 1)
    def _():
        rc, n = _row_chunks(x_ref.shape[0])

        def body(i, carry):
            r = pl.multiple_of(i * rc, rc)
            rows = pl.ds(r, rc)
            out = x_ref[rows, :] + HALF_STEP * _mod_rows(gt_ref, r, rc) * o_ref[rows, :]
            if final_norm:
                out = _rmsnorm(out, fg_ref[...])
            o_ref[rows, :] = out
            return carry

        lax.fori_loop(0, n, body, 0)


def _weight_spec(w, layer, block, index, resident=False):
    mode = {"pipeline_mode": pl.Buffered(1)} if resident else {}
    if w.ndim == 3:
        return pl.BlockSpec((None,) + block, lambda i, j: (layer,) + index(i, j), **mode)
    return pl.BlockSpec(block, index, **mode)


def _ffn(x, mod, layer, sub, rows_per_batch, norm_g, wg, wu, wd, final_g, tm, tf, emit_bf16=False):
    m, d = x.shape
    d_ff = wg.shape[-1]
    assert m % tm == 0 and d_ff % tf == 0
    assert not emit_bf16 or m == tm, "each weight block must be visited exactly once"
    n_f = d_ff // tf
    final_norm = final_g is not None
    col_block = lambda i, j: (0, j)
    row_block = lambda i, j: (j, 0)
    in_specs = [pl.BlockSpec((tm, d), lambda i, j: (i, 0))]
    in_specs += _mod_specs(mod, layer, sub, rows_per_batch, tm, 2)
    in_specs += [
        pl.BlockSpec((None, None, 1, d), lambda i, j: (layer, sub, 0, 0)),
        _weight_spec(wg, layer, (d, tf), col_block),
        _weight_spec(wu, layer, (d, tf), col_block),
        _weight_spec(wd, layer, (tf, d), row_block),
    ]
    args = [x, mod, mod, mod, norm_g, wg, wu, wd]
    if final_norm:
        in_specs.append(pl.BlockSpec((1, d), lambda i, j: (0, 0)))
        args.append(final_g)
    out_shape = [jax.ShapeDtypeStruct((m, d), F32)]
    out_specs = [pl.BlockSpec((tm, d), lambda i, j: (i, 0))]
    if emit_bf16:
        out_shape += [jax.ShapeDtypeStruct((d, d_ff), BF16), jax.ShapeDtypeStruct((d, d_ff), BF16),
                      jax.ShapeDtypeStruct((d_ff, d), BF16)]
        out_specs += [pl.BlockSpec((d, tf), col_block), pl.BlockSpec((d, tf), col_block),
                      pl.BlockSpec((tf, d), row_block)]
    out = pl.pallas_call(
        functools.partial(_ffn_kernel, n_f=n_f, final_norm=final_norm, emit_bf16=emit_bf16),
        out_shape=tuple(out_shape),
        grid=(m // tm, n_f),
        in_specs=in_specs,
        out_specs=tuple(out_specs),
        scratch_shapes=[pltpu.VMEM((tm, d), BF16)],
        compiler_params=_params("parallel", "arbitrary"),
        name="ffn",
    )(*args)
    return out if emit_bf16 else out[0]


def _inproj_kernel(x_ref, sh_ref, sc_ref, g_ref, w_ref, o_ref, *rest, emit_bf16):
    wb_ref, h_ref = rest if emit_bf16 else (None,) + rest

    @pl.when(pl.program_id(1) == 0)
    def _():
        _prenorm_to(h_ref, x_ref, g_ref, sh_ref, sc_ref)

    if emit_bf16:
        wb_ref[...] = w_ref[...].astype(BF16)
    h = h_ref[...]
    tn = o_ref.shape[1]
    nc = min(tn, 512)
    for c0 in range(0, tn, nc):
        c1 = min(c0 + nc, tn)
        o_ref[:, c0:c1] = _dot(h, w_ref[:, c0:c1].astype(BF16))


def _inproj(x, mod, layer, rows_per_batch, norm_g, w_in, tm, tn, emit_bf16=False):
    m, d = x.shape
    n = w_in.shape[-1]
    assert m % tm == 0 and n % tn == 0
    assert not emit_bf16 or m == tm, "each weight block must be visited exactly once"
    sh, sc, _ = _mod_specs(mod, layer, 1, rows_per_batch, tm, 2)
    col_block = lambda i, j: (0, j)
    out_shape = [jax.ShapeDtypeStruct((m, n), F32)]
    out_specs = [pl.BlockSpec((tm, tn), lambda i, j: (i, j))]
    if emit_bf16:
        out_shape.append(jax.ShapeDtypeStruct((d, n), BF16))
        out_specs.append(pl.BlockSpec((d, tn), col_block))
    out = pl.pallas_call(
        functools.partial(_inproj_kernel, emit_bf16=emit_bf16),
        out_shape=tuple(out_shape),
        grid=(m // tm, n // tn),
        in_specs=[
            pl.BlockSpec((tm, d), lambda i, j: (i, 0)),
            sh, sc,
            pl.BlockSpec((None, None, 1, d), lambda i, j: (layer, 1, 0, 0)),
            _weight_spec(w_in, layer, (d, tn), col_block, resident=(n == tn)),
        ],
        out_specs=tuple(out_specs),
        scratch_shapes=[pltpu.VMEM((tm, d), BF16)],
        compiler_params=_params("parallel", "arbitrary"),
        name="in_proj",
    )(x, mod, mod, norm_g, w_in)
    return out if emit_bf16 else out[0]


def _outproj_kernel(x_ref, gt_ref, ro_ref, ao_ref, po_ref, w_ref, o_ref, wb_ref):
    @pl.when(pl.program_id(0) == 0)
    def _():
        wb_ref[...] = w_ref[...].astype(BF16)

    r0, r1 = RET_WIDTH, RET_WIDTH + ATT_GROUP
    y = _dot(ro_ref[...].astype(BF16), wb_ref[0:r0, :])
    y += _dot(ao_ref[...].astype(BF16), wb_ref[r0:r1, :])
    y += _dot(po_ref[...].astype(BF16), wb_ref[r1:, :])
    o_ref[...] = x_ref[...] + gt_ref[...] * y


def _outproj(x, mod, layer, rows_per_batch, ro, ao, po, w_out, tm):
    m, d = x.shape
    k = w_out.shape[1]
    _, _, gt = _mod_specs(mod, layer, 1, rows_per_batch, tm, 1)
    return pl.pallas_call(
        _outproj_kernel,
        out_shape=jax.ShapeDtypeStruct((m, d), F32),
        grid=(m // tm,),
        in_specs=[
            pl.BlockSpec((tm, d), lambda i: (i, 0)),
            gt,
            pl.BlockSpec((tm, RET_WIDTH), lambda i: (i, 0)),
            pl.BlockSpec((tm, ATT_GROUP), lambda i: (i, 0)),
            pl.BlockSpec((tm, POOL_WIDTH), lambda i: (i, 0)),
            pl.BlockSpec((None, k, d), lambda i: (layer, 0, 0), pipeline_mode=pl.Buffered(1)),
        ],
        out_specs=pl.BlockSpec((tm, d), lambda i: (i, 0)),
        scratch_shapes=[pltpu.VMEM((k, d), BF16)],
        compiler_params=_params("arbitrary"),
        name="out_proj",
    )(x, mod, ro, ao, po, w_out)


def _ret_rope_tables(pos):
    half = RET_DIM // 2
    freq = jnp.power(jnp.float32(RET_THETA), -jnp.arange(half, dtype=F32) / half)
    ang = pos[:, None] * freq[None, :]
    cos, sin = jnp.cos(ang), jnp.sin(ang)
    return jnp.concatenate([cos, cos], axis=-1), jnp.concatenate([-sin, sin], axis=-1)


def _att_rope_tables(pos):
    freq = jnp.power(jnp.float32(ROPE_THETA), -jnp.arange(ROPE_HALF, dtype=F32) / ROPE_HALF)
    ang = pos[:, None] * freq[None, :]
    cos, sin = jnp.cos(ang), jnp.sin(ang)
    s = pos.shape[0]
    rest = ATT_DIM - ROPE_DIMS
    c = jnp.concatenate([cos, cos, jnp.ones((s, rest), F32)], axis=-1)
    a = jnp.concatenate([-sin, jnp.zeros((s, ATT_DIM - ROPE_HALF), F32)], axis=-1)
    b = jnp.concatenate([jnp.zeros((s, ROPE_HALF), F32), sin, jnp.zeros((s, rest), F32)], axis=-1)
    tile = lambda t: jnp.tile(t, (1, 128 // ATT_DIM))
    return tile(c), tile(a), tile(b)


def _ret_rope(x, cos_t, sin_t):
    return x * cos_t + pltpu.roll(x, RET_DIM // 2, axis=1) * sin_t


def _att_rope(x, c, a, b):
    n = x.shape[-1]
    return x * c + pltpu.roll(x, n - ROPE_HALF, axis=1) * a + pltpu.roll(x, ROPE_HALF, axis=1) * b


def _head_norm(o, g):
    mu = jnp.mean(o, axis=-1, keepdims=True)
    oc = o - mu
    var = jnp.mean(oc * oc, axis=-1, keepdims=True)
    return oc * lax.rsqrt(var + EPS) * g


def _ret_kernel(zq_ref, zk_ref, zv_ref, zg_ref, cos_ref, sin_ref, gn_ref,
                ro_ref, so_ref, s_ref, din_ref, dq_ref, dk_ref, *, n_chunks, chunk):
    c = pl.program_id(1)

    @pl.when(c == 0)
    def _():
        s_ref[...] = jnp.zeros_like(s_ref)
        row = lax.broadcasted_iota(jnp.int32, (chunk, chunk), 0).astype(F32)
        col = lax.broadcasted_iota(jnp.int32, (chunk, chunk), 1).astype(F32)
        diff = row - col
        rowd = lax.broadcasted_iota(jnp.int32, (chunk, RET_DIM), 0).astype(F32)
        for h in range(RET_HEADS):
            lg = LOG_GAMMA[h]
            din_ref[h] = jnp.where(diff >= 0, jnp.exp(jnp.maximum(diff, 0.0) * lg), 0.0)
            dq_ref[h] = jnp.exp((rowd + 1.0) * lg)
            dk_ref[h] = jnp.exp((chunk - 1.0 - rowd) * lg)

    cos_t = cos_ref[...]
    sin_t = sin_ref[...]
    for h in range(RET_HEADS):
        cols = slice(h * RET_DIM, (h + 1) * RET_DIM)
        q = _ret_rope(zq_ref[:, cols], cos_t, sin_t)
        k = _ret_rope(zk_ref[:, cols], cos_t, sin_t) * (RET_DIM ** -0.5)
        qb = q.astype(BF16)
        kb = k.astype(BF16)
        vb = zv_ref[:, cols].astype(BF16)
        s_old = s_ref[h]
        a = _dot_nt(qb, kb) * din_ref[h]
        o = _dot(a.astype(BF16), vb) + _dot(qb, s_old.astype(BF16)) * dq_ref[h]
        kd_t = (k * dk_ref[h]).T.astype(BF16)
        s_new = math.exp(chunk * LOG_GAMMA[h]) * s_old + _dot(kd_t, vb)
        s_ref[h] = s_new

        on = _head_norm(o, gn_ref[:, cols])
        g = zg_ref[:, cols]
        ro_ref[:, cols] = (g * jax.nn.sigmoid(g) * on).astype(BF16)
        so_ref[h] = s_new


def _retention_prompt(z3, cos_t, sin_t, ret_norm_g, layer):
    b, s, _ = z3.shape
    chunk = math.gcd(s, RET_CHUNK)
    n_chunks = s // chunk
    zspec = lambda cb: pl.BlockSpec((None, chunk, RET_WIDTH), lambda i, c: (i, c, cb))
    return pl.pallas_call(
        functools.partial(_ret_kernel, n_chunks=n_chunks, chunk=chunk),
        out_shape=(jax.ShapeDtypeStruct((b, s, RET_WIDTH), BF16),
                   jax.ShapeDtypeStruct((b, RET_HEADS, RET_DIM, RET_DIM), F32)),
        grid=(b, n_chunks),
        in_specs=[
            zspec(COL_RQ // RET_WIDTH), zspec(COL_RK // RET_WIDTH),
            zspec(COL_RV // RET_WIDTH), zspec(COL_RG // RET_WIDTH),
            pl.BlockSpec((chunk, RET_DIM), lambda i, c: (c, 0)),
            pl.BlockSpec((chunk, RET_DIM), lambda i, c: (c, 0)),
            pl.BlockSpec((None, 1, RET_WIDTH), lambda i, c: (layer, 0, 0)),
        ],
        out_specs=(
            pl.BlockSpec((None, chunk, RET_WIDTH), lambda i, c: (i, c, 0)),
            pl.BlockSpec((None, RET_HEADS, RET_DIM, RET_DIM), lambda i, c: (i, 0, 0, 0)),
        ),
        scratch_shapes=[
            pltpu.VMEM((RET_HEADS, RET_DIM, RET_DIM), F32),
            pltpu.VMEM((RET_HEADS, chunk, chunk), F32),
            pltpu.VMEM((RET_HEADS, chunk, RET_DIM), F32),
            pltpu.VMEM((RET_HEADS, chunk, RET_DIM), F32),
        ],
        compiler_params=_params("parallel", "arbitrary"),
        name="retention",
    )(z3, z3, z3, z3, cos_t, sin_t, ret_norm_g)


def _ret_dec_kernel(z_ref, s0_ref, cos_ref, sin_ref, gn_ref, ro_ref, so_ref, *, nb):
    cos_t = cos_ref[...]
    sin_t = sin_ref[...]
    row = lax.broadcasted_iota(jnp.int32, (nb, RET_DIM), 0)
    for h in range(RET_HEADS):
        gamma = math.exp(LOG_GAMMA[h])
        q = _ret_rope(z_ref[:, COL_RQ + h * RET_DIM:COL_RQ + (h + 1) * RET_DIM], cos_t, sin_t)
        k = _ret_rope(z_ref[:, COL_RK + h * RET_DIM:COL_RK + (h + 1) * RET_DIM], cos_t, sin_t)
        k = k * (RET_DIM ** -0.5)
        v = z_ref[:, COL_RV + h * RET_DIM:COL_RV + (h + 1) * RET_DIM]
        qr, kr, vr = _bf16_round(q), _bf16_round(k), _bf16_round(v)
        qk = jnp.sum(qr * kr, axis=-1, keepdims=True)
        o = _bf16_round(qk) * vr
        cross = jnp.zeros((nb, RET_DIM), F32)
        for b in range(nb):
            s_old = s0_ref[b, h]
            res = _dot(qr.astype(BF16), s_old.astype(BF16))
            cross = cross + jnp.where(row == b, res, 0.0)
            k_col = jnp.broadcast_to(kr[b:b + 1, :], (RET_DIM, RET_DIM)).T
            so_ref[b, h] = gamma * s_old + k_col * vr[b:b + 1, :]
        o = o + cross * gamma
        cols = slice(h * RET_DIM, (h + 1) * RET_DIM)
        on = _head_norm(o, gn_ref[:, cols])
        g = z_ref[:, COL_RG + h * RET_DIM:COL_RG + (h + 1) * RET_DIM]
        ro_ref[:, cols] = g * jax.nn.sigmoid(g) * on


def _retention_decode(z, state, cos_t, sin_t, ret_norm_g, layer):
    nb = z.shape[0]
    sshape = (nb, RET_HEADS, RET_DIM, RET_DIM)
    return pl.pallas_call(
        functools.partial(_ret_dec_kernel, nb=nb),
        out_shape=(jax.ShapeDtypeStruct((nb, RET_WIDTH), F32),
                   jax.ShapeDtypeStruct(sshape, F32)),
        grid=(1,),
        in_specs=[
            pl.BlockSpec(z.shape, lambda i: (0, 0)),
            pl.BlockSpec((None,) + sshape, lambda i: (layer, 0, 0, 0, 0)),
            pl.BlockSpec((1, RET_DIM), lambda i: (0, 0)),
            pl.BlockSpec((1, RET_DIM), lambda i: (0, 0)),
            pl.BlockSpec((None, 1, RET_WIDTH), lambda i: (layer, 0, 0)),
        ],
        out_specs=(
            pl.BlockSpec((nb, RET_WIDTH), lambda i: (0, 0)),
            pl.BlockSpec(sshape, lambda i: (0, 0, 0, 0)),
        ),
        compiler_params=_params("arbitrary"),
        name="retention_decode",
    )(z, state, cos_t, sin_t, ret_norm_g)


ATT_HALF = 128
N_HALF = ATT_GROUP // ATT_HALF
HEADS_PER_HALF = ATT_HALF // ATT_DIM


def _half_cols(hf):
    return slice(hf * ATT_HALF, (hf + 1) * ATT_HALF)


def _att_kernel(*refs, seq):
    zq = refs[0:N_HALF]
    zk = refs[N_HALF:2 * N_HALF]
    zv = refs[2 * N_HALF:3 * N_HALF]
    c_ref, a_ref, b_ref, ao_ref, ko_ref, q_ref, k_ref, o_ref, lse_ref = refs[3 * N_HALF:]
    g = pl.program_id(1)
    qb = QUERY_BLOCK
    n_blocks = seq // qb

    def rope_body(i, carry):
        r = pl.multiple_of(i * qb, qb)
        rows = pl.ds(r, qb)
        c, a, b = c_ref[rows, :], a_ref[rows, :], b_ref[rows, :]
        for hf in range(N_HALF):
            q_ref[hf, rows, :] = _att_rope(zq[hf][rows, :], c, a, b) * (ATT_DIM ** -0.5)
            k_rot = _att_rope(zk[hf][rows, :], c, a, b)
            k_ref[hf, rows, :] = k_rot
            ko_ref[rows, _half_cols(hf)] = k_rot
        return carry

    lax.fori_loop(0, n_blocks, rope_body, 0)

    tq = lax.broadcasted_iota(jnp.int32, (qb, qb), 0)
    tk = lax.broadcasted_iota(jnp.int32, (qb, qb), 1)
    cur_valid = tk <= tq
    prev_valid = tk >= tq
    lane = lax.broadcasted_iota(jnp.int32, (qb, ATT_HALF), 1)
    head_masks = [(lane // ATT_DIM) == hh for hh in range(HEADS_PER_HALF)]

    def group_body(gi, dil):
        blocks_per_class = seq // (dil * qb)
        has_prev = blocks_per_class > 1

        def rows_of(start):
            return pl.ds(start, qb, stride=dil) if dil > 1 else pl.ds(start, qb)

        def block_body(t, carry):
            cls = t % dil
            blk = t // dil
            rows = rows_of(cls + dil * qb * blk)
            valid = cur_valid
            if has_prev:
                prows = rows_of(cls + dil * qb * jnp.maximum(blk - 1, 0))
                valid = jnp.concatenate([prev_valid & (blk > 0), cur_valid], axis=1)
            for hf in range(N_HALF):
                q = q_ref[hf, rows, :].astype(BF16)
                keys = k_ref[hf, rows, :].astype(BF16)
                vals = zv[hf][rows, :].astype(BF16)
                if has_prev:
                    keys = jnp.concatenate([k_ref[hf, prows, :].astype(BF16), keys], axis=0)
                    vals = jnp.concatenate([zv[hf][prows, :].astype(BF16), vals], axis=0)
                vals_ones = jnp.concatenate([vals, jnp.ones_like(vals)], axis=1)
                num = jnp.zeros((qb, ATT_HALF), F32)
                den = jnp.ones((qb, ATT_HALF), F32)
                lse_acc = jnp.zeros((qb, ATT_HALF), F32)
                for hm in head_masks:
                    qh = jnp.where(hm, q, jnp.zeros_like(q))
                    s = jnp.where(valid, _dot_nt(qh, keys), MASK_VALUE)
                    m = jnp.max(s, axis=-1, keepdims=True)
                    e = jnp.exp(s - m).astype(BF16)
                    r = _dot(e, vals_ones)
                    l = r[:, ATT_HALF:]
                    num = jnp.where(hm, r[:, :ATT_HALF], num)
                    den = jnp.where(hm, l, den)
                    lse_acc = jnp.where(hm, m + jnp.log(l), lse_acc)
                o_ref[gi, hf, rows, :] = num / den
                lse_ref[gi, hf, rows, :] = lse_acc
            return carry

        lax.fori_loop(0, n_blocks, block_body, 0, unroll=2)

    for gi, (_, dil) in enumerate(ATT_PATTERNS):
        @pl.when(g == gi)
        def _(gi=gi, dil=dil):
            group_body(gi, dil)

    @pl.when(g == N_PAT - 1)
    def _():
        def merge_body(i, carry):
            r = pl.multiple_of(i * qb, qb)
            rows = pl.ds(r, qb)
            for hf in range(N_HALF):
                lses = [lse_ref[gi, hf, rows, :] for gi in range(N_PAT)]
                m = functools.reduce(jnp.maximum, lses)
                ws = [jnp.exp(l - m) for l in lses]
                den = functools.reduce(lambda x, y: x + y, ws)
                num = functools.reduce(lambda x, y: x + y,
                                       [w * o_ref[gi, hf, rows, :] for gi, w in enumerate(ws)])
                ao_ref[rows, _half_cols(hf)] = (num / den).astype(BF16)
            return carry

        lax.fori_loop(0, n_blocks, merge_body, 0)


def _attention_prompt(z3, tabs):
    b, s, _ = z3.shape
    assert all(s % (dil * QUERY_BLOCK) == 0 for _, dil in ATT_PATTERNS)
    zspec = lambda col, hf: pl.BlockSpec(
        (None, s, ATT_HALF), lambda i, g: (i, 0, col // ATT_HALF + N_HALF * g + hf))
    zspecs = [zspec(col, hf) for col in (COL_AQ, COL_AK, COL_AV) for hf in range(N_HALF)]
    tspec = pl.BlockSpec((s, ATT_HALF), lambda i, g: (0, 0), pipeline_mode=pl.Buffered(1))
    return pl.pallas_call(
        functools.partial(_att_kernel, seq=s),
        out_shape=(jax.ShapeDtypeStruct((b, s, ATT_GROUP), BF16),
                   jax.ShapeDtypeStruct((b, N_PAT, s, ATT_GROUP), F32)),
        grid=(b, N_PAT),
        in_specs=zspecs + [tspec, tspec, tspec],
        out_specs=(
            pl.BlockSpec((None, s, ATT_GROUP), lambda i, g: (i, 0, 0)),
            pl.BlockSpec((None, None, s, ATT_GROUP), lambda i, g: (i, g, 0, 0)),
        ),
        scratch_shapes=[
            pltpu.VMEM((N_HALF, s, ATT_HALF), F32),
            pltpu.VMEM((N_HALF, s, ATT_HALF), F32),
            pltpu.VMEM((N_PAT, N_HALF, s, ATT_HALF), F32),
            pltpu.VMEM((N_PAT, N_HALF, s, ATT_HALF), F32),
        ],
        compiler_params=_params("parallel", "arbitrary"),
        name="dilated_attention",
    )(*([z3] * (3 * N_HALF)), *tabs)


def _shift_append(src_ref, dst_ref, cols, new_row):
    w = src_ref.shape[0]
    step = 512
    body = w - SUBLANES
    for a in range(0, body, step):
        n = min(step, body - a)
        dst_ref[a:a + n, cols] = src_ref[a + 1:a + 1 + n, :]
    tail = pltpu.roll(src_ref[body:w, :], SUBLANES - 1, axis=0)
    row = lax.broadcasted_iota(jnp.int32, tail.shape, 0)
    dst_ref[body:w, cols] = jnp.where(row == SUBLANES - 1, new_row, tail)


def _att_dec_kernel(z_ref, c_ref, a_ref, b_ref, *refs):
    n = N_PAT * N_HALF
    kc_refs = refs[0:n]
    vc_refs = refs[n:2 * n]
    ao_ref = refs[2 * n]
    ko_refs = refs[2 * n + 1:2 * n + 1 + N_PAT]
    vo_refs = refs[2 * n + 1 + N_PAT:2 * n + 1 + 2 * N_PAT]
    bi = pl.program_id(0)
    rope = (c_ref[...], a_ref[...], b_ref[...])
    shape8 = (SUBLANES, ATT_HALF)
    row8 = lax.broadcasted_iota(jnp.int32, shape8, 0)
    lane8 = lax.broadcasted_iota(jnp.int32, shape8, 1)
    own_head = (lane8 // ATT_DIM) == row8

    for hf in range(N_HALF):
        outs, lses = [], []
        for gi, (win, dil) in enumerate(ATT_PATTERNS):
            off = gi * ATT_GROUP + hf * ATT_HALF
            zrow = lambda col: z_ref[:, col + off:col + off + ATT_HALF]
            q = _att_rope(zrow(COL_AQ), *rope) * (ATT_DIM ** -0.5)
            k_new = _att_rope(zrow(COL_AK), *rope)
            v_new = zrow(COL_AV)
            n_keys = win // dil
            kc, vc = kc_refs[gi * N_HALF + hf], vc_refs[gi * N_HALF + hf]
            rows = pl.ds(0, n_keys, stride=dil) if dil > 1 else pl.ds(0, n_keys)
            kd = kc[rows, :].astype(BF16)
            vd = vc[rows, :].astype(BF16)
            q8 = _bf16_round(jnp.where(own_head, jnp.broadcast_to(q, shape8), 0.0))
            s_old = _dot_nt(q8.astype(BF16), kd)
            s_new = jnp.sum(q8 * _bf16_round(k_new), axis=-1, keepdims=True)
            m = jnp.maximum(jnp.max(s_old, axis=-1, keepdims=True), s_new)
            e_old = jnp.exp(s_old - m)
            e_new = jnp.exp(s_new - m)
            l = jnp.sum(e_old, axis=-1, keepdims=True) + e_new
            pv = _dot(e_old.astype(BF16), vd) + _bf16_round(e_new) * _bf16_round(v_new)
            outs.append(pv / l)
            lses.append(m + jnp.log(l))
            _shift_append(kc, ko_refs[gi], _half_cols(hf), k_new)
            _shift_append(vc, vo_refs[gi], _half_cols(hf), v_new)

        m = functools.reduce(jnp.maximum, lses)
        ws = [jnp.exp(l - m) for l in lses]
        den = functools.reduce(lambda x, y: x + y, ws)
        num = functools.reduce(lambda x, y: x + y, [w * o for w, o in zip(ws, outs)])
        merged = jnp.where(own_head, num / den, 0.0)
        ao_ref[:, _half_cols(hf)] = jnp.sum(merged, axis=0, keepdims=True)


def _attention_decode(z, tabs, k_caches, v_caches, layer):
    nb = z.shape[0]
    cspec = lambda w, hf: pl.BlockSpec((None, None, w, ATT_HALF), lambda i: (layer, i, 0, hf))
    ospec = lambda w: pl.BlockSpec((None, w, ATT_GROUP), lambda i: (i, 0, 0))
    tspec = pl.BlockSpec((1, ATT_HALF), lambda i: (0, 0))
    widths = [kc.shape[2] for kc in k_caches]
    cspecs = [cspec(w, hf) for w in widths for hf in range(N_HALF)]
    cache_shapes = [jax.ShapeDtypeStruct((nb, w, ATT_GROUP), F32) for w in widths]
    halves = lambda caches: [c for c in caches for _ in range(N_HALF)]
    return pl.pallas_call(
        _att_dec_kernel,
        out_shape=tuple([jax.ShapeDtypeStruct((nb, 1, ATT_GROUP), F32)] + cache_shapes + cache_shapes),
        grid=(nb,),
        in_specs=[pl.BlockSpec((None, 1, z.shape[1]), lambda i: (i, 0, 0)), tspec, tspec, tspec]
                 + cspecs + cspecs,
        out_specs=tuple([pl.BlockSpec((None, 1, ATT_GROUP), lambda i: (i, 0, 0))]
                        + [ospec(w) for w in widths] + [ospec(w) for w in widths]),
        compiler_params=_params("arbitrary"),
        name="dilated_attention_decode",
    )(z.reshape(nb, 1, z.shape[1]), *tabs, *halves(k_caches), *halves(v_caches))


def _pool_kernel(u_ref, w_ref, sc_ref, po_ref, a_ref, b_ref, *, seq):
    g = pl.program_id(1)
    body = pl.ds(POOL_PAD, seq)

    def window_mean_minus_token(win):
        x = u_ref[...]
        a_ref[0:POOL_PAD, :] = jnp.zeros((POOL_PAD, POOL_GROUP), F32)
        b_ref[0:POOL_PAD, :] = jnp.zeros((POOL_PAD, POOL_GROUP), F32)
        a_ref[body, :] = x
        src, dst = a_ref, b_ref
        k = 1
        while k < win:
            dst[body, :] = src[body, :] + src[pl.ds(POOL_PAD - k, seq), :]
            src, dst = dst, src
            k *= 2
        t = lax.broadcasted_iota(jnp.int32, (seq, POOL_GROUP), 0)
        cnt = jnp.minimum(t + 1, win).astype(F32)
        pooled = src[body, :] / cnt - x
        y = _dot(pooled.astype(BF16), w_ref[...].astype(BF16)) * sc_ref[...]
        po_ref[...] = y.astype(BF16)

    for gi, win in enumerate(POOL_WINDOWS):
        @pl.when(g == gi)
        def _(win=win):
            window_mean_minus_token(win)


def _pool_prompt(z3, w_pool, pool_scale, layer):
    b, s, _ = z3.shape
    ng = len(POOL_WINDOWS)
    return pl.pallas_call(
        functools.partial(_pool_kernel, seq=s),
        out_shape=jax.ShapeDtypeStruct((b, s, POOL_WIDTH), BF16),
        grid=(b, ng),
        in_specs=[
            pl.BlockSpec((None, s, POOL_GROUP), lambda i, g: (i, 0, COL_PU // POOL_GROUP + g)),
            pl.BlockSpec((None, None, POOL_GROUP, POOL_GROUP), lambda i, g: (layer, g, 0, 0)),
            pl.BlockSpec((None, 1, POOL_GROUP), lambda i, g: (layer, 0, g)),
        ],
        out_specs=pl.BlockSpec((None, s, POOL_GROUP), lambda i, g: (i, 0, g)),
        scratch_shapes=[pltpu.VMEM((POOL_PAD + s, POOL_GROUP), F32),
                        pltpu.VMEM((POOL_PAD + s, POOL_GROUP), F32)],
        compiler_params=_params("parallel", "arbitrary"),
        name="pool_mixer",
    )(z3, w_pool, pool_scale)


def _pool_dec_kernel(z_ref, buf_ref, w_ref, sc_ref, po_ref, bo_ref, pooled_ref, *, nb):
    row = lax.broadcasted_iota(jnp.int32, (POOL_BUF, POOL_GROUP), 0)
    for b in range(nb):
        u = z_ref[b:b + 1, COL_PU:COL_PU + POOL_WIDTH]
        old = buf_ref[b]
        for gi, win in enumerate(POOL_WINDOWS):
            cols = slice(gi * POOL_GROUP, (gi + 1) * POOL_GROUP)
            tail = jnp.where(row >= POOL_BUF - (win - 1), old[:, cols], 0.0)
            total = jnp.sum(tail, axis=0, keepdims=True) + u[:, cols]
            pooled_ref[b:b + 1, cols] = total / float(win) - u[:, cols]
        bo_ref[b, 0:POOL_BUF - 1, :] = old[1:POOL_BUF, :]
        bo_ref[b, POOL_BUF - 1:POOL_BUF, :] = u
    for gi in range(len(POOL_WINDOWS)):
        cols = slice(gi * POOL_GROUP, (gi + 1) * POOL_GROUP)
        y = _dot(pooled_ref[:, cols].astype(BF16), w_ref[gi].astype(BF16))
        po_ref[:, cols] = y * sc_ref[:, cols]


def _pool_decode(z, cache_pool, w_pool, pool_scale, layer):
    nb = z.shape[0]
    ng = len(POOL_WINDOWS)
    bshape = (nb, POOL_BUF, POOL_WIDTH)
    return pl.pallas_call(
        functools.partial(_pool_dec_kernel, nb=nb),
        out_shape=(jax.ShapeDtypeStruct((nb, POOL_WIDTH), F32),
                   jax.ShapeDtypeStruct(bshape, F32)),
        grid=(1,),
        in_specs=[
            pl.BlockSpec(z.shape, lambda i: (0, 0)),
            pl.BlockSpec((None,) + bshape, lambda i: (layer, 0, 0, 0)),
            pl.BlockSpec((None, ng, POOL_GROUP, POOL_GROUP), lambda i: (layer, 0, 0, 0)),
            pl.BlockSpec((None, 1, POOL_WIDTH), lambda i: (layer, 0, 0)),
        ],
        out_specs=(pl.BlockSpec((nb, POOL_WIDTH), lambda i: (0, 0)),
                   pl.BlockSpec(bshape, lambda i: (0, 0, 0))),
        scratch_shapes=[pltpu.VMEM((nb, POOL_WIDTH), F32)],
        compiler_params=_params("arbitrary"),
        name="pool_mixer_decode",
    )(z, cache_pool, w_pool, pool_scale)


def _trunk_prompt(x, mod, p, wb, final_g):
    b, s, d = x.shape
    depth = p["w_in"].shape[0]
    pos = jnp.arange(s, dtype=F32)
    ret_tabs = _ret_rope_tables(pos)
    att_tabs = _att_rope_tables(pos)
    keep = tuple(min(win, s) for win, _ in ATT_PATTERNS)
    tm = math.gcd(s, 1024)
    xf = x.reshape(b * s, d)
    rets, pools = [], []
    ks = [[] for _ in ATT_PATTERNS]
    vs = [[] for _ in ATT_PATTERNS]
    for l in range(depth):
        xf = _ffn(xf, mod, l, 0, s, p["norm_g"], *wb[l]["ffn1"], None, tm, 512)
        z = _inproj(xf, mod, l, s, p["norm_g"], wb[l]["w_in"], math.gcd(s, 256), IN_WIDTH)
        z3 = z.reshape(b, s, IN_WIDTH)
        ro, ret_s = _retention_prompt(z3, *ret_tabs, p["ret_norm_g"], l)
        ao, k_rot = _attention_prompt(z3, att_tabs)
        po = _pool_prompt(z3, p["w_pool"], p["pool_scale"], l)
        xf = _outproj(xf, mod, l, s, ro.reshape(b * s, -1), ao.reshape(b * s, -1),
                      po.reshape(b * s, -1), p["w_out"], math.gcd(s, 512))
        xf = _ffn(xf, mod, l, 2, s, p["norm_g"], *wb[l]["ffn2"],
                  final_g if l == depth - 1 else None, tm, 512)
        rets.append(ret_s)
        pools.append(z3[:, s - POOL_BUF:, COL_PU:])
        for g in range(N_PAT):
            c0 = COL_AV + g * ATT_GROUP
            ks[g].append(k_rot[:, g, s - keep[g]:].reshape(b, keep[g], ATT_HEADS, ATT_DIM))
            vs[g].append(z3[:, s - keep[g]:, c0:c0 + ATT_GROUP].reshape(b, keep[g], ATT_HEADS, ATT_DIM))
    y = xf.reshape(b, s, d)
    return (y, jnp.stack(rets), [jnp.stack(k) for k in ks], [jnp.stack(v) for v in vs],
            jnp.stack(pools))


def _trunk_decode(x, mod, pos0, caches, p, final_g):
    nb, s, d = x.shape
    assert s == 1, "decode trunk handles one new token per batch row"
    depth = p["w_in"].shape[0]
    state_ret, cks, cvs, cpool = caches
    pos = pos0 + jnp.arange(s, dtype=F32)
    ret_tabs = _ret_rope_tables(pos)
    att_tabs = _att_rope_tables(pos)
    cks = [c.reshape(c.shape[0], nb, c.shape[2], ATT_GROUP) for c in cks]
    cvs = [c.reshape(c.shape[0], nb, c.shape[2], ATT_GROUP) for c in cvs]
    xf = x.reshape(nb, d)
    rets, pools, wb = [], [], []
    ks = [[] for _ in ATT_PATTERNS]
    vs = [[] for _ in ATT_PATTERNS]
    for l in range(depth):
        xf, *ffn1_bf16 = _ffn(xf, mod, l, 0, 1, p["norm_g"], p["w1_gate"], p["w1_up"], p["w1_down"],
                              None, nb, 512, emit_bf16=True)
        z, w_in_bf16 = _inproj(xf, mod, l, 1, p["norm_g"], p["w_in"], nb, 256, emit_bf16=True)
        ro, ret_s = _retention_decode(z, state_ret, *ret_tabs, p["ret_norm_g"], l)
        att = _attention_decode(z, att_tabs, cks, cvs, l)
        ao, k_new, v_new = att[0].reshape(nb, ATT_GROUP), att[1:1 + N_PAT], att[1 + N_PAT:]
        po, pool_new = _pool_decode(z, cpool, p["w_pool"], p["pool_scale"], l)
        xf = _outproj(xf, mod, l, 1, ro, ao, po, p["w_out"], nb)
        xf, *ffn2_bf16 = _ffn(xf, mod, l, 2, 1, p["norm_g"], p["w2_gate"], p["w2_up"], p["w2_down"],
                              final_g if l == depth - 1 else None, nb, 512, emit_bf16=True)
        wb.append({"ffn1": ffn1_bf16, "w_in": w_in_bf16, "ffn2": ffn2_bf16})
        rets.append(ret_s)
        pools.append(pool_new)
        for g in range(N_PAT):
            ks[g].append(k_new[g].reshape(nb, -1, ATT_HEADS, ATT_DIM))
            vs[g].append(v_new[g].reshape(nb, -1, ATT_HEADS, ATT_DIM))
    y = xf.reshape(nb, s, d)
    return (y, jnp.stack(rets), [jnp.stack(k) for k in ks], [jnp.stack(v) for v in vs],
            jnp.stack(pools), wb)


def kernel(x_prompt, x_sample, state_ret, cache_k_w128, cache_v_w128, cache_k_w512, cache_v_w512,
           cache_k_w2048, cache_v_w2048, cache_pool, c_prompt, c_sample, w_ada, b_ada, norm_g, w_in,
           ret_norm_g, w_pool, pool_scale, w_out, w1_gate, w1_up, w1_down, w2_gate, w2_up, w2_down,
           final_norm_g):
    depth, d = norm_g.shape[0], norm_g.shape[-1]
    n_pr, n_dec = c_prompt.shape[0], c_sample.shape[0]
    p = {
        "norm_g": norm_g.reshape(depth, N_SUB, 1, d),
        "w_in": w_in,
        "ret_norm_g": ret_norm_g.reshape(depth, 1, RET_WIDTH),
        "w_pool": w_pool,
        "pool_scale": pool_scale.reshape(depth, 1, POOL_WIDTH),
        "w_out": w_out,
        "w1_gate": w1_gate, "w1_up": w1_up, "w1_down": w1_down,
        "w2_gate": w2_gate, "w2_up": w2_up, "w2_down": w2_down,
    }
    final_g = final_norm_g.reshape(1, d)

    pad = (-(n_dec + n_pr)) % SUBLANES
    c_all = jnp.concatenate([c_sample, c_prompt, jnp.zeros((pad, d), F32)], axis=0)
    mod_dec, mod_pr = _ada(c_all, n_dec, n_pr, w_ada, b_ada)

    caches = (state_ret,
              (cache_k_w128, cache_k_w512, cache_k_w2048),
              (cache_v_w128, cache_v_w512, cache_v_w2048),
              cache_pool)
    y_s, ret_s, ks, vs, pool_s, wb = _trunk_decode(x_sample, mod_dec, float(PAST_LEN), caches, p, final_g)
    y_p, ret_p, kp, vp, pool_p = _trunk_prompt(x_prompt, mod_pr, p, wb, final_g)
    return (y_p, y_s, ret_p, ret_s,
            kp[0], ks[0], vp[0], vs[0],
            kp[1], ks[1], vp[1], vs[1],
            kp[2], ks[2], vp[2], vs[2],
            pool_p, pool_s)
```

```python
import functools
import math

import jax
import jax.numpy as jnp
from jax import lax
from jax.experimental import pallas as pl
from jax.experimental.pallas import tpu as pltpu

F32 = jnp.float32
BF16 = jnp.bfloat16

RET_HEADS = 6
RET_DIM = 128
RET_WIDTH = RET_HEADS * RET_DIM
RET_CHUNK = 128
RET_THETA = 10000.0
ATT_HEADS = 4
ATT_DIM = 64
ATT_GROUP = ATT_HEADS * ATT_DIM
ATT_PATTERNS = ((128, 1), (512, 4), (2048, 16))
N_PAT = len(ATT_PATTERNS)
ATT_WIDTH = N_PAT * ATT_GROUP
ROPE_THETA = 500000.0
ROPE_DIMS = ATT_DIM // 4
ROPE_HALF = ROPE_DIMS // 2
QUERY_BLOCK = 128
POOL_WINDOWS = (2, 4, 8, 16)
POOL_GROUP = 128
POOL_WIDTH = len(POOL_WINDOWS) * POOL_GROUP
POOL_BUF = max(POOL_WINDOWS) - 1
POOL_PAD = 16
N_SUB = 3
PAST_LEN = 16384
HALF_STEP = 0.5
EPS = 1e-6
MASK_VALUE = -1e30

COL_RQ, COL_RK, COL_RV, COL_RG = 0, RET_WIDTH, 2 * RET_WIDTH, 3 * RET_WIDTH
COL_AQ = 4 * RET_WIDTH
COL_AK = COL_AQ + ATT_WIDTH
COL_AV = COL_AK + ATT_WIDTH
COL_PU = COL_AV + ATT_WIDTH
IN_WIDTH = COL_PU + POOL_WIDTH

V7X_VMEM_BYTES = 64 * 1024 * 1024
VMEM_LIMIT = V7X_VMEM_BYTES - 8 * 1024 * 1024
SUBLANES = 8

LOG_GAMMA = tuple(math.log1p(-(2.0 ** (-5.0 - h))) for h in range(RET_HEADS))


def _params(*sem):
    return pltpu.CompilerParams(dimension_semantics=sem, vmem_limit_bytes=VMEM_LIMIT)


def _dot(a, b):
    return jnp.dot(a, b, preferred_element_type=F32)


def _dot_nt(a, b):
    return lax.dot_general(a, b, (((1,), (1,)), ((), ())), preferred_element_type=F32)


def _bf16_round(x):
    return x.astype(BF16).astype(F32)


def _ada_kernel(c_ref, w_ref, b_ref, od_ref, op_ref, *, n_dec, n_pr):
    c = c_ref[...]
    a = (c * jax.nn.sigmoid(c)).astype(BF16)
    res = _dot(a, w_ref[...].astype(BF16)) + b_ref[...]
    od_ref[...] = res[0:n_dec]
    for b in range(n_pr):
        op_ref[b] = res[n_dec + b:n_dec + b + 1]


def _ada(c_all, n_dec, n_pr, w_ada, b_ada, tn=1024):
    depth, d, n = w_ada.shape
    rows = c_all.shape[0]
    per = d // tn
    return pl.pallas_call(
        functools.partial(_ada_kernel, n_dec=n_dec, n_pr=n_pr),
        out_shape=(jax.ShapeDtypeStruct((depth, N_SUB * 3, n_dec, d), F32),
                   jax.ShapeDtypeStruct((depth, N_SUB * 3, n_pr, 1, d), F32)),
        grid=(depth, n // tn),
        in_specs=[
            pl.BlockSpec((rows, d), lambda l, j: (0, 0)),
            pl.BlockSpec((None, d, tn), lambda l, j: (l, 0, j)),
            pl.BlockSpec((None, 1, tn), lambda l, j: (l, 0, j)),
        ],
        out_specs=(
            pl.BlockSpec((None, None, n_dec, tn), lambda l, j: (l, j // per, 0, j % per)),
            pl.BlockSpec((None, None, n_pr, 1, tn), lambda l, j: (l, j // per, 0, 0, j % per)),
        ),
        compiler_params=_params("arbitrary", "arbitrary"),
        name="ada_mod",
    )(c_all, w_ada, b_ada.reshape(depth, 1, n))


def _rmsnorm(x, g):
    ms = jnp.mean(x * x, axis=-1, keepdims=True)
    return x * lax.rsqrt(ms + EPS) * g


def _row_chunks(tm):
    rc = min(tm, 128)
    return rc, tm // rc


def _mod_rows(ref, r, rc):
    return ref[...] if ref.shape[0] == 1 else ref[pl.ds(r, rc), :]


def _prenorm_to(h_ref, x_ref, g_ref, sh_ref, sc_ref, unrolled=False):
    rc, n = _row_chunks(x_ref.shape[0])

    def body(i, carry):
        r = i * rc if unrolled else pl.multiple_of(i * rc, rc)
        y = _rmsnorm(x_ref[pl.ds(r, rc), :], g_ref[...])
        h = y * (1.0 + _mod_rows(sc_ref, r, rc)) + _mod_rows(sh_ref, r, rc)
        h_ref[pl.ds(r, rc), :] = h.astype(BF16)
        return carry

    if unrolled:
        for i in range(n):
            body(i, 0)
    else:
        lax.fori_loop(0, n, body, 0)


def _mod_specs(mod, layer, sub, rows_per_batch, tm, grid_rank):
    d = mod.shape[-1]
    specs = []
    for k in range(3):
        j = sub * 3 + k
        if mod.ndim == 5:
            if grid_rank == 2:
                idx = (lambda j: lambda i, f: (layer, j, (i * tm) // rows_per_batch, 0, 0))(j)
            else:
                idx = (lambda j: lambda i: (layer, j, (i * tm) // rows_per_batch, 0, 0))(j)
            specs.append(pl.BlockSpec((None, None, None, 1, d), idx))
        else:
            if grid_rank == 2:
                idx = (lambda j: lambda i, f: (layer, j, 0, 0))(j)
            else:
                idx = (lambda j: lambda i: (layer, j, 0, 0))(j)
            specs.append(pl.BlockSpec((None, None, tm, d), idx))
    return specs


def _ffn_kernel(x_ref, sh_ref, sc_ref, gt_ref, g_ref, wg_ref, wu_ref, wd_ref, *rest,
                n_f, final_norm, emit_bf16):
    rest = list(rest)
    fg_ref = rest.pop(0) if final_norm else None
    o_ref = rest.pop(0)
    wb_refs = [rest.pop(0) for _ in range(3)] if emit_bf16 else None
    h_ref, = rest
    f = pl.program_id(1)

    @pl.when(f == 0)
    def _():
        _prenorm_to(h_ref, x_ref, g_ref, sh_ref, sc_ref)
        o_ref[...] = jnp.zeros_like(o_ref)

    wg = wg_ref[...].astype(BF16)
    wu = wu_ref[...].astype(BF16)
    wd = wd_ref[...].astype(BF16)
    if emit_bf16:
        wb_refs[0][...] = wg
        wb_refs[1][...] = wu
        wb_refs[2][...] = wd
    h = h_ref[...]
    gate = _dot(h, wg)
    up = _dot(h, wu)
    act = (gate * jax.nn.sigmoid(gate) * up).astype(BF16)
    d = o_ref.shape[1]
    dc = min(d, 512)
    for c0 in range(0, d, dc):
        o_ref[:, c0:c0 + dc] += _dot(act, wd[:, c0:c0 + dc])

    @pl.when(f == n_f - 1)
    def _():
        rc, n = _row_chunks(x_ref.shape[0])

        def body(i, carry):
            r = pl.multiple_of(i * rc, rc)
            rows = pl.ds(r, rc)
            out = x_ref[rows, :] + HALF_STEP * _mod_rows(gt_ref, r, rc) * o_ref[rows, :]
            if final_norm:
                out = _rmsnorm(out, fg_ref[...])
            o_ref[rows, :] = out
            return carry

        lax.fori_loop(0, n, body, 0)


def _weight_spec(w, layer, block, index, resident=False):
    mode = {"pipeline_mode": pl.Buffered(1)} if resident else {}
    if w.ndim == 3:
        return pl.BlockSpec((None,) + block, lambda i, j: (layer,) + index(i, j), **mode)
    return pl.BlockSpec(block, index, **mode)


def _ffn(x, mod, layer, sub, rows_per_batch, norm_g, wg, wu, wd, final_g, tm, tf, emit_bf16=False):
    m, d = x.shape
    d_ff = wg.shape[-1]
    assert m % tm == 0 and d_ff % tf == 0
    assert not emit_bf16 or m == tm, "each weight block must be visited exactly once"
    n_f = d_ff // tf
    final_norm = final_g is not None
    col_block = lambda i, j: (0, j)
    row_block = lambda i, j: (j, 0)
    in_specs = [pl.BlockSpec((tm, d), lambda i, j: (i, 0))]
    in_specs += _mod_specs(mod, layer, sub, rows_per_batch, tm, 2)
    in_specs += [
        pl.BlockSpec((None, None, 1, d), lambda i, j: (layer, sub, 0, 0)),
        _weight_spec(wg, layer, (d, tf), col_block),
        _weight_spec(wu, layer, (d, tf), col_block),
        _weight_spec(wd, layer, (tf, d), row_block),
    ]
    args = [x, mod, mod, mod, norm_g, wg, wu, wd]
    if final_norm:
        in_specs.append(pl.BlockSpec((1, d), lambda i, j: (0, 0)))
        args.append(final_g)
    out_shape = [jax.ShapeDtypeStruct((m, d), F32)]
    out_specs = [pl.BlockSpec((tm, d), lambda i, j: (i, 0))]
    if emit_bf16:
        out_shape += [jax.ShapeDtypeStruct((d, d_ff), BF16), jax.ShapeDtypeStruct((d, d_ff), BF16),
                      jax.ShapeDtypeStruct((d_ff, d), BF16)]
        out_specs += [pl.BlockSpec((d, tf), col_block), pl.BlockSpec((d, tf), col_block),
                      pl.BlockSpec((tf, d), row_block)]
    out = pl.pallas_call(
        functools.partial(_ffn_kernel, n_f=n_f, final_norm=final_norm, emit_bf16=emit_bf16),
        out_shape=tuple(out_shape),
        grid=(m // tm, n_f),
        in_specs=in_specs,
        out_specs=tuple(out_specs),
        scratch_shapes=[pltpu.VMEM((tm, d), BF16)],
        compiler_params=_params("parallel", "arbitrary"),
        name="ffn",
    )(*args)
    return out if emit_bf16 else out[0]


def _inproj_kernel(x_ref, sh_ref, sc_ref, g_ref, w_ref, o_ref, *rest, emit_bf16, single_col_tile):
    wb_ref, h_ref = rest if emit_bf16 else (None,) + rest

    if single_col_tile:
        _prenorm_to(h_ref, x_ref, g_ref, sh_ref, sc_ref, unrolled=True)
    else:
        @pl.when(pl.program_id(1) == 0)
        def _():
            _prenorm_to(h_ref, x_ref, g_ref, sh_ref, sc_ref)

    if emit_bf16:
        wb_ref[...] = w_ref[...].astype(BF16)
    h = h_ref[...]
    tn = o_ref.shape[1]
    nc = min(tn, 512)
    for c0 in range(0, tn, nc):
        c1 = min(c0 + nc, tn)
        o_ref[:, c0:c1] = _dot(h, w_ref[:, c0:c1].astype(BF16))


def _inproj(x, mod, layer, rows_per_batch, norm_g, w_in, tm, tn, emit_bf16=False):
    m, d = x.shape
    n = w_in.shape[-1]
    assert m % tm == 0 and n % tn == 0
    assert not emit_bf16 or m == tm, "each weight block must be visited exactly once"
    sh, sc, _ = _mod_specs(mod, layer, 1, rows_per_batch, tm, 2)
    col_block = lambda i, j: (0, j)
    out_shape = [jax.ShapeDtypeStruct((m, n), F32)]
    out_specs = [pl.BlockSpec((tm, tn), lambda i, j: (i, j))]
    if emit_bf16:
        out_shape.append(jax.ShapeDtypeStruct((d, n), BF16))
        out_specs.append(pl.BlockSpec((d, tn), col_block))
    out = pl.pallas_call(
        functools.partial(_inproj_kernel, emit_bf16=emit_bf16, single_col_tile=(n == tn)),
        out_shape=tuple(out_shape),
        grid=(m // tm, n // tn),
        in_specs=[
            pl.BlockSpec((tm, d), lambda i, j: (i, 0)),
            sh, sc,
            pl.BlockSpec((None, None, 1, d), lambda i, j: (layer, 1, 0, 0)),
            _weight_spec(w_in, layer, (d, tn), col_block, resident=(n == tn)),
        ],
        out_specs=tuple(out_specs),
        scratch_shapes=[pltpu.VMEM((tm, d), BF16)],
        compiler_params=_params("parallel", "arbitrary"),
        name="in_proj",
    )(x, mod, mod, norm_g, w_in)
    return out if emit_bf16 else out[0]


def _outproj_kernel(x_ref, gt_ref, ro_ref, ao_ref, po_ref, w_ref, o_ref, wb_ref):
    @pl.when(pl.program_id(0) == 0)
    def _():
        wb_ref[...] = w_ref[...].astype(BF16)

    r0, r1 = RET_WIDTH, RET_WIDTH + ATT_GROUP
    y = _dot(ro_ref[...].astype(BF16), wb_ref[0:r0, :])
    y += _dot(ao_ref[...].astype(BF16), wb_ref[r0:r1, :])
    y += _dot(po_ref[...].astype(BF16), wb_ref[r1:, :])
    o_ref[...] = x_ref[...] + gt_ref[...] * y


def _outproj(x, mod, layer, rows_per_batch, ro, ao, po, w_out, tm):
    m, d = x.shape
    k = w_out.shape[1]
    _, _, gt = _mod_specs(mod, layer, 1, rows_per_batch, tm, 1)
    return pl.pallas_call(
        _outproj_kernel,
        out_shape=jax.ShapeDtypeStruct((m, d), F32),
        grid=(m // tm,),
        in_specs=[
            pl.BlockSpec((tm, d), lambda i: (i, 0)),
            gt,
            pl.BlockSpec((tm, RET_WIDTH), lambda i: (i, 0)),
            pl.BlockSpec((tm, ATT_GROUP), lambda i: (i, 0)),
            pl.BlockSpec((tm, POOL_WIDTH), lambda i: (i, 0)),
            pl.BlockSpec((None, k, d), lambda i: (layer, 0, 0), pipeline_mode=pl.Buffered(1)),
        ],
        out_specs=pl.BlockSpec((tm, d), lambda i: (i, 0)),
        scratch_shapes=[pltpu.VMEM((k, d), BF16)],
        compiler_params=_params("arbitrary"),
        name="out_proj",
    )(x, mod, ro, ao, po, w_out)


def _ret_rope_tables(pos):
    half = RET_DIM // 2
    freq = jnp.power(jnp.float32(RET_THETA), -jnp.arange(half, dtype=F32) / half)
    ang = pos[:, None] * freq[None, :]
    cos, sin = jnp.cos(ang), jnp.sin(ang)
    return jnp.concatenate([cos, cos], axis=-1), jnp.concatenate([-sin, sin], axis=-1)


def _att_rope_tables(pos):
    freq = jnp.power(jnp.float32(ROPE_THETA), -jnp.arange(ROPE_HALF, dtype=F32) / ROPE_HALF)
    ang = pos[:, None] * freq[None, :]
    cos, sin = jnp.cos(ang), jnp.sin(ang)
    s = pos.shape[0]
    rest = ATT_DIM - ROPE_DIMS
    c = jnp.concatenate([cos, cos, jnp.ones((s, rest), F32)], axis=-1)
    a = jnp.concatenate([-sin, jnp.zeros((s, ATT_DIM - ROPE_HALF), F32)], axis=-1)
    b = jnp.concatenate([jnp.zeros((s, ROPE_HALF), F32), sin, jnp.zeros((s, rest), F32)], axis=-1)
    tile = lambda t: jnp.tile(t, (1, 128 // ATT_DIM))
    return tile(c), tile(a), tile(b)


def _ret_rope(x, cos_t, sin_t):
    return x * cos_t + pltpu.roll(x, RET_DIM // 2, axis=1) * sin_t


def _att_rope(x, c, a, b):
    n = x.shape[-1]
    return x * c + pltpu.roll(x, n - ROPE_HALF, axis=1) * a + pltpu.roll(x, ROPE_HALF, axis=1) * b


def _head_norm(o, g):
    mu = jnp.mean(o, axis=-1, keepdims=True)
    oc = o - mu
    var = jnp.mean(oc * oc, axis=-1, keepdims=True)
    return oc * lax.rsqrt(var + EPS) * g


def _ret_kernel(zq_ref, zk_ref, zv_ref, zg_ref, cos_ref, sin_ref, gn_ref,
                ro_ref, so_ref, din_ref, dq_ref, dk_ref, *, chunk, per_step):
    c = pl.program_id(1)

    @pl.when(c == 0)
    def _():
        so_ref[...] = jnp.zeros_like(so_ref)
        row = lax.broadcasted_iota(jnp.int32, (chunk, chunk), 0).astype(F32)
        col = lax.broadcasted_iota(jnp.int32, (chunk, chunk), 1).astype(F32)
        diff = row - col
        rowd = lax.broadcasted_iota(jnp.int32, (chunk, RET_DIM), 0).astype(F32)
        for h in range(RET_HEADS):
            lg = LOG_GAMMA[h]
            din_ref[h] = jnp.where(diff >= 0, jnp.exp(jnp.maximum(diff, 0.0) * lg), 0.0)
            dq_ref[h] = jnp.exp((rowd + 1.0) * lg)
            dk_ref[h] = jnp.exp((chunk - 1.0 - rowd) * lg)

    for h in range(RET_HEADS):
        cols = slice(h * RET_DIM, (h + 1) * RET_DIM)
        s_cur = so_ref[h]
        for j in range(per_step):
            rows = slice(j * chunk, (j + 1) * chunk)
            cos_t = cos_ref[rows, :]
            sin_t = sin_ref[rows, :]
            q = _ret_rope(zq_ref[rows, cols], cos_t, sin_t)
            k = _ret_rope(zk_ref[rows, cols], cos_t, sin_t) * (RET_DIM ** -0.5)
            qb = q.astype(BF16)
            kb = k.astype(BF16)
            vb = zv_ref[rows, cols].astype(BF16)
            a = _dot_nt(qb, kb) * din_ref[h]
            o = _dot(a.astype(BF16), vb) + _dot(qb, s_cur.astype(BF16)) * dq_ref[h]
            kd_t = (k * dk_ref[h]).T.astype(BF16)
            s_cur = math.exp(chunk * LOG_GAMMA[h]) * s_cur + _dot(kd_t, vb)

            on = _head_norm(o, gn_ref[:, cols])
            g = zg_ref[rows, cols]
            ro_ref[rows, cols] = (g * jax.nn.sigmoid(g) * on).astype(BF16)
        so_ref[h] = s_cur


def _retention_prompt(z3, cos_t, sin_t, ret_norm_g, layer, per_step=4):
    b, s, _ = z3.shape
    chunk = math.gcd(s, RET_CHUNK)
    per_step = math.gcd(s // chunk, per_step)
    rows = chunk * per_step
    zspec = lambda cb: pl.BlockSpec((None, rows, RET_WIDTH), lambda i, c: (i, c, cb))
    return pl.pallas_call(
        functools.partial(_ret_kernel, chunk=chunk, per_step=per_step),
        out_shape=(jax.ShapeDtypeStruct((b, s, RET_WIDTH), BF16),
                   jax.ShapeDtypeStruct((b, RET_HEADS, RET_DIM, RET_DIM), F32)),
        grid=(b, s // rows),
        in_specs=[
            zspec(COL_RQ // RET_WIDTH), zspec(COL_RK // RET_WIDTH),
            zspec(COL_RV // RET_WIDTH), zspec(COL_RG // RET_WIDTH),
            pl.BlockSpec((rows, RET_DIM), lambda i, c: (c, 0)),
            pl.BlockSpec((rows, RET_DIM), lambda i, c: (c, 0)),
            pl.BlockSpec((None, 1, RET_WIDTH), lambda i, c: (layer, 0, 0)),
        ],
        out_specs=(
            pl.BlockSpec((None, rows, RET_WIDTH), lambda i, c: (i, c, 0)),
            pl.BlockSpec((None, RET_HEADS, RET_DIM, RET_DIM), lambda i, c: (i, 0, 0, 0)),
        ),
        scratch_shapes=[
            pltpu.VMEM((RET_HEADS, chunk, chunk), F32),
            pltpu.VMEM((RET_HEADS, chunk, RET_DIM), F32),
            pltpu.VMEM((RET_HEADS, chunk, RET_DIM), F32),
        ],
        compiler_params=_params("parallel", "arbitrary"),
        name="retention",
    )(z3, z3, z3, z3, cos_t, sin_t, ret_norm_g)


def _ret_dec_kernel(z_ref, s0_ref, cos_ref, sin_ref, gn_ref, ro_ref, so_ref, *, nb):
    cos_t = cos_ref[...]
    sin_t = sin_ref[...]
    row = lax.broadcasted_iota(jnp.int32, (nb, RET_DIM), 0)
    for h in range(RET_HEADS):
        gamma = math.exp(LOG_GAMMA[h])
        q = _ret_rope(z_ref[:, COL_RQ + h * RET_DIM:COL_RQ + (h + 1) * RET_DIM], cos_t, sin_t)
        k = _ret_rope(z_ref[:, COL_RK + h * RET_DIM:COL_RK + (h + 1) * RET_DIM], cos_t, sin_t)
        k = k * (RET_DIM ** -0.5)
        v = z_ref[:, COL_RV + h * RET_DIM:COL_RV + (h + 1) * RET_DIM]
        qr, kr, vr = _bf16_round(q), _bf16_round(k), _bf16_round(v)
        qk = jnp.sum(qr * kr, axis=-1, keepdims=True)
        o = _bf16_round(qk) * vr
        cross = jnp.zeros((nb, RET_DIM), F32)
        for b in range(nb):
            s_old = s0_ref[b, h]
            res = _dot(qr.astype(BF16), s_old.astype(BF16))
            cross = cross + jnp.where(row == b, res, 0.0)
            k_col = jnp.broadcast_to(kr[b:b + 1, :], (RET_DIM, RET_DIM)).T
            so_ref[b, h] = gamma * s_old + k_col * vr[b:b + 1, :]
        o = o + cross * gamma
        cols = slice(h * RET_DIM, (h + 1) * RET_DIM)
        on = _head_norm(o, gn_ref[:, cols])
        g = z_ref[:, COL_RG + h * RET_DIM:COL_RG + (h + 1) * RET_DIM]
        ro_ref[:, cols] = g * jax.nn.sigmoid(g) * on


def _retention_decode(z, state, cos_t, sin_t, ret_norm_g, layer):
    nb = z.shape[0]
    sshape = (nb, RET_HEADS, RET_DIM, RET_DIM)
    return pl.pallas_call(
        functools.partial(_ret_dec_kernel, nb=nb),
        out_shape=(jax.ShapeDtypeStruct((nb, RET_WIDTH), F32),
                   jax.ShapeDtypeStruct(sshape, F32)),
        grid=(1,),
        in_specs=[
            pl.BlockSpec(z.shape, lambda i: (0, 0)),
            pl.BlockSpec((None,) + sshape, lambda i: (layer, 0, 0, 0, 0)),
            pl.BlockSpec((1, RET_DIM), lambda i: (0, 0)),
            pl.BlockSpec((1, RET_DIM), lambda i: (0, 0)),
            pl.BlockSpec((None, 1, RET_WIDTH), lambda i: (layer, 0, 0)),
        ],
        out_specs=(
            pl.BlockSpec((nb, RET_WIDTH), lambda i: (0, 0)),
            pl.BlockSpec(sshape, lambda i: (0, 0, 0, 0)),
        ),
        compiler_params=_params("arbitrary"),
        name="retention_decode",
    )(z, state, cos_t, sin_t, ret_norm_g)


ATT_HALF = 128
ATT_MAX_STRIDE = 4
N_HALF = ATT_GROUP // ATT_HALF
HEADS_PER_HALF = ATT_HALF // ATT_DIM


def _half_cols(hf):
    return slice(hf * ATT_HALF, (hf + 1) * ATT_HALF)


def _att_kernel(*refs, seq):
    zq = refs[0:N_HALF]
    zk = refs[N_HALF:2 * N_HALF]
    zv = refs[2 * N_HALF:3 * N_HALF]
    (c_ref, a_ref, b_ref, ao_ref, ko_ref,
     q_ref, k_ref, o_ref, lse_ref, st_ref, ost_ref) = refs[3 * N_HALF:]
    g = pl.program_id(1)
    qb = QUERY_BLOCK
    n_blocks = seq // qb

    def rope_body(i, carry):
        r = pl.multiple_of(i * qb, qb)
        rows = pl.ds(r, qb)
        c, a, b = c_ref[rows, :], a_ref[rows, :], b_ref[rows, :]
        for hf in range(N_HALF):
            q_ref[hf, rows, :] = _att_rope(zq[hf][rows, :], c, a, b) * (ATT_DIM ** -0.5)
            k_rot = _att_rope(zk[hf][rows, :], c, a, b)
            k_ref[hf, rows, :] = k_rot
            ko_ref[rows, _half_cols(hf)] = k_rot
        return carry

    lax.fori_loop(0, n_blocks, rope_body, 0)

    tq = lax.broadcasted_iota(jnp.int32, (qb, qb), 0)
    tk = lax.broadcasted_iota(jnp.int32, (qb, qb), 1)
    cur_valid = tk <= tq
    prev_valid = tk >= tq
    lane = lax.broadcasted_iota(jnp.int32, (qb, ATT_HALF), 1)
    head_masks = [(lane // ATT_DIM) == hh for hh in range(HEADS_PER_HALF)]

    def group_body(gi, dil):
        has_prev = True
        d1 = min(dil, ATT_MAX_STRIDE)
        d2 = dil // d1
        staged = d2 > 1
        assert d2 <= ATT_MAX_STRIDE and d1 * d2 == dil
        sub_len = seq // d1

        def strided(start, n, stride):
            return pl.ds(start, n, stride=stride) if stride > 1 else pl.ds(start, n)

        if staged:
            for hf in range(N_HALF):
                for r1 in range(d1):
                    src = strided(r1, sub_len, d1)
                    st_ref[0, hf, r1] = q_ref[hf, src, :]
                    st_ref[1, hf, r1] = k_ref[hf, src, :]
                    st_ref[2, hf, r1] = zv[hf][src, :]

        def load(which, hf, cls, blk):
            if staged:
                rows = strided(cls // d1 + d2 * qb * blk, qb, d2)
                return st_ref[which, hf, cls % d1, rows, :].astype(BF16)
            rows = strided(cls + dil * qb * blk, qb, dil)
            src = (q_ref, k_ref, None)[which]
            return (zv[hf][rows, :] if src is None else src[hf, rows, :]).astype(BF16)

        def store(which, hf, cls, blk, val):
            if staged:
                rows = strided(cls // d1 + d2 * qb * blk, qb, d2)
                ost_ref[which, hf, cls % d1, rows, :] = val
            else:
                rows = strided(cls + dil * qb * blk, qb, dil)
                (o_ref, lse_ref)[which][gi, hf, rows, :] = val

        def block_body(t, carry):
            cls = t % dil
            blk = t // dil
            valid = cur_valid
            if has_prev:
                pblk = jnp.maximum(blk - 1, 0)
                valid = jnp.concatenate([prev_valid & (blk > 0), cur_valid], axis=1)
            for hf in range(N_HALF):
                q = load(0, hf, cls, blk)
                keys = load(1, hf, cls, blk)
                vals = load(2, hf, cls, blk)
                if has_prev:
                    keys = jnp.concatenate([load(1, hf, cls, pblk), keys], axis=0)
                    vals = jnp.concatenate([load(2, hf, cls, pblk), vals], axis=0)
                vals_ones = jnp.concatenate([vals, jnp.ones_like(vals)], axis=1)
                num = jnp.zeros((qb, ATT_HALF), F32)
                den = jnp.ones((qb, ATT_HALF), F32)
                lse_acc = jnp.zeros((qb, ATT_HALF), F32)
                for hm in head_masks:
                    qh = jnp.where(hm, q, jnp.zeros_like(q))
                    s = jnp.where(valid, _dot_nt(qh, keys), MASK_VALUE)
                    m = jnp.max(s, axis=-1, keepdims=True)
                    e = jnp.exp(s - m).astype(BF16)
                    r = _dot(e, vals_ones)
                    l = r[:, ATT_HALF:]
                    num = jnp.where(hm, r[:, :ATT_HALF], num)
                    den = jnp.where(hm, l, den)
                    lse_acc = jnp.where(hm, m + jnp.log(l), lse_acc)
                store(0, hf, cls, blk, num / den)
                store(1, hf, cls, blk, lse_acc)
            return carry

        lax.fori_loop(0, n_blocks, block_body, 0, unroll=2)

        if staged:
            for hf in range(N_HALF):
                for r1 in range(d1):
                    dst = strided(r1, sub_len, d1)
                    o_ref[gi, hf, dst, :] = ost_ref[0, hf, r1]
                    lse_ref[gi, hf, dst, :] = ost_ref[1, hf, r1]

    for gi, (_, dil) in enumerate(ATT_PATTERNS):
        @pl.when(g == gi)
        def _(gi=gi, dil=dil):
            group_body(gi, dil)

    @pl.when(g == N_PAT - 1)
    def _():
        def merge_body(i, carry):
            r = pl.multiple_of(i * qb, qb)
            rows = pl.ds(r, qb)
            for hf in range(N_HALF):
                lses = [lse_ref[gi, hf, rows, :] for gi in range(N_PAT)]
                m = functools.reduce(jnp.maximum, lses)
                ws = [jnp.exp(l - m) for l in lses]
                den = functools.reduce(lambda x, y: x + y, ws)
                num = functools.reduce(lambda x, y: x + y,
                                       [w * o_ref[gi, hf, rows, :] for gi, w in enumerate(ws)])
                ao_ref[rows, _half_cols(hf)] = (num / den).astype(BF16)
            return carry

        lax.fori_loop(0, n_blocks, merge_body, 0)


def _attention_prompt(z3, tabs):
    b, s, _ = z3.shape
    assert all(s % (dil * QUERY_BLOCK) == 0 for _, dil in ATT_PATTERNS)
    zspec = lambda col, hf: pl.BlockSpec(
        (None, s, ATT_HALF), lambda i, g: (i, 0, col // ATT_HALF + N_HALF * g + hf))
    zspecs = [zspec(col, hf) for col in (COL_AQ, COL_AK, COL_AV) for hf in range(N_HALF)]
    tspec = pl.BlockSpec((s, ATT_HALF), lambda i, g: (0, 0), pipeline_mode=pl.Buffered(1))
    return pl.pallas_call(
        functools.partial(_att_kernel, seq=s),
        out_shape=(jax.ShapeDtypeStruct((b, s, ATT_GROUP), BF16),
                   jax.ShapeDtypeStruct((b, N_PAT, s, ATT_GROUP), F32)),
        grid=(b, N_PAT),
        in_specs=zspecs + [tspec, tspec, tspec],
        out_specs=(
            pl.BlockSpec((None, s, ATT_GROUP), lambda i, g: (i, 0, 0)),
            pl.BlockSpec((None, None, s, ATT_GROUP), lambda i, g: (i, g, 0, 0)),
        ),
        scratch_shapes=[
            pltpu.VMEM((N_HALF, s, ATT_HALF), F32),
            pltpu.VMEM((N_HALF, s, ATT_HALF), F32),
            pltpu.VMEM((N_PAT, N_HALF, s, ATT_HALF), F32),
            pltpu.VMEM((N_PAT, N_HALF, s, ATT_HALF), F32),
            pltpu.VMEM((3, N_HALF, ATT_MAX_STRIDE, s // ATT_MAX_STRIDE, ATT_HALF), F32),
            pltpu.VMEM((2, N_HALF, ATT_MAX_STRIDE, s // ATT_MAX_STRIDE, ATT_HALF), F32),
        ],
        compiler_params=_params("parallel", "arbitrary"),
        name="dilated_attention",
    )(*([z3] * (3 * N_HALF)), *tabs)


def _shift_append(src_ref, dst_ref, cols, new_row):
    w = src_ref.shape[0]
    step = 512
    body = w - SUBLANES
    for a in range(0, body, step):
        n = min(step, body - a)
        dst_ref[a:a + n, cols] = src_ref[a + 1:a + 1 + n, :]
    tail = pltpu.roll(src_ref[body:w, :], SUBLANES - 1, axis=0)
    row = lax.broadcasted_iota(jnp.int32, tail.shape, 0)
    dst_ref[body:w, cols] = jnp.where(row == SUBLANES - 1, new_row, tail)


def _att_dec_kernel(z_ref, c_ref, a_ref, b_ref, *refs):
    n = N_PAT * N_HALF
    kc_refs = refs[0:n]
    vc_refs = refs[n:2 * n]
    ao_ref = refs[2 * n]
    ko_refs = refs[2 * n + 1:2 * n + 1 + N_PAT]
    vo_refs = refs[2 * n + 1 + N_PAT:2 * n + 1 + 2 * N_PAT]
    bi = pl.program_id(0)
    rope = (c_ref[...], a_ref[...], b_ref[...])
    shape8 = (SUBLANES, ATT_HALF)
    row8 = lax.broadcasted_iota(jnp.int32, shape8, 0)
    lane8 = lax.broadcasted_iota(jnp.int32, shape8, 1)
    own_head = (lane8 // ATT_DIM) == row8

    for hf in range(N_HALF):
        outs, lses = [], []
        for gi, (win, dil) in enumerate(ATT_PATTERNS):
            off = gi * ATT_GROUP + hf * ATT_HALF
            zrow = lambda col: z_ref[:, col + off:col + off + ATT_HALF]
            q = _att_rope(zrow(COL_AQ), *rope) * (ATT_DIM ** -0.5)
            k_new = _att_rope(zrow(COL_AK), *rope)
            v_new = zrow(COL_AV)
            n_keys = win // dil
            kc, vc = kc_refs[gi * N_HALF + hf], vc_refs[gi * N_HALF + hf]
            rows = pl.ds(0, n_keys, stride=dil) if dil > 1 else pl.ds(0, n_keys)
            kd = kc[rows, :].astype(BF16)
            vd = vc[rows, :].astype(BF16)
            q8 = _bf16_round(jnp.where(own_head, jnp.broadcast_to(q, shape8), 0.0))
            s_old = _dot_nt(q8.astype(BF16), kd)
            s_new = jnp.sum(q8 * _bf16_round(k_new), axis=-1, keepdims=True)
            m = jnp.maximum(jnp.max(s_old, axis=-1, keepdims=True), s_new)
            e_old = jnp.exp(s_old - m)
            e_new = jnp.exp(s_new - m)
            l = jnp.sum(e_old, axis=-1, keepdims=True) + e_new
            pv = _dot(e_old.astype(BF16), vd) + _bf16_round(e_new) * _bf16_round(v_new)
            outs.append(pv / l)
            lses.append(m + jnp.log(l))
            _shift_append(kc, ko_refs[gi], _half_cols(hf), k_new)
            _shift_append(vc, vo_refs[gi], _half_cols(hf), v_new)

        m = functools.reduce(jnp.maximum, lses)
        ws = [jnp.exp(l - m) for l in lses]
        den = functools.reduce(lambda x, y: x + y, ws)
        num = functools.reduce(lambda x, y: x + y, [w * o for w, o in zip(ws, outs)])
        merged = jnp.where(own_head, num / den, 0.0)
        ao_ref[:, _half_cols(hf)] = jnp.sum(merged, axis=0, keepdims=True)


def _attention_decode(z, tabs, k_caches, v_caches, layer):
    nb = z.shape[0]
    cspec = lambda w, hf: pl.BlockSpec((None, None, w, ATT_HALF), lambda i: (layer, i, 0, hf))
    ospec = lambda w: pl.BlockSpec((None, w, ATT_GROUP), lambda i: (i, 0, 0))
    tspec = pl.BlockSpec((1, ATT_HALF), lambda i: (0, 0))
    widths = [kc.shape[2] for kc in k_caches]
    cspecs = [cspec(w, hf) for w in widths for hf in range(N_HALF)]
    cache_shapes = [jax.ShapeDtypeStruct((nb, w, ATT_GROUP), F32) for w in widths]
    halves = lambda caches: [c for c in caches for _ in range(N_HALF)]
    return pl.pallas_call(
        _att_dec_kernel,
        out_shape=tuple([jax.ShapeDtypeStruct((nb, 1, ATT_GROUP), F32)] + cache_shapes + cache_shapes),
        grid=(nb,),
        in_specs=[pl.BlockSpec((None, 1, z.shape[1]), lambda i: (i, 0, 0)), tspec, tspec, tspec]
                 + cspecs + cspecs,
        out_specs=tuple([pl.BlockSpec((None, 1, ATT_GROUP), lambda i: (i, 0, 0))]
                        + [ospec(w) for w in widths] + [ospec(w) for w in widths]),
        compiler_params=_params("arbitrary"),
        name="dilated_attention_decode",
    )(z.reshape(nb, 1, z.shape[1]), *tabs, *halves(k_caches), *halves(v_caches))


def _pool_kernel(u_ref, w_ref, sc_ref, po_ref, a_ref, b_ref, *, seq):
    g = pl.program_id(1)
    body = pl.ds(POOL_PAD, seq)

    def window_mean_minus_token(win):
        x = u_ref[...]
        a_ref[0:POOL_PAD, :] = jnp.zeros((POOL_PAD, POOL_GROUP), F32)
        b_ref[0:POOL_PAD, :] = jnp.zeros((POOL_PAD, POOL_GROUP), F32)
        a_ref[body, :] = x
        src, dst = a_ref, b_ref
        k = 1
        while k < win:
            dst[body, :] = src[body, :] + src[pl.ds(POOL_PAD - k, seq), :]
            src, dst = dst, src
            k *= 2
        t = lax.broadcasted_iota(jnp.int32, (seq, POOL_GROUP), 0)
        cnt = jnp.minimum(t + 1, win).astype(F32)
        pooled = src[body, :] / cnt - x
        y = _dot(pooled.astype(BF16), w_ref[...].astype(BF16)) * sc_ref[...]
        po_ref[...] = y.astype(BF16)

    for gi, win in enumerate(POOL_WINDOWS):
        @pl.when(g == gi)
        def _(win=win):
            window_mean_minus_token(win)


def _pool_prompt(z3, w_pool, pool_scale, layer):
    b, s, _ = z3.shape
    ng = len(POOL_WINDOWS)
    return pl.pallas_call(
        functools.partial(_pool_kernel, seq=s),
        out_shape=jax.ShapeDtypeStruct((b, s, POOL_WIDTH), BF16),
        grid=(b, ng),
        in_specs=[
            pl.BlockSpec((None, s, POOL_GROUP), lambda i, g: (i, 0, COL_PU // POOL_GROUP + g)),
            pl.BlockSpec((None, None, POOL_GROUP, POOL_GROUP), lambda i, g: (layer, g, 0, 0)),
            pl.BlockSpec((None, 1, POOL_GROUP), lambda i, g: (layer, 0, g)),
        ],
        out_specs=pl.BlockSpec((None, s, POOL_GROUP), lambda i, g: (i, 0, g)),
        scratch_shapes=[pltpu.VMEM((POOL_PAD + s, POOL_GROUP), F32),
                        pltpu.VMEM((POOL_PAD + s, POOL_GROUP), F32)],
        compiler_params=_params("parallel", "arbitrary"),
        name="pool_mixer",
    )(z3, w_pool, pool_scale)


def _pool_dec_kernel(z_ref, buf_ref, w_ref, sc_ref, po_ref, bo_ref, pooled_ref, *, nb):
    row = lax.broadcasted_iota(jnp.int32, (POOL_BUF, POOL_GROUP), 0)
    for b in range(nb):
        u = z_ref[b:b + 1, COL_PU:COL_PU + POOL_WIDTH]
        old = buf_ref[b]
        for gi, win in enumerate(POOL_WINDOWS):
            cols = slice(gi * POOL_GROUP, (gi + 1) * POOL_GROUP)
            tail = jnp.where(row >= POOL_BUF - (win - 1), old[:, cols], 0.0)
            total = jnp.sum(tail, axis=0, keepdims=True) + u[:, cols]
            pooled_ref[b:b + 1, cols] = total / float(win) - u[:, cols]
        bo_ref[b, 0:POOL_BUF - 1, :] = old[1:POOL_BUF, :]
        bo_ref[b, POOL_BUF - 1:POOL_BUF, :] = u
    for gi in range(len(POOL_WINDOWS)):
        cols = slice(gi * POOL_GROUP, (gi + 1) * POOL_GROUP)
        y = _dot(pooled_ref[:, cols].astype(BF16), w_ref[gi].astype(BF16))
        po_ref[:, cols] = y * sc_ref[:, cols]


def _pool_decode(z, cache_pool, w_pool, pool_scale, layer):
    nb = z.shape[0]
    ng = len(POOL_WINDOWS)
    bshape = (nb, POOL_BUF, POOL_WIDTH)
    return pl.pallas_call(
        functools.partial(_pool_dec_kernel, nb=nb),
        out_shape=(jax.ShapeDtypeStruct((nb, POOL_WIDTH), F32),
                   jax.ShapeDtypeStruct(bshape, F32)),
        grid=(1,),
        in_specs=[
            pl.BlockSpec(z.shape, lambda i: (0, 0)),
            pl.BlockSpec((None,) + bshape, lambda i: (layer, 0, 0, 0)),
            pl.BlockSpec((None, ng, POOL_GROUP, POOL_GROUP), lambda i: (layer, 0, 0, 0)),
            pl.BlockSpec((None, 1, POOL_WIDTH), lambda i: (layer, 0, 0)),
        ],
        out_specs=(pl.BlockSpec((nb, POOL_WIDTH), lambda i: (0, 0)),
                   pl.BlockSpec(bshape, lambda i: (0, 0, 0))),
        scratch_shapes=[pltpu.VMEM((nb, POOL_WIDTH), F32)],
        compiler_params=_params("arbitrary"),
        name="pool_mixer_decode",
    )(z, cache_pool, w_pool, pool_scale)


def _trunk_prompt(x, mod, p, wb, final_g):
    b, s, d = x.shape
    depth = p["w_in"].shape[0]
    pos = jnp.arange(s, dtype=F32)
    ret_tabs = _ret_rope_tables(pos)
    att_tabs = _att_rope_tables(pos)
    keep = tuple(min(win, s) for win, _ in ATT_PATTERNS)
    tm = math.gcd(s, 1024)
    xf = x.reshape(b * s, d)
    rets, pools = [], []
    ks = [[] for _ in ATT_PATTERNS]
    vs = [[] for _ in ATT_PATTERNS]
    for l in range(depth):
        xf = _ffn(xf, mod, l, 0, s, p["norm_g"], *wb[l]["ffn1"], None, tm, 512)
        z = _inproj(xf, mod, l, s, p["norm_g"], wb[l]["w_in"], math.gcd(s, 256), IN_WIDTH)
        z3 = z.reshape(b, s, IN_WIDTH)
        ro, ret_s = _retention_prompt(z3, *ret_tabs, p["ret_norm_g"], l)
        ao, k_rot = _attention_prompt(z3, att_tabs)
        po = _pool_prompt(z3, p["w_pool"], p["pool_scale"], l)
        xf = _outproj(xf, mod, l, s, ro.reshape(b * s, -1), ao.reshape(b * s, -1),
                      po.reshape(b * s, -1), p["w_out"], math.gcd(s, 512))
        xf = _ffn(xf, mod, l, 2, s, p["norm_g"], *wb[l]["ffn2"],
                  final_g if l == depth - 1 else None, tm, 512)
        rets.append(ret_s)
        pools.append(z3[:, s - POOL_BUF:, COL_PU:])
        for g in range(N_PAT):
            c0 = COL_AV + g * ATT_GROUP
            ks[g].append(k_rot[:, g, s - keep[g]:].reshape(b, keep[g], ATT_HEADS, ATT_DIM))
            vs[g].append(z3[:, s - keep[g]:, c0:c0 + ATT_GROUP].reshape(b, keep[g], ATT_HEADS, ATT_DIM))
    y = xf.reshape(b, s, d)
    return (y, jnp.stack(rets), [jnp.stack(k) for k in ks], [jnp.stack(v) for v in vs],
            jnp.stack(pools))


def _trunk_decode(x, mod, pos0, caches, p, final_g):
    nb, s, d = x.shape
    assert s == 1, "decode trunk handles one new token per batch row"
    depth = p["w_in"].shape[0]
    state_ret, cks, cvs, cpool = caches
    pos = pos0 + jnp.arange(s, dtype=F32)
    ret_tabs = _ret_rope_tables(pos)
    att_tabs = _att_rope_tables(pos)
    cks = [c.reshape(c.shape[0], nb, c.shape[2], ATT_GROUP) for c in cks]
    cvs = [c.reshape(c.shape[0], nb, c.shape[2], ATT_GROUP) for c in cvs]
    xf = x.reshape(nb, d)
    rets, pools, wb = [], [], []
    ks = [[] for _ in ATT_PATTERNS]
    vs = [[] for _ in ATT_PATTERNS]
    for l in range(depth):
        xf, *ffn1_bf16 = _ffn(xf, mod, l, 0, 1, p["norm_g"], p["w1_gate"], p["w1_up"], p["w1_down"],
                              None, nb, 512, emit_bf16=True)
        z, w_in_bf16 = _inproj(xf, mod, l, 1, p["norm_g"], p["w_in"], nb, 256, emit_bf16=True)
        ro, ret_s = _retention_decode(z, state_ret, *ret_tabs, p["ret_norm_g"], l)
        att = _attention_decode(z, att_tabs, cks, cvs, l)
        ao, k_new, v_new = att[0].reshape(nb, ATT_GROUP), att[1:1 + N_PAT], att[1 + N_PAT:]
        po, pool_new = _pool_decode(z, cpool, p["w_pool"], p["pool_scale"], l)
        xf = _outproj(xf, mod, l, 1, ro, ao, po, p["w_out"], nb)
        xf, *ffn2_bf16 = _ffn(xf, mod, l, 2, 1, p["norm_g"], p["w2_gate"], p["w2_up"], p["w2_down"],
                              final_g if l == depth - 1 else None, nb, 512, emit_bf16=True)
        wb.append({"ffn1": ffn1_bf16, "w_in": w_in_bf16, "ffn2": ffn2_bf16})
        rets.append(ret_s)
        pools.append(pool_new)
        for g in range(N_PAT):
            ks[g].append(k_new[g].reshape(nb, -1, ATT_HEADS, ATT_DIM))
            vs[g].append(v_new[g].reshape(nb, -1, ATT_HEADS, ATT_DIM))
    y = xf.reshape(nb, s, d)
    return (y, jnp.stack(rets), [jnp.stack(k) for k in ks], [jnp.stack(v) for v in vs],
            jnp.stack(pools), wb)


def kernel(x_prompt, x_sample, state_ret, cache_k_w128, cache_v_w128, cache_k_w512, cache_v_w512,
           cache_k_w2048, cache_v_w2048, cache_pool, c_prompt, c_sample, w_ada, b_ada, norm_g, w_in,
           ret_norm_g, w_pool, pool_scale, w_out, w1_gate, w1_up, w1_down, w2_gate, w2_up, w2_down,
           final_norm_g):
    depth, d = norm_g.shape[0], norm_g.shape[-1]
    n_pr, n_dec = c_prompt.shape[0], c_sample.shape[0]
    p = {
        "norm_g": norm_g.reshape(depth, N_SUB, 1, d),
        "w_in": w_in,
        "ret_norm_g": ret_norm_g.reshape(depth, 1, RET_WIDTH),
        "w_pool": w_pool,
        "pool_scale": pool_scale.reshape(depth, 1, POOL_WIDTH),
        "w_out": w_out,
        "w1_gate": w1_gate, "w1_up": w1_up, "w1_down": w1_down,
        "w2_gate": w2_gate, "w2_up": w2_up, "w2_down": w2_down,
    }
    final_g = final_norm_g.reshape(1, d)

    pad = (-(n_dec + n_pr)) % SUBLANES
    c_all = jnp.concatenate([c_sample, c_prompt, jnp.zeros((pad, d), F32)], axis=0)
    mod_dec, mod_pr = _ada(c_all, n_dec, n_pr, w_ada, b_ada)

    caches = (state_ret,
              (cache_k_w128, cache_k_w512, cache_k_w2048),
              (cache_v_w128, cache_v_w512, cache_v_w2048),
              cache_pool)
    y_s, ret_s, ks, vs, pool_s, wb = _trunk_decode(x_sample, mod_dec, float(PAST_LEN), caches, p, final_g)
    y_p, ret_p, kp, vp, pool_p = _trunk_prompt(x_prompt, mod_pr, p, wb, final_g)
    return (y_p, y_s, ret_p, ret_s,
            kp[0], ks[0], vp[0], vs[0],
            kp[1], ks[1], vp[1], vs[1],
            kp[2], ks[2], vp[2], vs[2],
            pool_p, pool_s)
```

```python
import functools
import math

import jax
import jax.numpy as jnp
from jax import lax
from jax.experimental import pallas as pl
from jax.experimental.pallas import tpu as pltpu

F32 = jnp.float32
BF16 = jnp.bfloat16

RET_HEADS = 6
RET_DIM = 128
RET_WIDTH = RET_HEADS * RET_DIM
RET_CHUNK = 128
RET_THETA = 10000.0
ATT_HEADS = 4
ATT_DIM = 64
ATT_GROUP = ATT_HEADS * ATT_DIM
ATT_PATTERNS = ((128, 1), (512, 4), (2048, 16))
N_PAT = len(ATT_PATTERNS)
ATT_WIDTH = N_PAT * ATT_GROUP
ROPE_THETA = 500000.0
ROPE_DIMS = ATT_DIM // 4
ROPE_HALF = ROPE_DIMS // 2
QUERY_BLOCK = 128
POOL_WINDOWS = (2, 4, 8, 16)
POOL_GROUP = 128
POOL_WIDTH = len(POOL_WINDOWS) * POOL_GROUP
POOL_BUF = max(POOL_WINDOWS) - 1
POOL_PAD = 16
N_SUB = 3
PAST_LEN = 16384
HALF_STEP = 0.5
EPS = 1e-6
MASK_VALUE = -1e30

COL_RQ, COL_RK, COL_RV, COL_RG = 0, RET_WIDTH, 2 * RET_WIDTH, 3 * RET_WIDTH
COL_AQ = 4 * RET_WIDTH
COL_AK = COL_AQ + ATT_WIDTH
COL_AV = COL_AK + ATT_WIDTH
COL_PU = COL_AV + ATT_WIDTH
IN_WIDTH = COL_PU + POOL_WIDTH

V7X_VMEM_BYTES = 64 * 1024 * 1024
VMEM_LIMIT = V7X_VMEM_BYTES - 8 * 1024 * 1024
SUBLANES = 8

LOG_GAMMA = tuple(math.log1p(-(2.0 ** (-5.0 - h))) for h in range(RET_HEADS))


def _params(*sem):
    return pltpu.CompilerParams(dimension_semantics=sem, vmem_limit_bytes=VMEM_LIMIT)


def _dot(a, b):
    return jnp.dot(a, b, preferred_element_type=F32)


def _dot_nt(a, b):
    return lax.dot_general(a, b, (((1,), (1,)), ((), ())), preferred_element_type=F32)


def _bf16_round(x):
    return x.astype(BF16).astype(F32)


def _ada_kernel(c_ref, w_ref, b_ref, od_ref, op_ref, *, n_dec, n_pr):
    c = c_ref[...]
    a = (c * jax.nn.sigmoid(c)).astype(BF16)
    res = _dot(a, w_ref[...].astype(BF16)) + b_ref[...]
    od_ref[...] = res[0:n_dec]
    for b in range(n_pr):
        op_ref[b] = res[n_dec + b:n_dec + b + 1]


def _ada(c_all, n_dec, n_pr, w_ada, b_ada, tn=1024):
    depth, d, n = w_ada.shape
    rows = c_all.shape[0]
    per = d // tn
    return pl.pallas_call(
        functools.partial(_ada_kernel, n_dec=n_dec, n_pr=n_pr),
        out_shape=(jax.ShapeDtypeStruct((depth, N_SUB * 3, n_dec, d), F32),
                   jax.ShapeDtypeStruct((depth, N_SUB * 3, n_pr, 1, d), F32)),
        grid=(depth, n // tn),
        in_specs=[
            pl.BlockSpec((rows, d), lambda l, j: (0, 0)),
            pl.BlockSpec((None, d, tn), lambda l, j: (l, 0, j)),
            pl.BlockSpec((None, 1, tn), lambda l, j: (l, 0, j)),
        ],
        out_specs=(
            pl.BlockSpec((None, None, n_dec, tn), lambda l, j: (l, j // per, 0, j % per)),
            pl.BlockSpec((None, None, n_pr, 1, tn), lambda l, j: (l, j // per, 0, 0, j % per)),
        ),
        compiler_params=_params("arbitrary", "arbitrary"),
        name="ada_mod",
    )(c_all, w_ada, b_ada.reshape(depth, 1, n))


def _rmsnorm(x, g):
    ms = jnp.mean(x * x, axis=-1, keepdims=True)
    return x * lax.rsqrt(ms + EPS) * g


def _row_chunks(tm):
    rc = min(tm, 128)
    return rc, tm // rc


def _mod_rows(ref, r, rc):
    return ref[...] if ref.shape[0] == 1 else ref[pl.ds(r, rc), :]


def _prenorm_to(h_ref, x_ref, g_ref, sh_ref, sc_ref, unrolled=False):
    rc, n = _row_chunks(x_ref.shape[0])

    def body(i, carry):
        r = i * rc if unrolled else pl.multiple_of(i * rc, rc)
        y = _rmsnorm(x_ref[pl.ds(r, rc), :], g_ref[...])
        h = y * (1.0 + _mod_rows(sc_ref, r, rc)) + _mod_rows(sh_ref, r, rc)
        h_ref[pl.ds(r, rc), :] = h.astype(BF16)
        return carry

    if unrolled:
        for i in range(n):
            body(i, 0)
    else:
        lax.fori_loop(0, n, body, 0)


def _mod_specs(mod, layer, sub, rows_per_batch, tm, grid_rank):
    d = mod.shape[-1]
    specs = []
    for k in range(3):
        j = sub * 3 + k
        if mod.ndim == 5:
            if grid_rank == 2:
                idx = (lambda j: lambda i, f: (layer, j, (i * tm) // rows_per_batch, 0, 0))(j)
            else:
                idx = (lambda j: lambda i: (layer, j, (i * tm) // rows_per_batch, 0, 0))(j)
            specs.append(pl.BlockSpec((None, None, None, 1, d), idx))
        else:
            if grid_rank == 2:
                idx = (lambda j: lambda i, f: (layer, j, 0, 0))(j)
            else:
                idx = (lambda j: lambda i: (layer, j, 0, 0))(j)
            specs.append(pl.BlockSpec((None, None, tm, d), idx))
    return specs


def _ffn_kernel(x_ref, sh_ref, sc_ref, gt_ref, g_ref, wg_ref, wu_ref, wd_ref, *rest,
                n_f, final_norm, emit_bf16):
    rest = list(rest)
    fg_ref = rest.pop(0) if final_norm else None
    o_ref = rest.pop(0)
    wb_refs = [rest.pop(0) for _ in range(3)] if emit_bf16 else None
    h_ref, = rest
    f = pl.program_id(1)

    @pl.when(f == 0)
    def _():
        _prenorm_to(h_ref, x_ref, g_ref, sh_ref, sc_ref)
        o_ref[...] = jnp.zeros_like(o_ref)

    wg = wg_ref[...].astype(BF16)
    wu = wu_ref[...].astype(BF16)
    wd = wd_ref[...].astype(BF16)
    if emit_bf16:
        wb_refs[0][...] = wg
        wb_refs[1][...] = wu
        wb_refs[2][...] = wd
    h = h_ref[...]
    gate = _dot(h, wg)
    up = _dot(h, wu)
    act = (gate * jax.nn.sigmoid(gate) * up).astype(BF16)
    d = o_ref.shape[1]
    dc = min(d, 512)
    for c0 in range(0, d, dc):
        o_ref[:, c0:c0 + dc] += _dot(act, wd[:, c0:c0 + dc])

    @pl.when(f == n_f - 1)
    def _():
        rc, n = _row_chunks(x_ref.shape[0])

        def body(i, carry):
            r = pl.multiple_of(i * rc, rc)
            rows = pl.ds(r, rc)
            out = x_ref[rows, :] + HALF_STEP * _mod_rows(gt_ref, r, rc) * o_ref[rows, :]
            if final_norm:
                out = _rmsnorm(out, fg_ref[...])
            o_ref[rows, :] = out
            return carry

        lax.fori_loop(0, n, body, 0)


def _weight_spec(w, layer, block, index, resident=False):
    mode = {"pipeline_mode": pl.Buffered(1)} if resident else {}
    if w.ndim == 3:
        return pl.BlockSpec((None,) + block, lambda i, j: (layer,) + index(i, j), **mode)
    return pl.BlockSpec(block, index, **mode)


def _ffn(x, mod, layer, sub, rows_per_batch, norm_g, wg, wu, wd, final_g, tm, tf, emit_bf16=False):
    m, d = x.shape
    d_ff = wg.shape[-1]
    assert m % tm == 0 and d_ff % tf == 0
    assert not emit_bf16 or m == tm, "each weight block must be visited exactly once"
    n_f = d_ff // tf
    final_norm = final_g is not None
    col_block = lambda i, j: (0, j)
    row_block = lambda i, j: (j, 0)
    in_specs = [pl.BlockSpec((tm, d), lambda i, j: (i, 0))]
    in_specs += _mod_specs(mod, layer, sub, rows_per_batch, tm, 2)
    in_specs += [
        pl.BlockSpec((None, None, 1, d), lambda i, j: (layer, sub, 0, 0)),
        _weight_spec(wg, layer, (d, tf), col_block),
        _weight_spec(wu, layer, (d, tf), col_block),
        _weight_spec(wd, layer, (tf, d), row_block),
    ]
    args = [x, mod, mod, mod, norm_g, wg, wu, wd]
    if final_norm:
        in_specs.append(pl.BlockSpec((1, d), lambda i, j: (0, 0)))
        args.append(final_g)
    out_shape = [jax.ShapeDtypeStruct((m, d), F32)]
    out_specs = [pl.BlockSpec((tm, d), lambda i, j: (i, 0))]
    if emit_bf16:
        out_shape += [jax.ShapeDtypeStruct((d, d_ff), BF16), jax.ShapeDtypeStruct((d, d_ff), BF16),
                      jax.ShapeDtypeStruct((d_ff, d), BF16)]
        out_specs += [pl.BlockSpec((d, tf), col_block), pl.BlockSpec((d, tf), col_block),
                      pl.BlockSpec((tf, d), row_block)]
    out = pl.pallas_call(
        functools.partial(_ffn_kernel, n_f=n_f, final_norm=final_norm, emit_bf16=emit_bf16),
        out_shape=tuple(out_shape),
        grid=(m // tm, n_f),
        in_specs=in_specs,
        out_specs=tuple(out_specs),
        scratch_shapes=[pltpu.VMEM((tm, d), BF16)],
        compiler_params=_params("parallel", "arbitrary"),
        name="ffn",
    )(*args)
    return out if emit_bf16 else out[0]


def _rotate_projection_block(y, col, ret_tabs, att_tabs):
    if col < COL_RV:
        y = _ret_rope(y, *ret_tabs)
        return y * (RET_DIM ** -0.5) if col >= COL_RK else y
    if COL_AQ <= col < COL_AV:
        y = _att_rope(y, *att_tabs)
        return y * (ATT_DIM ** -0.5) if col < COL_AK else y
    return y


def _inproj_kernel(x_ref, sh_ref, sc_ref, g_ref, w_ref, *rest, emit_bf16, single_col_tile, rotate):
    rest = list(rest)
    tab_refs = [rest.pop(0) for _ in range(5)] if rotate else None
    o_ref = rest.pop(0)
    wb_ref = rest.pop(0) if emit_bf16 else None
    h_ref, = rest

    if single_col_tile:
        _prenorm_to(h_ref, x_ref, g_ref, sh_ref, sc_ref, unrolled=True)
    else:
        @pl.when(pl.program_id(1) == 0)
        def _():
            _prenorm_to(h_ref, x_ref, g_ref, sh_ref, sc_ref)

    if emit_bf16:
        wb_ref[...] = w_ref[...].astype(BF16)
    h = h_ref[...]
    tn = o_ref.shape[1]
    nc = min(tn, 512)
    for c0 in range(0, tn, nc):
        c1 = min(c0 + nc, tn)
        y = _dot(h, w_ref[:, c0:c1].astype(BF16))
        if rotate:
            ret_tabs = (tab_refs[0][...], tab_refs[1][...])
            att_tabs = (tab_refs[2][...], tab_refs[3][...], tab_refs[4][...])
            for b0 in range(0, c1 - c0, 128):
                o_ref[:, c0 + b0:c0 + b0 + 128] = _rotate_projection_block(
                    y[:, b0:b0 + 128], c0 + b0, ret_tabs, att_tabs)
        else:
            o_ref[:, c0:c1] = y


def _inproj(x, mod, layer, rows_per_batch, norm_g, w_in, tm, tn, emit_bf16=False, rope_tabs=None):
    m, d = x.shape
    n = w_in.shape[-1]
    rotate = rope_tabs is not None
    assert m % tm == 0 and n % tn == 0
    assert not emit_bf16 or m == tm, "each weight block must be visited exactly once"
    assert not rotate or (n == tn and rows_per_batch % tm == 0)
    sh, sc, _ = _mod_specs(mod, layer, 1, rows_per_batch, tm, 2)
    col_block = lambda i, j: (0, j)
    in_specs = [
        pl.BlockSpec((tm, d), lambda i, j: (i, 0)),
        sh, sc,
        pl.BlockSpec((None, None, 1, d), lambda i, j: (layer, 1, 0, 0)),
        _weight_spec(w_in, layer, (d, tn), col_block, resident=(n == tn)),
    ]
    args = [x, mod, mod, norm_g, w_in]
    if rotate:
        tiles_per_batch = rows_per_batch // tm
        in_specs += [pl.BlockSpec((tm, 128), lambda i, j: (i % tiles_per_batch, 0))] * len(rope_tabs)
        args += list(rope_tabs)
    out_shape = [jax.ShapeDtypeStruct((m, n), F32)]
    out_specs = [pl.BlockSpec((tm, tn), lambda i, j: (i, j))]
    if emit_bf16:
        out_shape.append(jax.ShapeDtypeStruct((d, n), BF16))
        out_specs.append(pl.BlockSpec((d, tn), col_block))
    out = pl.pallas_call(
        functools.partial(_inproj_kernel, emit_bf16=emit_bf16, single_col_tile=(n == tn),
                          rotate=rotate),
        out_shape=tuple(out_shape),
        grid=(m // tm, n // tn),
        in_specs=in_specs,
        out_specs=tuple(out_specs),
        scratch_shapes=[pltpu.VMEM((tm, d), BF16)],
        compiler_params=_params("parallel", "arbitrary"),
        name="in_proj",
    )(*args)
    return out if emit_bf16 else out[0]


def _outproj_kernel(x_ref, gt_ref, ro_ref, ao_ref, po_ref, w_ref, o_ref, wb_ref):
    @pl.when(pl.program_id(0) == 0)
    def _():
        wb_ref[...] = w_ref[...].astype(BF16)

    r0, r1 = RET_WIDTH, RET_WIDTH + ATT_GROUP
    y = _dot(ro_ref[...].astype(BF16), wb_ref[0:r0, :])
    y += _dot(ao_ref[...].astype(BF16), wb_ref[r0:r1, :])
    y += _dot(po_ref[...].astype(BF16), wb_ref[r1:, :])
    o_ref[...] = x_ref[...] + gt_ref[...] * y


def _outproj(x, mod, layer, rows_per_batch, ro, ao, po, w_out, tm):
    m, d = x.shape
    k = w_out.shape[1]
    _, _, gt = _mod_specs(mod, layer, 1, rows_per_batch, tm, 1)
    return pl.pallas_call(
        _outproj_kernel,
        out_shape=jax.ShapeDtypeStruct((m, d), F32),
        grid=(m // tm,),
        in_specs=[
            pl.BlockSpec((tm, d), lambda i: (i, 0)),
            gt,
            pl.BlockSpec((tm, RET_WIDTH), lambda i: (i, 0)),
            pl.BlockSpec((tm, ATT_GROUP), lambda i: (i, 0)),
            pl.BlockSpec((tm, POOL_WIDTH), lambda i: (i, 0)),
            pl.BlockSpec((None, k, d), lambda i: (layer, 0, 0), pipeline_mode=pl.Buffered(1)),
        ],
        out_specs=pl.BlockSpec((tm, d), lambda i: (i, 0)),
        scratch_shapes=[pltpu.VMEM((k, d), BF16)],
        compiler_params=_params("arbitrary"),
        name="out_proj",
    )(x, mod, ro, ao, po, w_out)


def _ret_rope_tables(pos):
    half = RET_DIM // 2
    freq = jnp.power(jnp.float32(RET_THETA), -jnp.arange(half, dtype=F32) / half)
    ang = pos[:, None] * freq[None, :]
    cos, sin = jnp.cos(ang), jnp.sin(ang)
    return jnp.concatenate([cos, cos], axis=-1), jnp.concatenate([-sin, sin], axis=-1)


def _att_rope_tables(pos):
    freq = jnp.power(jnp.float32(ROPE_THETA), -jnp.arange(ROPE_HALF, dtype=F32) / ROPE_HALF)
    ang = pos[:, None] * freq[None, :]
    cos, sin = jnp.cos(ang), jnp.sin(ang)
    s = pos.shape[0]
    rest = ATT_DIM - ROPE_DIMS
    c = jnp.concatenate([cos, cos, jnp.ones((s, rest), F32)], axis=-1)
    a = jnp.concatenate([-sin, jnp.zeros((s, ATT_DIM - ROPE_HALF), F32)], axis=-1)
    b = jnp.concatenate([jnp.zeros((s, ROPE_HALF), F32), sin, jnp.zeros((s, rest), F32)], axis=-1)
    tile = lambda t: jnp.tile(t, (1, 128 // ATT_DIM))
    return tile(c), tile(a), tile(b)


def _ret_rope(x, cos_t, sin_t):
    return x * cos_t + pltpu.roll(x, RET_DIM // 2, axis=1) * sin_t


def _att_rope(x, c, a, b):
    n = x.shape[-1]
    return x * c + pltpu.roll(x, n - ROPE_HALF, axis=1) * a + pltpu.roll(x, ROPE_HALF, axis=1) * b


def _head_norm(o, g):
    mu = jnp.mean(o, axis=-1, keepdims=True)
    oc = o - mu
    var = jnp.mean(oc * oc, axis=-1, keepdims=True)
    return oc * lax.rsqrt(var + EPS) * g


def _ret_kernel(zq_ref, zk_ref, zv_ref, zg_ref, gn_ref,
                ro_ref, so_ref, din_ref, dq_ref, dk_ref, *, chunk, per_step):
    c = pl.program_id(1)

    @pl.when(c == 0)
    def _():
        so_ref[...] = jnp.zeros_like(so_ref)
        row = lax.broadcasted_iota(jnp.int32, (chunk, chunk), 0).astype(F32)
        col = lax.broadcasted_iota(jnp.int32, (chunk, chunk), 1).astype(F32)
        diff = row - col
        rowd = lax.broadcasted_iota(jnp.int32, (chunk, RET_DIM), 0).astype(F32)
        for h in range(RET_HEADS):
            lg = LOG_GAMMA[h]
            din_ref[h] = jnp.where(diff >= 0, jnp.exp(jnp.maximum(diff, 0.0) * lg), 0.0)
            dq_ref[h] = jnp.exp((rowd + 1.0) * lg)
            dk_ref[h] = jnp.exp((chunk - 1.0 - rowd) * lg)

    for h in range(RET_HEADS):
        cols = slice(h * RET_DIM, (h + 1) * RET_DIM)
        s_cur = so_ref[h]
        for j in range(per_step):
            rows = slice(j * chunk, (j + 1) * chunk)
            k = zk_ref[rows, cols]
            qb = zq_ref[rows, cols].astype(BF16)
            kb = k.astype(BF16)
            vb = zv_ref[rows, cols].astype(BF16)
            a = _dot_nt(qb, kb) * din_ref[h]
            o = _dot(a.astype(BF16), vb) + _dot(qb, s_cur.astype(BF16)) * dq_ref[h]
            kd_t = (k * dk_ref[h]).T.astype(BF16)
            s_cur = math.exp(chunk * LOG_GAMMA[h]) * s_cur + _dot(kd_t, vb)

            on = _head_norm(o, gn_ref[:, cols])
            g = zg_ref[rows, cols]
            ro_ref[rows, cols] = (g * jax.nn.sigmoid(g) * on).astype(BF16)
        so_ref[h] = s_cur


def _retention_prompt(z3, ret_norm_g, layer, per_step=4):
    b, s, _ = z3.shape
    chunk = math.gcd(s, RET_CHUNK)
    per_step = math.gcd(s // chunk, per_step)
    rows = chunk * per_step
    zspec = lambda cb: pl.BlockSpec((None, rows, RET_WIDTH), lambda i, c: (i, c, cb))
    return pl.pallas_call(
        functools.partial(_ret_kernel, chunk=chunk, per_step=per_step),
        out_shape=(jax.ShapeDtypeStruct((b, s, RET_WIDTH), BF16),
                   jax.ShapeDtypeStruct((b, RET_HEADS, RET_DIM, RET_DIM), F32)),
        grid=(b, s // rows),
        in_specs=[
            zspec(COL_RQ // RET_WIDTH), zspec(COL_RK // RET_WIDTH),
            zspec(COL_RV // RET_WIDTH), zspec(COL_RG // RET_WIDTH),
            pl.BlockSpec((None, 1, RET_WIDTH), lambda i, c: (layer, 0, 0)),
        ],
        out_specs=(
            pl.BlockSpec((None, rows, RET_WIDTH), lambda i, c: (i, c, 0)),
            pl.BlockSpec((None, RET_HEADS, RET_DIM, RET_DIM), lambda i, c: (i, 0, 0, 0)),
        ),
        scratch_shapes=[
            pltpu.VMEM((RET_HEADS, chunk, chunk), F32),
            pltpu.VMEM((RET_HEADS, chunk, RET_DIM), F32),
            pltpu.VMEM((RET_HEADS, chunk, RET_DIM), F32),
        ],
        compiler_params=_params("parallel", "arbitrary"),
        name="retention",
    )(z3, z3, z3, z3, ret_norm_g)


def _ret_dec_kernel(z_ref, s0_ref, cos_ref, sin_ref, gn_ref, ro_ref, so_ref, *, nb):
    cos_t = cos_ref[...]
    sin_t = sin_ref[...]
    row = lax.broadcasted_iota(jnp.int32, (nb, RET_DIM), 0)
    for h in range(RET_HEADS):
        gamma = math.exp(LOG_GAMMA[h])
        q = _ret_rope(z_ref[:, COL_RQ + h * RET_DIM:COL_RQ + (h + 1) * RET_DIM], cos_t, sin_t)
        k = _ret_rope(z_ref[:, COL_RK + h * RET_DIM:COL_RK + (h + 1) * RET_DIM], cos_t, sin_t)
        k = k * (RET_DIM ** -0.5)
        v = z_ref[:, COL_RV + h * RET_DIM:COL_RV + (h + 1) * RET_DIM]
        qr, kr, vr = _bf16_round(q), _bf16_round(k), _bf16_round(v)
        qk = jnp.sum(qr * kr, axis=-1, keepdims=True)
        o = _bf16_round(qk) * vr
        cross = jnp.zeros((nb, RET_DIM), F32)
        for b in range(nb):
            s_old = s0_ref[b, h]
            res = _dot(qr.astype(BF16), s_old.astype(BF16))
            cross = cross + jnp.where(row == b, res, 0.0)
            k_col = jnp.broadcast_to(kr[b:b + 1, :], (RET_DIM, RET_DIM)).T
            so_ref[b, h] = gamma * s_old + k_col * vr[b:b + 1, :]
        o = o + cross * gamma
        cols = slice(h * RET_DIM, (h + 1) * RET_DIM)
        on = _head_norm(o, gn_ref[:, cols])
        g = z_ref[:, COL_RG + h * RET_DIM:COL_RG + (h + 1) * RET_DIM]
        ro_ref[:, cols] = g * jax.nn.sigmoid(g) * on


def _retention_decode(z, state, cos_t, sin_t, ret_norm_g, layer):
    nb = z.shape[0]
    sshape = (nb, RET_HEADS, RET_DIM, RET_DIM)
    return pl.pallas_call(
        functools.partial(_ret_dec_kernel, nb=nb),
        out_shape=(jax.ShapeDtypeStruct((nb, RET_WIDTH), F32),
                   jax.ShapeDtypeStruct(sshape, F32)),
        grid=(1,),
        in_specs=[
            pl.BlockSpec(z.shape, lambda i: (0, 0)),
            pl.BlockSpec((None,) + sshape, lambda i: (layer, 0, 0, 0, 0)),
            pl.BlockSpec((1, RET_DIM), lambda i: (0, 0)),
            pl.BlockSpec((1, RET_DIM), lambda i: (0, 0)),
            pl.BlockSpec((None, 1, RET_WIDTH), lambda i: (layer, 0, 0)),
        ],
        out_specs=(
            pl.BlockSpec((nb, RET_WIDTH), lambda i: (0, 0)),
            pl.BlockSpec(sshape, lambda i: (0, 0, 0, 0)),
        ),
        compiler_params=_params("arbitrary"),
        name="retention_decode",
    )(z, state, cos_t, sin_t, ret_norm_g)


ATT_HALF = 128
ATT_MAX_STRIDE = 4
N_HALF = ATT_GROUP // ATT_HALF
HEADS_PER_HALF = ATT_HALF // ATT_DIM


def _half_cols(hf):
    return slice(hf * ATT_HALF, (hf + 1) * ATT_HALF)


def _att_kernel(*refs, seq):
    zqkv = (refs[0:N_HALF], refs[N_HALF:2 * N_HALF], refs[2 * N_HALF:3 * N_HALF])
    ao_ref, o_ref, lse_ref, st_ref, ost_ref = refs[3 * N_HALF:]
    g = pl.program_id(1)
    qb = QUERY_BLOCK
    n_blocks = seq // qb

    tq = lax.broadcasted_iota(jnp.int32, (qb, qb), 0)
    tk = lax.broadcasted_iota(jnp.int32, (qb, qb), 1)
    cur_valid = tk <= tq
    prev_valid = tk >= tq
    lane = lax.broadcasted_iota(jnp.int32, (qb, ATT_HALF), 1)
    head_masks = [(lane // ATT_DIM) == hh for hh in range(HEADS_PER_HALF)]

    def group_body(gi, dil):
        has_prev = True
        d1 = min(dil, ATT_MAX_STRIDE)
        d2 = dil // d1
        staged = d2 > 1
        assert d2 <= ATT_MAX_STRIDE and d1 * d2 == dil
        sub_len = seq // d1

        def strided(start, n, stride):
            return pl.ds(start, n, stride=stride) if stride > 1 else pl.ds(start, n)

        if staged:
            for hf in range(N_HALF):
                for r1 in range(d1):
                    src = strided(r1, sub_len, d1)
                    for which in range(3):
                        st_ref[which, hf, r1] = zqkv[which][hf][src, :]

        def load(which, hf, cls, blk):
            if staged:
                rows = strided(cls // d1 + d2 * qb * blk, qb, d2)
                return st_ref[which, hf, cls % d1, rows, :].astype(BF16)
            rows = strided(cls + dil * qb * blk, qb, dil)
            return zqkv[which][hf][rows, :].astype(BF16)

        def store(which, hf, cls, blk, val):
            if staged:
                rows = strided(cls // d1 + d2 * qb * blk, qb, d2)
                ost_ref[which, hf, cls % d1, rows, :] = val
            else:
                rows = strided(cls + dil * qb * blk, qb, dil)
                (o_ref, lse_ref)[which][gi, hf, rows, :] = val

        def block_body(t, carry):
            cls = t % dil
            blk = t // dil
            valid = cur_valid
            if has_prev:
                pblk = jnp.maximum(blk - 1, 0)
                valid = jnp.concatenate([prev_valid & (blk > 0), cur_valid], axis=1)
            for hf in range(N_HALF):
                q = load(0, hf, cls, blk)
                keys = load(1, hf, cls, blk)
                vals = load(2, hf, cls, blk)
                if has_prev:
                    keys = jnp.concatenate([load(1, hf, cls, pblk), keys], axis=0)
                    vals = jnp.concatenate([load(2, hf, cls, pblk), vals], axis=0)
                vals_ones = jnp.concatenate([vals, jnp.ones_like(vals)], axis=1)
                num = jnp.zeros((qb, ATT_HALF), F32)
                den = jnp.ones((qb, ATT_HALF), F32)
                lse_acc = jnp.zeros((qb, ATT_HALF), F32)
                for hm in head_masks:
                    qh = jnp.where(hm, q, jnp.zeros_like(q))
                    s = jnp.where(valid, _dot_nt(qh, keys), MASK_VALUE)
                    m = jnp.max(s, axis=-1, keepdims=True)
                    e = jnp.exp(s - m).astype(BF16)
                    r = _dot(e, vals_ones)
                    l = r[:, ATT_HALF:]
                    num = jnp.where(hm, r[:, :ATT_HALF], num)
                    den = jnp.where(hm, l, den)
                    lse_acc = jnp.where(hm, m + jnp.log(l), lse_acc)
                store(0, hf, cls, blk, num / den)
                store(1, hf, cls, blk, lse_acc)
            return carry

        lax.fori_loop(0, n_blocks, block_body, 0, unroll=2)

        if staged:
            for hf in range(N_HALF):
                for r1 in range(d1):
                    dst = strided(r1, sub_len, d1)
                    o_ref[gi, hf, dst, :] = ost_ref[0, hf, r1]
                    lse_ref[gi, hf, dst, :] = ost_ref[1, hf, r1]

    for gi, (_, dil) in enumerate(ATT_PATTERNS):
        @pl.when(g == gi)
        def _(gi=gi, dil=dil):
            group_body(gi, dil)

    @pl.when(g == N_PAT - 1)
    def _():
        def merge_body(i, carry):
            r = pl.multiple_of(i * qb, qb)
            rows = pl.ds(r, qb)
            for hf in range(N_HALF):
                lses = [lse_ref[gi, hf, rows, :] for gi in range(N_PAT)]
                m = functools.reduce(jnp.maximum, lses)
                ws = [jnp.exp(l - m) for l in lses]
                den = functools.reduce(lambda x, y: x + y, ws)
                num = functools.reduce(lambda x, y: x + y,
                                       [w * o_ref[gi, hf, rows, :] for gi, w in enumerate(ws)])
                ao_ref[rows, _half_cols(hf)] = (num / den).astype(BF16)
            return carry

        lax.fori_loop(0, n_blocks, merge_body, 0)


def _attention_prompt(z3):
    b, s, _ = z3.shape
    assert all(s % (dil * QUERY_BLOCK) == 0 for _, dil in ATT_PATTERNS)
    zspec = lambda col, hf: pl.BlockSpec(
        (None, s, ATT_HALF), lambda i, g: (i, 0, col // ATT_HALF + N_HALF * g + hf))
    zspecs = [zspec(col, hf) for col in (COL_AQ, COL_AK, COL_AV) for hf in range(N_HALF)]
    return pl.pallas_call(
        functools.partial(_att_kernel, seq=s),
        out_shape=jax.ShapeDtypeStruct((b, s, ATT_GROUP), BF16),
        grid=(b, N_PAT),
        in_specs=zspecs,
        out_specs=pl.BlockSpec((None, s, ATT_GROUP), lambda i, g: (i, 0, 0)),
        scratch_shapes=[
            pltpu.VMEM((N_PAT, N_HALF, s, ATT_HALF), F32),
            pltpu.VMEM((N_PAT, N_HALF, s, ATT_HALF), F32),
            pltpu.VMEM((3, N_HALF, ATT_MAX_STRIDE, s // ATT_MAX_STRIDE, ATT_HALF), F32),
            pltpu.VMEM((2, N_HALF, ATT_MAX_STRIDE, s // ATT_MAX_STRIDE, ATT_HALF), F32),
        ],
        compiler_params=_params("parallel", "arbitrary"),
        name="dilated_attention",
    )(*([z3] * (3 * N_HALF)))


def _shift_append(src_ref, dst_ref, cols, new_row):
    w = src_ref.shape[0]
    step = 512
    body = w - SUBLANES
    for a in range(0, body, step):
        n = min(step, body - a)
        dst_ref[a:a + n, cols] = src_ref[a + 1:a + 1 + n, :]
    tail = pltpu.roll(src_ref[body:w, :], SUBLANES - 1, axis=0)
    row = lax.broadcasted_iota(jnp.int32, tail.shape, 0)
    dst_ref[body:w, cols] = jnp.where(row == SUBLANES - 1, new_row, tail)


def _att_dec_kernel(z_ref, c_ref, a_ref, b_ref, *refs):
    n = N_PAT * N_HALF
    kc_refs = refs[0:n]
    vc_refs = refs[n:2 * n]
    ao_ref = refs[2 * n]
    ko_refs = refs[2 * n + 1:2 * n + 1 + N_PAT]
    vo_refs = refs[2 * n + 1 + N_PAT:2 * n + 1 + 2 * N_PAT]
    bi = pl.program_id(0)
    rope = (c_ref[...], a_ref[...], b_ref[...])
    shape8 = (SUBLANES, ATT_HALF)
    row8 = lax.broadcasted_iota(jnp.int32, shape8, 0)
    lane8 = lax.broadcasted_iota(jnp.int32, shape8, 1)
    own_head = (lane8 // ATT_DIM) == row8

    for hf in range(N_HALF):
        outs, lses = [], []
        for gi, (win, dil) in enumerate(ATT_PATTERNS):
            off = gi * ATT_GROUP + hf * ATT_HALF
            zrow = lambda col: z_ref[:, col + off:col + off + ATT_HALF]
            q = _att_rope(zrow(COL_AQ), *rope) * (ATT_DIM ** -0.5)
            k_new = _att_rope(zrow(COL_AK), *rope)
            v_new = zrow(COL_AV)
            n_keys = win // dil
            kc, vc = kc_refs[gi * N_HALF + hf], vc_refs[gi * N_HALF + hf]
            rows = pl.ds(0, n_keys, stride=dil) if dil > 1 else pl.ds(0, n_keys)
            kd = kc[rows, :].astype(BF16)
            vd = vc[rows, :].astype(BF16)
            q8 = _bf16_round(jnp.where(own_head, jnp.broadcast_to(q, shape8), 0.0))
            s_old = _dot_nt(q8.astype(BF16), kd)
            s_new = jnp.sum(q8 * _bf16_round(k_new), axis=-1, keepdims=True)
            m = jnp.maximum(jnp.max(s_old, axis=-1, keepdims=True), s_new)
            e_old = jnp.exp(s_old - m)
            e_new = jnp.exp(s_new - m)
            l = jnp.sum(e_old, axis=-1, keepdims=True) + e_new
            pv = _dot(e_old.astype(BF16), vd) + _bf16_round(e_new) * _bf16_round(v_new)
            outs.append(pv / l)
            lses.append(m + jnp.log(l))
            _shift_append(kc, ko_refs[gi], _half_cols(hf), k_new)
            _shift_append(vc, vo_refs[gi], _half_cols(hf), v_new)

        m = functools.reduce(jnp.maximum, lses)
        ws = [jnp.exp(l - m) for l in lses]
        den = functools.reduce(lambda x, y: x + y, ws)
        num = functools.reduce(lambda x, y: x + y, [w * o for w, o in zip(ws, outs)])
        merged = jnp.where(own_head, num / den, 0.0)
        ao_ref[:, _half_cols(hf)] = jnp.sum(merged, axis=0, keepdims=True)


def _attention_decode(z, tabs, k_caches, v_caches, layer):
    nb = z.shape[0]
    cspec = lambda w, hf: pl.BlockSpec((None, None, w, ATT_HALF), lambda i: (layer, i, 0, hf))
    ospec = lambda w: pl.BlockSpec((None, w, ATT_GROUP), lambda i: (i, 0, 0))
    tspec = pl.BlockSpec((1, ATT_HALF), lambda i: (0, 0))
    widths = [kc.shape[2] for kc in k_caches]
    cspecs = [cspec(w, hf) for w in widths for hf in range(N_HALF)]
    cache_shapes = [jax.ShapeDtypeStruct((nb, w, ATT_GROUP), F32) for w in widths]
    halves = lambda caches: [c for c in caches for _ in range(N_HALF)]
    return pl.pallas_call(
        _att_dec_kernel,
        out_shape=tuple([jax.ShapeDtypeStruct((nb, 1, ATT_GROUP), F32)] + cache_shapes + cache_shapes),
        grid=(nb,),
        in_specs=[pl.BlockSpec((None, 1, z.shape[1]), lambda i: (i, 0, 0)), tspec, tspec, tspec]
                 + cspecs + cspecs,
        out_specs=tuple([pl.BlockSpec((None, 1, ATT_GROUP), lambda i: (i, 0, 0))]
                        + [ospec(w) for w in widths] + [ospec(w) for w in widths]),
        compiler_params=_params("arbitrary"),
        name="dilated_attention_decode",
    )(z.reshape(nb, 1, z.shape[1]), *tabs, *halves(k_caches), *halves(v_caches))


def _pool_kernel(u_ref, w_ref, sc_ref, po_ref, a_ref, b_ref, *, seq):
    g = pl.program_id(1)
    body = pl.ds(POOL_PAD, seq)

    def window_mean_minus_token(win):
        x = u_ref[...]
        a_ref[0:POOL_PAD, :] = jnp.zeros((POOL_PAD, POOL_GROUP), F32)
        b_ref[0:POOL_PAD, :] = jnp.zeros((POOL_PAD, POOL_GROUP), F32)
        a_ref[body, :] = x
        src, dst = a_ref, b_ref
        k = 1
        while k < win:
            dst[body, :] = src[body, :] + src[pl.ds(POOL_PAD - k, seq), :]
            src, dst = dst, src
            k *= 2
        t = lax.broadcasted_iota(jnp.int32, (seq, POOL_GROUP), 0)
        cnt = jnp.minimum(t + 1, win).astype(F32)
        pooled = src[body, :] / cnt - x
        y = _dot(pooled.astype(BF16), w_ref[...].astype(BF16)) * sc_ref[...]
        po_ref[...] = y.astype(BF16)

    for gi, win in enumerate(POOL_WINDOWS):
        @pl.when(g == gi)
        def _(win=win):
            window_mean_minus_token(win)


def _pool_prompt(z3, w_pool, pool_scale, layer):
    b, s, _ = z3.shape
    ng = len(POOL_WINDOWS)
    return pl.pallas_call(
        functools.partial(_pool_kernel, seq=s),
        out_shape=jax.ShapeDtypeStruct((b, s, POOL_WIDTH), BF16),
        grid=(b, ng),
        in_specs=[
            pl.BlockSpec((None, s, POOL_GROUP), lambda i, g: (i, 0, COL_PU // POOL_GROUP + g)),
            pl.BlockSpec((None, None, POOL_GROUP, POOL_GROUP), lambda i, g: (layer, g, 0, 0)),
            pl.BlockSpec((None, 1, POOL_GROUP), lambda i, g: (layer, 0, g)),
        ],
        out_specs=pl.BlockSpec((None, s, POOL_GROUP), lambda i, g: (i, 0, g)),
        scratch_shapes=[pltpu.VMEM((POOL_PAD + s, POOL_GROUP), F32),
                        pltpu.VMEM((POOL_PAD + s, POOL_GROUP), F32)],
        compiler_params=_params("parallel", "arbitrary"),
        name="pool_mixer",
    )(z3, w_pool, pool_scale)


def _pool_dec_kernel(z_ref, buf_ref, w_ref, sc_ref, po_ref, bo_ref, pooled_ref, *, nb):
    row = lax.broadcasted_iota(jnp.int32, (POOL_BUF, POOL_GROUP), 0)
    for b in range(nb):
        u = z_ref[b:b + 1, COL_PU:COL_PU + POOL_WIDTH]
        old = buf_ref[b]
        for gi, win in enumerate(POOL_WINDOWS):
            cols = slice(gi * POOL_GROUP, (gi + 1) * POOL_GROUP)
            tail = jnp.where(row >= POOL_BUF - (win - 1), old[:, cols], 0.0)
            total = jnp.sum(tail, axis=0, keepdims=True) + u[:, cols]
            pooled_ref[b:b + 1, cols] = total / float(win) - u[:, cols]
        bo_ref[b, 0:POOL_BUF - 1, :] = old[1:POOL_BUF, :]
        bo_ref[b, POOL_BUF - 1:POOL_BUF, :] = u
    for gi in range(len(POOL_WINDOWS)):
        cols = slice(gi * POOL_GROUP, (gi + 1) * POOL_GROUP)
        y = _dot(pooled_ref[:, cols].astype(BF16), w_ref[gi].astype(BF16))
        po_ref[:, cols] = y * sc_ref[:, cols]


def _pool_decode(z, cache_pool, w_pool, pool_scale, layer):
    nb = z.shape[0]
    ng = len(POOL_WINDOWS)
    bshape = (nb, POOL_BUF, POOL_WIDTH)
    return pl.pallas_call(
        functools.partial(_pool_dec_kernel, nb=nb),
        out_shape=(jax.ShapeDtypeStruct((nb, POOL_WIDTH), F32),
                   jax.ShapeDtypeStruct(bshape, F32)),
        grid=(1,),
        in_specs=[
            pl.BlockSpec(z.shape, lambda i: (0, 0)),
            pl.BlockSpec((None,) + bshape, lambda i: (layer, 0, 0, 0)),
            pl.BlockSpec((None, ng, POOL_GROUP, POOL_GROUP), lambda i: (layer, 0, 0, 0)),
            pl.BlockSpec((None, 1, POOL_WIDTH), lambda i: (layer, 0, 0)),
        ],
        out_specs=(pl.BlockSpec((nb, POOL_WIDTH), lambda i: (0, 0)),
                   pl.BlockSpec(bshape, lambda i: (0, 0, 0))),
        scratch_shapes=[pltpu.VMEM((nb, POOL_WIDTH), F32)],
        compiler_params=_params("arbitrary"),
        name="pool_mixer_decode",
    )(z, cache_pool, w_pool, pool_scale)


def _trunk_prompt(x, mod, p, wb, final_g):
    b, s, d = x.shape
    depth = p["w_in"].shape[0]
    pos = jnp.arange(s, dtype=F32)
    ret_tabs = _ret_rope_tables(pos)
    att_tabs = _att_rope_tables(pos)
    keep = tuple(min(win, s) for win, _ in ATT_PATTERNS)
    tm = math.gcd(s, 1024)
    xf = x.reshape(b * s, d)
    rets, pools = [], []
    ks = [[] for _ in ATT_PATTERNS]
    vs = [[] for _ in ATT_PATTERNS]
    for l in range(depth):
        xf = _ffn(xf, mod, l, 0, s, p["norm_g"], *wb[l]["ffn1"], None, tm, 512)
        z = _inproj(xf, mod, l, s, p["norm_g"], wb[l]["w_in"], math.gcd(s, 256), IN_WIDTH,
                    rope_tabs=ret_tabs + att_tabs)
        z3 = z.reshape(b, s, IN_WIDTH)
        ro, ret_s = _retention_prompt(z3, p["ret_norm_g"], l)
        ao = _attention_prompt(z3)
        po = _pool_prompt(z3, p["w_pool"], p["pool_scale"], l)
        xf = _outproj(xf, mod, l, s, ro.reshape(b * s, -1), ao.reshape(b * s, -1),
                      po.reshape(b * s, -1), p["w_out"], math.gcd(s, 512))
        xf = _ffn(xf, mod, l, 2, s, p["norm_g"], *wb[l]["ffn2"],
                  final_g if l == depth - 1 else None, tm, 512)
        rets.append(ret_s)
        pools.append(z3[:, s - POOL_BUF:, COL_PU:])
        for g in range(N_PAT):
            for col, dst in ((COL_AK, ks), (COL_AV, vs)):
                c0 = col + g * ATT_GROUP
                dst[g].append(z3[:, s - keep[g]:, c0:c0 + ATT_GROUP]
                              .reshape(b, keep[g], ATT_HEADS, ATT_DIM))
    y = xf.reshape(b, s, d)
    return (y, jnp.stack(rets), [jnp.stack(k) for k in ks], [jnp.stack(v) for v in vs],
            jnp.stack(pools))


def _trunk_decode(x, mod, pos0, caches, p, final_g):
    nb, s, d = x.shape
    assert s == 1, "decode trunk handles one new token per batch row"
    depth = p["w_in"].shape[0]
    state_ret, cks, cvs, cpool = caches
    pos = pos0 + jnp.arange(s, dtype=F32)
    ret_tabs = _ret_rope_tables(pos)
    att_tabs = _att_rope_tables(pos)
    cks = [c.reshape(c.shape[0], nb, c.shape[2], ATT_GROUP) for c in cks]
    cvs = [c.reshape(c.shape[0], nb, c.shape[2], ATT_GROUP) for c in cvs]
    xf = x.reshape(nb, d)
    rets, pools, wb = [], [], []
    ks = [[] for _ in ATT_PATTERNS]
    vs = [[] for _ in ATT_PATTERNS]
    for l in range(depth):
        xf, *ffn1_bf16 = _ffn(xf, mod, l, 0, 1, p["norm_g"], p["w1_gate"], p["w1_up"], p["w1_down"],
                              None, nb, 512, emit_bf16=True)
        z, w_in_bf16 = _inproj(xf, mod, l, 1, p["norm_g"], p["w_in"], nb, 256, emit_bf16=True)
        ro, ret_s = _retention_decode(z, state_ret, *ret_tabs, p["ret_norm_g"], l)
        att = _attention_decode(z, att_tabs, cks, cvs, l)
        ao, k_new, v_new = att[0].reshape(nb, ATT_GROUP), att[1:1 + N_PAT], att[1 + N_PAT:]
        po, pool_new = _pool_decode(z, cpool, p["w_pool"], p["pool_scale"], l)
        xf = _outproj(xf, mod, l, 1, ro, ao, po, p["w_out"], nb)
        xf, *ffn2_bf16 = _ffn(xf, mod, l, 2, 1, p["norm_g"], p["w2_gate"], p["w2_up"], p["w2_down"],
                              final_g if l == depth - 1 else None, nb, 512, emit_bf16=True)
        wb.append({"ffn1": ffn1_bf16, "w_in": w_in_bf16, "ffn2": ffn2_bf16})
        rets.append(ret_s)
        pools.append(pool_new)
        for g in range(N_PAT):
            ks[g].append(k_new[g].reshape(nb, -1, ATT_HEADS, ATT_DIM))
            vs[g].append(v_new[g].reshape(nb, -1, ATT_HEADS, ATT_DIM))
    y = xf.reshape(nb, s, d)
    return (y, jnp.stack(rets), [jnp.stack(k) for k in ks], [jnp.stack(v) for v in vs],
            jnp.stack(pools), wb)


def kernel(x_prompt, x_sample, state_ret, cache_k_w128, cache_v_w128, cache_k_w512, cache_v_w512,
           cache_k_w2048, cache_v_w2048, cache_pool, c_prompt, c_sample, w_ada, b_ada, norm_g, w_in,
           ret_norm_g, w_pool, pool_scale, w_out, w1_gate, w1_up, w1_down, w2_gate, w2_up, w2_down,
           final_norm_g):
    depth, d = norm_g.shape[0], norm_g.shape[-1]
    n_pr, n_dec = c_prompt.shape[0], c_sample.shape[0]
    p = {
        "norm_g": norm_g.reshape(depth, N_SUB, 1, d),
        "w_in": w_in,
        "ret_norm_g": ret_norm_g.reshape(depth, 1, RET_WIDTH),
        "w_pool": w_pool,
        "pool_scale": pool_scale.reshape(depth, 1, POOL_WIDTH),
        "w_out": w_out,
        "w1_gate": w1_gate, "w1_up": w1_up, "w1_down": w1_down,
        "w2_gate": w2_gate, "w2_up": w2_up, "w2_down": w2_down,
    }
    final_g = final_norm_g.reshape(1, d)

    pad = (-(n_dec + n_pr)) % SUBLANES
    c_all = jnp.concatenate([c_sample, c_prompt, jnp.zeros((pad, d), F32)], axis=0)
    mod_dec, mod_pr = _ada(c_all, n_dec, n_pr, w_ada, b_ada)

    caches = (state_ret,
              (cache_k_w128, cache_k_w512, cache_k_w2048),
              (cache_v_w128, cache_v_w512, cache_v_w2048),
              cache_pool)
    y_s, ret_s, ks, vs, pool_s, wb = _trunk_decode(x_sample, mod_dec, float(PAST_LEN), caches, p, final_g)
    y_p, ret_p, kp, vp, pool_p = _trunk_prompt(x_prompt, mod_pr, p, wb, final_g)
    return (y_p, y_s, ret_p, ret_s,
            kp[0], ks[0], vp[0], vs[0],
            kp[1], ks[1], vp[1], vs[1],
            kp[2], ks[2], vp[2], vs[2],
            pool_p, pool_s)
```

```python
import functools
import math

import jax
import jax.numpy as jnp
from jax import lax
from jax.experimental import pallas as pl
from jax.experimental.pallas import tpu as pltpu

F32 = jnp.float32
BF16 = jnp.bfloat16

RET_HEADS = 6
RET_DIM = 128
RET_WIDTH = RET_HEADS * RET_DIM
RET_CHUNK = 128
RET_THETA = 10000.0
ATT_HEADS = 4
ATT_DIM = 64
ATT_GROUP = ATT_HEADS * ATT_DIM
ATT_PATTERNS = ((128, 1), (512, 4), (2048, 16))
N_PAT = len(ATT_PATTERNS)
ATT_WIDTH = N_PAT * ATT_GROUP
ROPE_THETA = 500000.0
ROPE_DIMS = ATT_DIM // 4
ROPE_HALF = ROPE_DIMS // 2
QUERY_BLOCK = 128
POOL_WINDOWS = (2, 4, 8, 16)
POOL_GROUP = 128
POOL_WIDTH = len(POOL_WINDOWS) * POOL_GROUP
POOL_BUF = max(POOL_WINDOWS) - 1
POOL_PAD = 16
N_SUB = 3
PAST_LEN = 16384
HALF_STEP = 0.5
EPS = 1e-6
MASK_VALUE = -1e30

COL_RQ, COL_RK, COL_RV, COL_RG = 0, RET_WIDTH, 2 * RET_WIDTH, 3 * RET_WIDTH
COL_AQ = 4 * RET_WIDTH
COL_AK = COL_AQ + ATT_WIDTH
COL_AV = COL_AK + ATT_WIDTH
COL_PU = COL_AV + ATT_WIDTH
IN_WIDTH = COL_PU + POOL_WIDTH

V7X_VMEM_BYTES = 64 * 1024 * 1024
VMEM_LIMIT = V7X_VMEM_BYTES - 8 * 1024 * 1024
SUBLANES = 8

LOG_GAMMA = tuple(math.log1p(-(2.0 ** (-5.0 - h))) for h in range(RET_HEADS))


def _params(*sem):
    return pltpu.CompilerParams(dimension_semantics=sem, vmem_limit_bytes=VMEM_LIMIT)


def _dot(a, b):
    return jnp.dot(a, b, preferred_element_type=F32)


def _dot_nt(a, b):
    return lax.dot_general(a, b, (((1,), (1,)), ((), ())), preferred_element_type=F32)


def _bf16_round(x):
    return x.astype(BF16).astype(F32)


def _ada_kernel(c_ref, w_ref, b_ref, od_ref, op_ref, *, n_dec, n_pr):
    c = c_ref[...]
    a = (c * jax.nn.sigmoid(c)).astype(BF16)
    res = _dot(a, w_ref[...].astype(BF16)) + b_ref[...]
    od_ref[...] = res[0:n_dec]
    for b in range(n_pr):
        op_ref[b] = res[n_dec + b:n_dec + b + 1]


def _ada(c_all, n_dec, n_pr, w_ada, b_ada, tn=1024):
    depth, d, n = w_ada.shape
    rows = c_all.shape[0]
    per = d // tn
    return pl.pallas_call(
        functools.partial(_ada_kernel, n_dec=n_dec, n_pr=n_pr),
        out_shape=(jax.ShapeDtypeStruct((depth, N_SUB * 3, n_dec, d), F32),
                   jax.ShapeDtypeStruct((depth, N_SUB * 3, n_pr, 1, d), F32)),
        grid=(depth, n // tn),
        in_specs=[
            pl.BlockSpec((rows, d), lambda l, j: (0, 0)),
            pl.BlockSpec((None, d, tn), lambda l, j: (l, 0, j)),
            pl.BlockSpec((None, 1, tn), lambda l, j: (l, 0, j)),
        ],
        out_specs=(
            pl.BlockSpec((None, None, n_dec, tn), lambda l, j: (l, j // per, 0, j % per)),
            pl.BlockSpec((None, None, n_pr, 1, tn), lambda l, j: (l, j // per, 0, 0, j % per)),
        ),
        compiler_params=_params("arbitrary", "arbitrary"),
        name="ada_mod",
    )(c_all, w_ada, b_ada.reshape(depth, 1, n))


def _rmsnorm(x, g):
    ms = jnp.mean(x * x, axis=-1, keepdims=True)
    return x * lax.rsqrt(ms + EPS) * g


def _row_chunks(tm):
    rc = min(tm, 128)
    return rc, tm // rc


def _mod_rows(ref, r, rc):
    return ref[...] if ref.shape[0] == 1 else ref[pl.ds(r, rc), :]


def _prenorm_to(h_ref, x_ref, g_ref, sh_ref, sc_ref, unrolled=False):
    rc, n = _row_chunks(x_ref.shape[0])

    def body(i, carry):
        r = i * rc if unrolled else pl.multiple_of(i * rc, rc)
        y = _rmsnorm(x_ref[pl.ds(r, rc), :], g_ref[...])
        h = y * (1.0 + _mod_rows(sc_ref, r, rc)) + _mod_rows(sh_ref, r, rc)
        h_ref[pl.ds(r, rc), :] = h.astype(BF16)
        return carry

    if unrolled:
        for i in range(n):
            body(i, 0)
    else:
        lax.fori_loop(0, n, body, 0)


def _mod_specs(mod, layer, sub, rows_per_batch, tm, grid_rank):
    d = mod.shape[-1]
    specs = []
    for k in range(3):
        j = sub * 3 + k
        if mod.ndim == 5:
            if grid_rank == 2:
                idx = (lambda j: lambda i, f: (layer, j, (i * tm) // rows_per_batch, 0, 0))(j)
            else:
                idx = (lambda j: lambda i: (layer, j, (i * tm) // rows_per_batch, 0, 0))(j)
            specs.append(pl.BlockSpec((None, None, None, 1, d), idx))
        else:
            if grid_rank == 2:
                idx = (lambda j: lambda i, f: (layer, j, 0, 0))(j)
            else:
                idx = (lambda j: lambda i: (layer, j, 0, 0))(j)
            specs.append(pl.BlockSpec((None, None, tm, d), idx))
    return specs


def _ffn_kernel(x_ref, sh_ref, sc_ref, gt_ref, g_ref, wg_ref, wu_ref, wd_ref, *rest,
                n_f, final_norm, emit_bf16):
    rest = list(rest)
    fg_ref = rest.pop(0) if final_norm else None
    o_ref = rest.pop(0)
    wb_refs = [rest.pop(0) for _ in range(3)] if emit_bf16 else None
    h_ref, = rest
    f = pl.program_id(1)

    @pl.when(f == 0)
    def _():
        _prenorm_to(h_ref, x_ref, g_ref, sh_ref, sc_ref)
        o_ref[...] = jnp.zeros_like(o_ref)

    wg = wg_ref[...].astype(BF16)
    wu = wu_ref[...].astype(BF16)
    wd = wd_ref[...].astype(BF16)
    if emit_bf16:
        wb_refs[0][...] = wg
        wb_refs[1][...] = wu
        wb_refs[2][...] = wd
    h = h_ref[...]
    gate = _dot(h, wg)
    up = _dot(h, wu)
    act = (gate * jax.nn.sigmoid(gate) * up).astype(BF16)
    d = o_ref.shape[1]
    dc = min(d, 512)
    for c0 in range(0, d, dc):
        o_ref[:, c0:c0 + dc] += _dot(act, wd[:, c0:c0 + dc])

    @pl.when(f == n_f - 1)
    def _():
        rc, n = _row_chunks(x_ref.shape[0])

        def body(i, carry):
            r = pl.multiple_of(i * rc, rc)
            rows = pl.ds(r, rc)
            out = x_ref[rows, :] + HALF_STEP * _mod_rows(gt_ref, r, rc) * o_ref[rows, :]
            if final_norm:
                out = _rmsnorm(out, fg_ref[...])
            o_ref[rows, :] = out
            return carry

        lax.fori_loop(0, n, body, 0)


def _weight_spec(w, layer, block, index, resident=False):
    mode = {"pipeline_mode": pl.Buffered(1)} if resident else {}
    if w.ndim == 3:
        return pl.BlockSpec((None,) + block, lambda i, j: (layer,) + index(i, j), **mode)
    return pl.BlockSpec(block, index, **mode)


def _ffn(x, mod, layer, sub, rows_per_batch, norm_g, wg, wu, wd, final_g, tm, tf, emit_bf16=False):
    m, d = x.shape
    d_ff = wg.shape[-1]
    assert m % tm == 0 and d_ff % tf == 0
    assert not emit_bf16 or m == tm, "each weight block must be visited exactly once"
    n_f = d_ff // tf
    final_norm = final_g is not None
    col_block = lambda i, j: (0, j)
    row_block = lambda i, j: (j, 0)
    in_specs = [pl.BlockSpec((tm, d), lambda i, j: (i, 0))]
    in_specs += _mod_specs(mod, layer, sub, rows_per_batch, tm, 2)
    in_specs += [
        pl.BlockSpec((None, None, 1, d), lambda i, j: (layer, sub, 0, 0)),
        _weight_spec(wg, layer, (d, tf), col_block),
        _weight_spec(wu, layer, (d, tf), col_block),
        _weight_spec(wd, layer, (tf, d), row_block),
    ]
    args = [x, mod, mod, mod, norm_g, wg, wu, wd]
    if final_norm:
        in_specs.append(pl.BlockSpec((1, d), lambda i, j: (0, 0)))
        args.append(final_g)
    out_shape = [jax.ShapeDtypeStruct((m, d), F32)]
    out_specs = [pl.BlockSpec((tm, d), lambda i, j: (i, 0))]
    if emit_bf16:
        out_shape += [jax.ShapeDtypeStruct((d, d_ff), BF16), jax.ShapeDtypeStruct((d, d_ff), BF16),
                      jax.ShapeDtypeStruct((d_ff, d), BF16)]
        out_specs += [pl.BlockSpec((d, tf), col_block), pl.BlockSpec((d, tf), col_block),
                      pl.BlockSpec((tf, d), row_block)]
    out = pl.pallas_call(
        functools.partial(_ffn_kernel, n_f=n_f, final_norm=final_norm, emit_bf16=emit_bf16),
        out_shape=tuple(out_shape),
        grid=(m // tm, n_f),
        in_specs=in_specs,
        out_specs=tuple(out_specs),
        scratch_shapes=[pltpu.VMEM((tm, d), BF16)],
        compiler_params=_params("parallel", "arbitrary"),
        name="ffn",
    )(*args)
    return out if emit_bf16 else out[0]


def _rotate_projection_block(y, col, ret_tabs, att_tabs):
    if col < COL_RV:
        y = _ret_rope(y, *ret_tabs)
        return y * (RET_DIM ** -0.5) if col >= COL_RK else y
    if COL_AQ <= col < COL_AV:
        y = _att_rope(y, *att_tabs)
        return y * (ATT_DIM ** -0.5) if col < COL_AK else y
    return y


def _inproj_kernel(x_ref, sh_ref, sc_ref, g_ref, w_ref, *rest, emit_bf16, single_col_tile, rotate):
    rest = list(rest)
    tab_refs = [rest.pop(0) for _ in range(5)] if rotate else None
    o_ref = rest.pop(0)
    wb_ref = rest.pop(0) if emit_bf16 else None
    h_ref, = rest

    if single_col_tile:
        _prenorm_to(h_ref, x_ref, g_ref, sh_ref, sc_ref, unrolled=True)
    else:
        @pl.when(pl.program_id(1) == 0)
        def _():
            _prenorm_to(h_ref, x_ref, g_ref, sh_ref, sc_ref)

    if emit_bf16:
        wb_ref[...] = w_ref[...].astype(BF16)
    h = h_ref[...]
    tn = o_ref.shape[1]
    nc = min(tn, 512)
    for c0 in range(0, tn, nc):
        c1 = min(c0 + nc, tn)
        y = _dot(h, w_ref[:, c0:c1].astype(BF16))
        if rotate:
            ret_tabs = (tab_refs[0][...], tab_refs[1][...])
            att_tabs = (tab_refs[2][...], tab_refs[3][...], tab_refs[4][...])
            for b0 in range(0, c1 - c0, 128):
                o_ref[:, c0 + b0:c0 + b0 + 128] = _rotate_projection_block(
                    y[:, b0:b0 + 128], c0 + b0, ret_tabs, att_tabs)
        else:
            o_ref[:, c0:c1] = y


def _inproj(x, mod, layer, rows_per_batch, norm_g, w_in, tm, tn, emit_bf16=False, rope_tabs=None):
    m, d = x.shape
    n = w_in.shape[-1]
    rotate = rope_tabs is not None
    assert m % tm == 0 and n % tn == 0
    assert not emit_bf16 or m == tm, "each weight block must be visited exactly once"
    assert not rotate or (n == tn and rows_per_batch % tm == 0)
    sh, sc, _ = _mod_specs(mod, layer, 1, rows_per_batch, tm, 2)
    col_block = lambda i, j: (0, j)
    in_specs = [
        pl.BlockSpec((tm, d), lambda i, j: (i, 0)),
        sh, sc,
        pl.BlockSpec((None, None, 1, d), lambda i, j: (layer, 1, 0, 0)),
        _weight_spec(w_in, layer, (d, tn), col_block, resident=(n == tn)),
    ]
    args = [x, mod, mod, norm_g, w_in]
    if rotate:
        tiles_per_batch = rows_per_batch // tm
        in_specs += [pl.BlockSpec((tm, 128), lambda i, j: (i % tiles_per_batch, 0))] * len(rope_tabs)
        args += list(rope_tabs)
    out_shape = [jax.ShapeDtypeStruct((m, n), F32)]
    out_specs = [pl.BlockSpec((tm, tn), lambda i, j: (i, j))]
    if emit_bf16:
        out_shape.append(jax.ShapeDtypeStruct((d, n), BF16))
        out_specs.append(pl.BlockSpec((d, tn), col_block))
    out = pl.pallas_call(
        functools.partial(_inproj_kernel, emit_bf16=emit_bf16, single_col_tile=(n == tn),
                          rotate=rotate),
        out_shape=tuple(out_shape),
        grid=(m // tm, n // tn),
        in_specs=in_specs,
        out_specs=tuple(out_specs),
        scratch_shapes=[pltpu.VMEM((tm, d), BF16)],
        compiler_params=_params("parallel", "arbitrary"),
        name="in_proj",
    )(*args)
    return out if emit_bf16 else out[0]


def _inproj_rows_kernel(x_ref, sh_ref, sc_ref, g_ref, w_ref, o_ref, wb_ref, h_ref, *, tk):
    k = pl.program_id(0)

    @pl.when(k == 0)
    def _():
        h = _rmsnorm(x_ref[...], g_ref[...]) * (1.0 + sc_ref[...]) + sh_ref[...]
        for c in range(h_ref.shape[0]):
            h_ref[c] = h[:, c * tk:(c + 1) * tk].astype(BF16)
        o_ref[...] = jnp.zeros_like(o_ref)

    wb = w_ref[...].astype(BF16)
    wb_ref[...] = wb
    o_ref[...] += _dot(h_ref[k], wb)


def _inproj_rows(x, mod, layer, norm_g, w_in, tk=512):
    m, d = x.shape
    n = w_in.shape[-1]
    assert d % tk == 0
    sh, sc, _ = _mod_specs(mod, layer, 1, 1, m, 1)
    return pl.pallas_call(
        functools.partial(_inproj_rows_kernel, tk=tk),
        out_shape=(jax.ShapeDtypeStruct((m, n), F32), jax.ShapeDtypeStruct((d, n), BF16)),
        grid=(d // tk,),
        in_specs=[
            pl.BlockSpec((m, d), lambda k: (0, 0)),
            sh, sc,
            pl.BlockSpec((None, None, 1, d), lambda k: (layer, 1, 0, 0)),
            pl.BlockSpec((None, tk, n), lambda k: (layer, k, 0)),
        ],
        out_specs=(pl.BlockSpec((m, n), lambda k: (0, 0)), pl.BlockSpec((tk, n), lambda k: (k, 0))),
        scratch_shapes=[pltpu.VMEM((d // tk, m, tk), BF16)],
        compiler_params=_params("arbitrary"),
        name="in_proj_rows",
    )(x, mod, mod, norm_g, w_in)


def _outproj_kernel(x_ref, gt_ref, ro_ref, ao_ref, po_ref, w_ref, o_ref, wb_ref):
    @pl.when(pl.program_id(0) == 0)
    def _():
        wb_ref[...] = w_ref[...].astype(BF16)

    r0, r1 = RET_WIDTH, RET_WIDTH + ATT_GROUP
    y = _dot(ro_ref[...].astype(BF16), wb_ref[0:r0, :])
    y += _dot(ao_ref[...].astype(BF16), wb_ref[r0:r1, :])
    y += _dot(po_ref[...].astype(BF16), wb_ref[r1:, :])
    o_ref[...] = x_ref[...] + gt_ref[...] * y


def _outproj(x, mod, layer, rows_per_batch, ro, ao, po, w_out, tm):
    m, d = x.shape
    k = w_out.shape[1]
    _, _, gt = _mod_specs(mod, layer, 1, rows_per_batch, tm, 1)
    return pl.pallas_call(
        _outproj_kernel,
        out_shape=jax.ShapeDtypeStruct((m, d), F32),
        grid=(m // tm,),
        in_specs=[
            pl.BlockSpec((tm, d), lambda i: (i, 0)),
            gt,
            pl.BlockSpec((tm, RET_WIDTH), lambda i: (i, 0)),
            pl.BlockSpec((tm, ATT_GROUP), lambda i: (i, 0)),
            pl.BlockSpec((tm, POOL_WIDTH), lambda i: (i, 0)),
            pl.BlockSpec((None, k, d), lambda i: (layer, 0, 0), pipeline_mode=pl.Buffered(1)),
        ],
        out_specs=pl.BlockSpec((tm, d), lambda i: (i, 0)),
        scratch_shapes=[pltpu.VMEM((k, d), BF16)],
        compiler_params=_params("arbitrary"),
        name="out_proj",
    )(x, mod, ro, ao, po, w_out)


def _ret_rope_tables(pos):
    half = RET_DIM // 2
    freq = jnp.power(jnp.float32(RET_THETA), -jnp.arange(half, dtype=F32) / half)
    ang = pos[:, None] * freq[None, :]
    cos, sin = jnp.cos(ang), jnp.sin(ang)
    return jnp.concatenate([cos, cos], axis=-1), jnp.concatenate([-sin, sin], axis=-1)


def _att_rope_tables(pos):
    freq = jnp.power(jnp.float32(ROPE_THETA), -jnp.arange(ROPE_HALF, dtype=F32) / ROPE_HALF)
    ang = pos[:, None] * freq[None, :]
    cos, sin = jnp.cos(ang), jnp.sin(ang)
    s = pos.shape[0]
    rest = ATT_DIM - ROPE_DIMS
    c = jnp.concatenate([cos, cos, jnp.ones((s, rest), F32)], axis=-1)
    a = jnp.concatenate([-sin, jnp.zeros((s, ATT_DIM - ROPE_HALF), F32)], axis=-1)
    b = jnp.concatenate([jnp.zeros((s, ROPE_HALF), F32), sin, jnp.zeros((s, rest), F32)], axis=-1)
    tile = lambda t: jnp.tile(t, (1, 128 // ATT_DIM))
    return tile(c), tile(a), tile(b)


def _ret_rope(x, cos_t, sin_t):
    return x * cos_t + pltpu.roll(x, RET_DIM // 2, axis=1) * sin_t


def _att_rope(x, c, a, b):
    n = x.shape[-1]
    return x * c + pltpu.roll(x, n - ROPE_HALF, axis=1) * a + pltpu.roll(x, ROPE_HALF, axis=1) * b


def _head_norm(o, g):
    mu = jnp.mean(o, axis=-1, keepdims=True)
    oc = o - mu
    var = jnp.mean(oc * oc, axis=-1, keepdims=True)
    return oc * lax.rsqrt(var + EPS) * g


def _ret_kernel(zq_ref, zk_ref, zv_ref, zg_ref, gn_ref,
                ro_ref, so_ref, din_ref, dq_ref, dk_ref, *, chunk, per_step):
    c = pl.program_id(1)

    @pl.when(c == 0)
    def _():
        so_ref[...] = jnp.zeros_like(so_ref)
        row = lax.broadcasted_iota(jnp.int32, (chunk, chunk), 0).astype(F32)
        col = lax.broadcasted_iota(jnp.int32, (chunk, chunk), 1).astype(F32)
        diff = row - col
        rowd = lax.broadcasted_iota(jnp.int32, (chunk, RET_DIM), 0).astype(F32)
        for h in range(RET_HEADS):
            lg = LOG_GAMMA[h]
            din_ref[h] = jnp.where(diff >= 0, jnp.exp(jnp.maximum(diff, 0.0) * lg), 0.0)
            dq_ref[h] = jnp.exp((rowd + 1.0) * lg)
            dk_ref[h] = jnp.exp((chunk - 1.0 - rowd) * lg)

    for h in range(RET_HEADS):
        cols = slice(h * RET_DIM, (h + 1) * RET_DIM)
        s_cur = so_ref[h]
        for j in range(per_step):
            rows = slice(j * chunk, (j + 1) * chunk)
            k = zk_ref[rows, cols]
            qb = zq_ref[rows, cols].astype(BF16)
            kb = k.astype(BF16)
            vb = zv_ref[rows, cols].astype(BF16)
            a = _dot_nt(qb, kb) * din_ref[h]
            o = _dot(a.astype(BF16), vb) + _dot(qb, s_cur.astype(BF16)) * dq_ref[h]
            kd_t = (k * dk_ref[h]).T.astype(BF16)
            s_cur = math.exp(chunk * LOG_GAMMA[h]) * s_cur + _dot(kd_t, vb)

            on = _head_norm(o, gn_ref[:, cols])
            g = zg_ref[rows, cols]
            ro_ref[rows, cols] = (g * jax.nn.sigmoid(g) * on).astype(BF16)
        so_ref[h] = s_cur


def _retention_prompt(z3, ret_norm_g, layer, per_step=4):
    b, s, _ = z3.shape
    chunk = math.gcd(s, RET_CHUNK)
    per_step = math.gcd(s // chunk, per_step)
    rows = chunk * per_step
    zspec = lambda cb: pl.BlockSpec((None, rows, RET_WIDTH), lambda i, c: (i, c, cb))
    return pl.pallas_call(
        functools.partial(_ret_kernel, chunk=chunk, per_step=per_step),
        out_shape=(jax.ShapeDtypeStruct((b, s, RET_WIDTH), BF16),
                   jax.ShapeDtypeStruct((b, RET_HEADS, RET_DIM, RET_DIM), F32)),
        grid=(b, s // rows),
        in_specs=[
            zspec(COL_RQ // RET_WIDTH), zspec(COL_RK // RET_WIDTH),
            zspec(COL_RV // RET_WIDTH), zspec(COL_RG // RET_WIDTH),
            pl.BlockSpec((None, 1, RET_WIDTH), lambda i, c: (layer, 0, 0)),
        ],
        out_specs=(
            pl.BlockSpec((None, rows, RET_WIDTH), lambda i, c: (i, c, 0)),
            pl.BlockSpec((None, RET_HEADS, RET_DIM, RET_DIM), lambda i, c: (i, 0, 0, 0)),
        ),
        scratch_shapes=[
            pltpu.VMEM((RET_HEADS, chunk, chunk), F32),
            pltpu.VMEM((RET_HEADS, chunk, RET_DIM), F32),
            pltpu.VMEM((RET_HEADS, chunk, RET_DIM), F32),
        ],
        compiler_params=_params("parallel", "arbitrary"),
        name="retention",
    )(z3, z3, z3, z3, ret_norm_g)


def _ret_dec_kernel(z_ref, s0_ref, cos_ref, sin_ref, gn_ref, ro_ref, so_ref, *, nb):
    cos_t = cos_ref[...]
    sin_t = sin_ref[...]
    row = lax.broadcasted_iota(jnp.int32, (nb, RET_DIM), 0)
    for h in range(RET_HEADS):
        gamma = math.exp(LOG_GAMMA[h])
        q = _ret_rope(z_ref[:, COL_RQ + h * RET_DIM:COL_RQ + (h + 1) * RET_DIM], cos_t, sin_t)
        k = _ret_rope(z_ref[:, COL_RK + h * RET_DIM:COL_RK + (h + 1) * RET_DIM], cos_t, sin_t)
        k = k * (RET_DIM ** -0.5)
        v = z_ref[:, COL_RV + h * RET_DIM:COL_RV + (h + 1) * RET_DIM]
        qr, kr, vr = _bf16_round(q), _bf16_round(k), _bf16_round(v)
        qk = jnp.sum(qr * kr, axis=-1, keepdims=True)
        o = _bf16_round(qk) * vr
        cross = jnp.zeros((nb, RET_DIM), F32)
        for b in range(nb):
            s_old = s0_ref[b, h]
            res = _dot(qr.astype(BF16), s_old.astype(BF16))
            cross = cross + jnp.where(row == b, res, 0.0)
            k_col = jnp.broadcast_to(kr[b:b + 1, :], (RET_DIM, RET_DIM)).T
            so_ref[b, h] = gamma * s_old + k_col * vr[b:b + 1, :]
        o = o + cross * gamma
        cols = slice(h * RET_DIM, (h + 1) * RET_DIM)
        on = _head_norm(o, gn_ref[:, cols])
        g = z_ref[:, COL_RG + h * RET_DIM:COL_RG + (h + 1) * RET_DIM]
        ro_ref[:, cols] = g * jax.nn.sigmoid(g) * on


def _retention_decode(z, state, cos_t, sin_t, ret_norm_g, layer):
    nb = z.shape[0]
    sshape = (nb, RET_HEADS, RET_DIM, RET_DIM)
    return pl.pallas_call(
        functools.partial(_ret_dec_kernel, nb=nb),
        out_shape=(jax.ShapeDtypeStruct((nb, RET_WIDTH), F32),
                   jax.ShapeDtypeStruct(sshape, F32)),
        grid=(1,),
        in_specs=[
            pl.BlockSpec(z.shape, lambda i: (0, 0)),
            pl.BlockSpec((None,) + sshape, lambda i: (layer, 0, 0, 0, 0)),
            pl.BlockSpec((1, RET_DIM), lambda i: (0, 0)),
            pl.BlockSpec((1, RET_DIM), lambda i: (0, 0)),
            pl.BlockSpec((None, 1, RET_WIDTH), lambda i: (layer, 0, 0)),
        ],
        out_specs=(
            pl.BlockSpec((nb, RET_WIDTH), lambda i: (0, 0)),
            pl.BlockSpec(sshape, lambda i: (0, 0, 0, 0)),
        ),
        compiler_params=_params("arbitrary"),
        name="retention_decode",
    )(z, state, cos_t, sin_t, ret_norm_g)


ATT_HALF = 128
ATT_MAX_STRIDE = 4
N_HALF = ATT_GROUP // ATT_HALF
HEADS_PER_HALF = ATT_HALF // ATT_DIM


def _half_cols(hf):
    return slice(hf * ATT_HALF, (hf + 1) * ATT_HALF)


def _att_kernel(*refs, seq):
    zqkv = (refs[0:N_HALF], refs[N_HALF:2 * N_HALF], refs[2 * N_HALF:3 * N_HALF])
    ao_ref, o_ref, lse_ref, st_ref, ost_ref = refs[3 * N_HALF:]
    g = pl.program_id(1)
    qb = QUERY_BLOCK
    n_blocks = seq // qb

    tq = lax.broadcasted_iota(jnp.int32, (qb, qb), 0)
    tk = lax.broadcasted_iota(jnp.int32, (qb, qb), 1)
    cur_valid = tk <= tq
    prev_valid = tk >= tq
    lane = lax.broadcasted_iota(jnp.int32, (qb, ATT_HALF), 1)
    head_masks = [(lane // ATT_DIM) == hh for hh in range(HEADS_PER_HALF)]

    def group_body(gi, dil):
        has_prev = True
        d1 = min(dil, ATT_MAX_STRIDE)
        d2 = dil // d1
        staged = d2 > 1
        assert d2 <= ATT_MAX_STRIDE and d1 * d2 == dil
        sub_len = seq // d1

        def strided(start, n, stride):
            return pl.ds(start, n, stride=stride) if stride > 1 else pl.ds(start, n)

        if staged:
            for hf in range(N_HALF):
                for r1 in range(d1):
                    src = strided(r1, sub_len, d1)
                    for which in range(3):
                        st_ref[which, hf, r1] = zqkv[which][hf][src, :]

        def load(which, hf, cls, blk):
            if staged:
                rows = strided(cls // d1 + d2 * qb * blk, qb, d2)
                return st_ref[which, hf, cls % d1, rows, :].astype(BF16)
            rows = strided(cls + dil * qb * blk, qb, dil)
            return zqkv[which][hf][rows, :].astype(BF16)

        def store(which, hf, cls, blk, val):
            if staged:
                rows = strided(cls // d1 + d2 * qb * blk, qb, d2)
                ost_ref[which, hf, cls % d1, rows, :] = val
            else:
                rows = strided(cls + dil * qb * blk, qb, dil)
                (o_ref, lse_ref)[which][gi, hf, rows, :] = val

        def block_body(t, carry):
            cls = t % dil
            blk = t // dil
            valid = cur_valid
            if has_prev:
                pblk = jnp.maximum(blk - 1, 0)
                valid = jnp.concatenate([prev_valid & (blk > 0), cur_valid], axis=1)
            for hf in range(N_HALF):
                q = load(0, hf, cls, blk)
                keys = load(1, hf, cls, blk)
                vals = load(2, hf, cls, blk)
                if has_prev:
                    keys = jnp.concatenate([load(1, hf, cls, pblk), keys], axis=0)
                    vals = jnp.concatenate([load(2, hf, cls, pblk), vals], axis=0)
                vals_ones = jnp.concatenate([vals, jnp.ones_like(vals)], axis=1)
                num = jnp.zeros((qb, ATT_HALF), F32)
                den = jnp.ones((qb, ATT_HALF), F32)
                lse_acc = jnp.zeros((qb, ATT_HALF), F32)
                for hm in head_masks:
                    qh = jnp.where(hm, q, jnp.zeros_like(q))
                    s = jnp.where(valid, _dot_nt(qh, keys), MASK_VALUE)
                    m = jnp.max(s, axis=-1, keepdims=True)
                    e = jnp.exp(s - m).astype(BF16)
                    r = _dot(e, vals_ones)
                    l = r[:, ATT_HALF:]
                    num = jnp.where(hm, r[:, :ATT_HALF], num)
                    den = jnp.where(hm, l, den)
                    lse_acc = jnp.where(hm, m + jnp.log(l), lse_acc)
                store(0, hf, cls, blk, num / den)
                store(1, hf, cls, blk, lse_acc)
            return carry

        lax.fori_loop(0, n_blocks, block_body, 0, unroll=4)

        if staged:
            for hf in range(N_HALF):
                for r1 in range(d1):
                    dst = strided(r1, sub_len, d1)
                    o_ref[gi, hf, dst, :] = ost_ref[0, hf, r1]
                    lse_ref[gi, hf, dst, :] = ost_ref[1, hf, r1]

    for gi, (_, dil) in enumerate(ATT_PATTERNS):
        @pl.when(g == gi)
        def _(gi=gi, dil=dil):
            group_body(gi, dil)

    @pl.when(g == N_PAT - 1)
    def _():
        def merge_body(i, carry):
            r = pl.multiple_of(i * qb, qb)
            rows = pl.ds(r, qb)
            for hf in range(N_HALF):
                lses = [lse_ref[gi, hf, rows, :] for gi in range(N_PAT)]
                m = functools.reduce(jnp.maximum, lses)
                ws = [jnp.exp(l - m) for l in lses]
                den = functools.reduce(lambda x, y: x + y, ws)
                num = functools.reduce(lambda x, y: x + y,
                                       [w * o_ref[gi, hf, rows, :] for gi, w in enumerate(ws)])
                ao_ref[rows, _half_cols(hf)] = (num / den).astype(BF16)
            return carry

        lax.fori_loop(0, n_blocks, merge_body, 0)


def _attention_prompt(z3):
    b, s, _ = z3.shape
    assert all(s % (dil * QUERY_BLOCK) == 0 for _, dil in ATT_PATTERNS)
    zspec = lambda col, hf: pl.BlockSpec(
        (None, s, ATT_HALF), lambda i, g: (i, 0, col // ATT_HALF + N_HALF * g + hf))
    zspecs = [zspec(col, hf) for col in (COL_AQ, COL_AK, COL_AV) for hf in range(N_HALF)]
    return pl.pallas_call(
        functools.partial(_att_kernel, seq=s),
        out_shape=jax.ShapeDtypeStruct((b, s, ATT_GROUP), BF16),
        grid=(b, N_PAT),
        in_specs=zspecs,
        out_specs=pl.BlockSpec((None, s, ATT_GROUP), lambda i, g: (i, 0, 0)),
        scratch_shapes=[
            pltpu.VMEM((N_PAT, N_HALF, s, ATT_HALF), F32),
            pltpu.VMEM((N_PAT, N_HALF, s, ATT_HALF), F32),
            pltpu.VMEM((3, N_HALF, ATT_MAX_STRIDE, s // ATT_MAX_STRIDE, ATT_HALF), F32),
            pltpu.VMEM((2, N_HALF, ATT_MAX_STRIDE, s // ATT_MAX_STRIDE, ATT_HALF), F32),
        ],
        compiler_params=_params("parallel", "arbitrary"),
        name="dilated_attention",
    )(*([z3] * (3 * N_HALF)))


def _cache_shift_kernel(*refs):
    n = len(refs) // 2
    for src, dst in zip(refs[:n], refs[n:]):
        h, dim, w = src.shape
        dst[...] = pltpu.roll(src[...].reshape(h * dim, w), w - 1, axis=1).reshape(h, dim, w)


def _cache_shift(caches):
    depth, nb = caches[0].shape[:2]
    specs = [pl.BlockSpec((None, None) + c.shape[2:], lambda l, i: (l, i, 0, 0, 0)) for c in caches]
    return pl.pallas_call(
        _cache_shift_kernel,
        out_shape=tuple(jax.ShapeDtypeStruct(c.shape, c.dtype) for c in caches),
        grid=(depth, nb),
        in_specs=specs,
        out_specs=tuple(specs),
        compiler_params=_params("arbitrary", "arbitrary"),
        name="cache_shift",
    )(*caches)


def _row_to_col(row, eye):
    return jnp.sum(jnp.where(eye, row, 0.0), axis=1, keepdims=True)


def _col_to_row(col, eye):
    return jnp.sum(jnp.where(eye, col, 0.0), axis=0, keepdims=True)


def _att_dec_kernel(z_ref, c_ref, a_ref, b_ref, *refs):
    kc_refs = refs[0:N_PAT]
    vc_refs = refs[N_PAT:2 * N_PAT]
    kt_refs = refs[2 * N_PAT:3 * N_PAT]
    vt_refs = refs[3 * N_PAT:4 * N_PAT]
    ao_ref = refs[4 * N_PAT]
    kt_out_refs = refs[4 * N_PAT + 1:5 * N_PAT + 1]
    vt_out_refs = refs[5 * N_PAT + 1:6 * N_PAT + 1]
    rope =(c_ref[...], a_ref[...], b_ref[...])
    eye = (lax.broadcasted_iota(jnp.int32, (ATT_DIM, ATT_DIM), 0)
           == lax.broadcasted_iota(jnp.int32, (ATT_DIM, ATT_DIM), 1))
    last_lane = lax.broadcasted_iota(jnp.int32, (ATT_DIM, ATT_HALF), 1) == ATT_HALF - 1

    head_rows = []
    for h in range(ATT_HEADS):
        hf, lanes = h // HEADS_PER_HALF, slice((h % HEADS_PER_HALF) * ATT_DIM,
                                               (h % HEADS_PER_HALF + 1) * ATT_DIM)
        outs, lses = [], []
        for gi, (win, dil) in enumerate(ATT_PATTERNS):
            off = gi * ATT_GROUP + hf * ATT_HALF
            zrow = lambda col: z_ref[:, col + off:col + off + ATT_HALF]
            q = (_att_rope(zrow(COL_AQ), *rope) * (ATT_DIM ** -0.5))[:, lanes]
            k_new = _att_rope(zrow(COL_AK), *rope)[:, lanes]
            v_new = zrow(COL_AV)[:, lanes]
            k_col, v_col = _row_to_col(k_new, eye), _row_to_col(v_new, eye)
            keys, vals = kc_refs[gi][h], vc_refs[gi][h]
            w = keys.shape[1]
            pos = lax.broadcasted_iota(jnp.int32, (1, w), 1)
            s_old = jnp.sum(keys * _row_to_col(q, eye), axis=0, keepdims=True)
            s_old = jnp.where(pos % dil == 0, s_old, MASK_VALUE)
            s_new = jnp.sum(q * k_new, axis=1, keepdims=True)
            m = jnp.maximum(jnp.max(s_old, axis=1, keepdims=True), s_new)
            e_old = jnp.exp(s_old - m)
            e_new = jnp.exp(s_new - m)
            l = jnp.sum(e_old, axis=1, keepdims=True) + e_new
            pv = jnp.sum(vals * e_old, axis=1, keepdims=True) + e_new * v_col
            outs.append(pv / l)
            lses.append(m + jnp.log(l))
            kt_out_refs[gi][h] = jnp.where(last_lane, k_col, kt_refs[gi][h])
            vt_out_refs[gi][h] = jnp.where(last_lane, v_col, vt_refs[gi][h])

        m = functools.reduce(jnp.maximum, lses)
        ws = [jnp.exp(l - m) for l in lses]
        den = functools.reduce(lambda x, y: x + y, ws)
        num = functools.reduce(lambda x, y: x + y, [w * o for w, o in zip(ws, outs)])
        head_rows.append(_col_to_row(num / den, eye))
    ao_ref[...] = jnp.concatenate(head_rows, axis=1)


def _attention_decode(z, tabs, k_caches, v_caches, k_next, v_next, layer):
    nb = z.shape[0]
    n_fixed = 4
    cspec = lambda c: pl.BlockSpec((None, None) + c.shape[2:], lambda i: (layer, i, 0, 0, 0))
    tail = lambda c: pl.BlockSpec((None, None) + c.shape[2:4] + (ATT_HALF,),
                                  lambda i: (layer, i, 0, 0, c.shape[4] // ATT_HALF - 1))
    tspec = pl.BlockSpec((1, ATT_HALF), lambda i: (0, 0))
    caches = list(k_caches) + list(v_caches)
    nexts = list(k_next) + list(v_next)
    return pl.pallas_call(
        _att_dec_kernel,
        out_shape=tuple([jax.ShapeDtypeStruct((nb, 1, ATT_GROUP), F32)]
                        + [jax.ShapeDtypeStruct(c.shape, c.dtype) for c in nexts]),
        grid=(nb,),
        in_specs=[pl.BlockSpec((None, 1, z.shape[1]), lambda i: (i, 0, 0)), tspec, tspec, tspec]
                 + [cspec(c) for c in caches] + [tail(c) for c in nexts],
        out_specs=tuple([pl.BlockSpec((None, 1, ATT_GROUP), lambda i: (i, 0, 0))]
                        + [tail(c) for c in nexts]),
        input_output_aliases={n_fixed + len(caches) + j: 1 + j for j in range(len(nexts))},
        compiler_params=_params("arbitrary"),
        name="dilated_attention_decode",
    )(z.reshape(nb, 1, z.shape[1]), *tabs, *caches, *nexts)


def _pool_kernel(u_ref, w_ref, sc_ref, po_ref, a_ref, b_ref, *, seq):
    g = pl.program_id(1)
    body = pl.ds(POOL_PAD, seq)

    def window_mean_minus_token(win):
        x = u_ref[...]
        a_ref[0:POOL_PAD, :] = jnp.zeros((POOL_PAD, POOL_GROUP), F32)
        b_ref[0:POOL_PAD, :] = jnp.zeros((POOL_PAD, POOL_GROUP), F32)
        a_ref[body, :] = x
        src, dst = a_ref, b_ref
        k = 1
        while k < win:
            dst[body, :] = src[body, :] + src[pl.ds(POOL_PAD - k, seq), :]
            src, dst = dst, src
            k *= 2
        t = lax.broadcasted_iota(jnp.int32, (seq, POOL_GROUP), 0)
        cnt = jnp.minimum(t + 1, win).astype(F32)
        pooled = src[body, :] / cnt - x
        y = _dot(pooled.astype(BF16), w_ref[...].astype(BF16)) * sc_ref[...]
        po_ref[...] = y.astype(BF16)

    for gi, win in enumerate(POOL_WINDOWS):
        @pl.when(g == gi)
        def _(win=win):
            window_mean_minus_token(win)


def _pool_prompt(z3, w_pool, pool_scale, layer):
    b, s, _ = z3.shape
    ng = len(POOL_WINDOWS)
    return pl.pallas_call(
        functools.partial(_pool_kernel, seq=s),
        out_shape=jax.ShapeDtypeStruct((b, s, POOL_WIDTH), BF16),
        grid=(b, ng),
        in_specs=[
            pl.BlockSpec((None, s, POOL_GROUP), lambda i, g: (i, 0, COL_PU // POOL_GROUP + g)),
            pl.BlockSpec((None, None, POOL_GROUP, POOL_GROUP), lambda i, g: (layer, g, 0, 0)),
            pl.BlockSpec((None, 1, POOL_GROUP), lambda i, g: (layer, 0, g)),
        ],
        out_specs=pl.BlockSpec((None, s, POOL_GROUP), lambda i, g: (i, 0, g)),
        scratch_shapes=[pltpu.VMEM((POOL_PAD + s, POOL_GROUP), F32),
                        pltpu.VMEM((POOL_PAD + s, POOL_GROUP), F32)],
        compiler_params=_params("parallel", "arbitrary"),
        name="pool_mixer",
    )(z3, w_pool, pool_scale)


def _pool_dec_kernel(z_ref, buf_ref, w_ref, sc_ref, po_ref, bo_ref, pooled_ref, *, nb):
    row = lax.broadcasted_iota(jnp.int32, (POOL_BUF, POOL_GROUP), 0)
    for b in range(nb):
        u = z_ref[b:b + 1, COL_PU:COL_PU + POOL_WIDTH]
        old = buf_ref[b]
        for gi, win in enumerate(POOL_WINDOWS):
            cols = slice(gi * POOL_GROUP, (gi + 1) * POOL_GROUP)
            tail = jnp.where(row >= POOL_BUF - (win - 1), old[:, cols], 0.0)
            total = jnp.sum(tail, axis=0, keepdims=True) + u[:, cols]
            pooled_ref[b:b + 1, cols] = total / float(win) - u[:, cols]
        bo_ref[b, 0:POOL_BUF - 1, :] = old[1:POOL_BUF, :]
        bo_ref[b, POOL_BUF - 1:POOL_BUF, :] = u
    for gi in range(len(POOL_WINDOWS)):
        cols = slice(gi * POOL_GROUP, (gi + 1) * POOL_GROUP)
        y = _dot(pooled_ref[:, cols].astype(BF16), w_ref[gi].astype(BF16))
        po_ref[:, cols] = y * sc_ref[:, cols]


def _pool_decode(z, cache_pool, w_pool, pool_scale, layer):
    nb = z.shape[0]
    ng = len(POOL_WINDOWS)
    bshape = (nb, POOL_BUF, POOL_WIDTH)
    return pl.pallas_call(
        functools.partial(_pool_dec_kernel, nb=nb),
        out_shape=(jax.ShapeDtypeStruct((nb, POOL_WIDTH), F32),
                   jax.ShapeDtypeStruct(bshape, F32)),
        grid=(1,),
        in_specs=[
            pl.BlockSpec(z.shape, lambda i: (0, 0)),
            pl.BlockSpec((None,) + bshape, lambda i: (layer, 0, 0, 0)),
            pl.BlockSpec((None, ng, POOL_GROUP, POOL_GROUP), lambda i: (layer, 0, 0, 0)),
            pl.BlockSpec((None, 1, POOL_WIDTH), lambda i: (layer, 0, 0)),
        ],
        out_specs=(pl.BlockSpec((nb, POOL_WIDTH), lambda i: (0, 0)),
                   pl.BlockSpec(bshape, lambda i: (0, 0, 0))),
        scratch_shapes=[pltpu.VMEM((nb, POOL_WIDTH), F32)],
        compiler_params=_params("arbitrary"),
        name="pool_mixer_decode",
    )(z, cache_pool, w_pool, pool_scale)


def _trunk_prompt(x, mod, p, wb, final_g):
    b, s, d = x.shape
    depth = p["w_in"].shape[0]
    pos = jnp.arange(s, dtype=F32)
    ret_tabs = _ret_rope_tables(pos)
    att_tabs = _att_rope_tables(pos)
    keep = tuple(min(win, s) for win, _ in ATT_PATTERNS)
    tm = math.gcd(s, 1024)
    xf = x.reshape(b * s, d)
    rets, pools = [], []
    ks = [[] for _ in ATT_PATTERNS]
    vs = [[] for _ in ATT_PATTERNS]
    for l in range(depth):
        xf = _ffn(xf, mod, l, 0, s, p["norm_g"], *wb[l]["ffn1"], None, tm, 512)
        z = _inproj(xf, mod, l, s, p["norm_g"], wb[l]["w_in"], math.gcd(s, 256), IN_WIDTH,
                    rope_tabs=ret_tabs + att_tabs)
        z3 = z.reshape(b, s, IN_WIDTH)
        ro, ret_s = _retention_prompt(z3, p["ret_norm_g"], l)
        ao = _attention_prompt(z3)
        po = _pool_prompt(z3, p["w_pool"], p["pool_scale"], l)
        xf = _outproj(xf, mod, l, s, ro.reshape(b * s, -1), ao.reshape(b * s, -1),
                      po.reshape(b * s, -1), p["w_out"], math.gcd(s, 512))
        xf = _ffn(xf, mod, l, 2, s, p["norm_g"], *wb[l]["ffn2"],
                  final_g if l == depth - 1 else None, tm, 512)
        rets.append(ret_s)
        pools.append(z3[:, s - POOL_BUF:, COL_PU:])
        for g in range(N_PAT):
            for col, dst in ((COL_AK, ks), (COL_AV, vs)):
                c0 = col + g * ATT_GROUP
                dst[g].append(z3[:, s - keep[g]:, c0:c0 + ATT_GROUP]
                              .reshape(b, keep[g], ATT_HEADS, ATT_DIM))
    y = xf.reshape(b, s, d)
    return (y, jnp.stack(rets), [jnp.stack(k) for k in ks], [jnp.stack(v) for v in vs],
            jnp.stack(pools))


def _trunk_decode(x, mod, pos0, caches, p, final_g):
    nb, s, d = x.shape
    assert s == 1, "decode trunk handles one new token per batch row"
    depth = p["w_in"].shape[0]
    state_ret, cks, cvs, cpool = caches
    pos = pos0 + jnp.arange(s, dtype=F32)
    ret_tabs = _ret_rope_tables(pos)
    att_tabs = _att_rope_tables(pos)
    feature_major = lambda c: jnp.transpose(c, (0, 1, 3, 4, 2))
    position_major = lambda c: jnp.transpose(c, (0, 1, 4, 2, 3))
    cks = [feature_major(c) for c in cks]
    cvs = [feature_major(c) for c in cvs]
    advanced = _cache_shift(cks + cvs)
    k_next, v_next = list(advanced[:N_PAT]), list(advanced[N_PAT:])
    xf = x.reshape(nb, d)
    rets, pools, wb = [], [], []
    for l in range(depth):
        xf, *ffn1_bf16 = _ffn(xf, mod, l, 0, 1, p["norm_g"], p["w1_gate"], p["w1_up"], p["w1_down"],
                              None, nb, 512, emit_bf16=True)
        z, w_in_bf16 = _inproj_rows(xf, mod, l, p["norm_g"], p["w_in"])
        ro, ret_s = _retention_decode(z, state_ret, *ret_tabs, p["ret_norm_g"], l)
        att = _attention_decode(z, att_tabs, cks, cvs, k_next, v_next, l)
        ao, k_next, v_next = att[0].reshape(nb, ATT_GROUP), list(att[1:1 + N_PAT]), list(att[1 + N_PAT:])
        po, pool_new = _pool_decode(z, cpool, p["w_pool"], p["pool_scale"], l)
        xf = _outproj(xf, mod, l, 1, ro, ao, po, p["w_out"], nb)
        xf, *ffn2_bf16 = _ffn(xf, mod, l, 2, 1, p["norm_g"], p["w2_gate"], p["w2_up"], p["w2_down"],
                              final_g if l == depth - 1 else None, nb, 512, emit_bf16=True)
        wb.append({"ffn1": ffn1_bf16, "w_in": w_in_bf16, "ffn2": ffn2_bf16})
        rets.append(ret_s)
        pools.append(pool_new)
    y = xf.reshape(nb, s, d)
    return (y, jnp.stack(rets), [position_major(k) for k in k_next],
            [position_major(v) for v in v_next], jnp.stack(pools), wb)


def kernel(x_prompt, x_sample, state_ret, cache_k_w128, cache_v_w128, cache_k_w512, cache_v_w512,
           cache_k_w2048, cache_v_w2048, cache_pool, c_prompt, c_sample, w_ada, b_ada, norm_g, w_in,
           ret_norm_g, w_pool, pool_scale, w_out, w1_gate, w1_up, w1_down, w2_gate, w2_up, w2_down,
           final_norm_g):
    depth, d = norm_g.shape[0], norm_g.shape[-1]
    n_pr, n_dec = c_prompt.shape[0], c_sample.shape[0]
    p = {
        "norm_g": norm_g.reshape(depth, N_SUB, 1, d),
        "w_in": w_in,
        "ret_norm_g": ret_norm_g.reshape(depth, 1, RET_WIDTH),
        "w_pool": w_pool,
        "pool_scale": pool_scale.reshape(depth, 1, POOL_WIDTH),
        "w_out": w_out,
        "w1_gate": w1_gate, "w1_up": w1_up, "w1_down": w1_down,
        "w2_gate": w2_gate, "w2_up": w2_up, "w2_down": w2_down,
    }
    final_g = final_norm_g.reshape(1, d)

    pad = (-(n_dec + n_pr)) % SUBLANES
    c_all = jnp.concatenate([c_sample, c_prompt, jnp.zeros((pad, d), F32)], axis=0)
    mod_dec, mod_pr = _ada(c_all, n_dec, n_pr, w_ada, b_ada)

    caches = (state_ret,
              (cache_k_w128, cache_k_w512, cache_k_w2048),
              (cache_v_w128, cache_v_w512, cache_v_w2048),
              cache_pool)
    y_s, ret_s, ks, vs, pool_s, wb = _trunk_decode(x_sample, mod_dec, float(PAST_LEN), caches, p, final_g)
    y_p, ret_p, kp, vp, pool_p = _trunk_prompt(x_prompt, mod_pr, p, wb, final_g)
    return (y_p, y_s, ret_p, ret_s,
            kp[0], ks[0], vp[0], vs[0],
            kp[1], ks[1], vp[1], vs[1],
            kp[2], ks[2], vp[2], vs[2],
            pool_p, pool_s)
```

```python
import functools
import math

import jax
import jax.numpy as jnp
from jax import lax
from jax.experimental import pallas as pl
from jax.experimental.pallas import tpu as pltpu

F32 = jnp.float32
BF16 = jnp.bfloat16

RET_HEADS = 6
RET_DIM = 128
RET_WIDTH = RET_HEADS * RET_DIM
RET_CHUNK = 128
RET_THETA = 10000.0
ATT_HEADS = 4
ATT_DIM = 64
ATT_GROUP = ATT_HEADS * ATT_DIM
ATT_PATTERNS = ((128, 1), (512, 4), (2048, 16))
N_PAT = len(ATT_PATTERNS)
ATT_WIDTH = N_PAT * ATT_GROUP
ROPE_THETA = 500000.0
ROPE_DIMS = ATT_DIM // 4
ROPE_HALF = ROPE_DIMS // 2
QUERY_BLOCK = 128
POOL_WINDOWS = (2, 4, 8, 16)
POOL_GROUP = 128
POOL_WIDTH = len(POOL_WINDOWS) * POOL_GROUP
POOL_BUF = max(POOL_WINDOWS) - 1
POOL_PAD = 16
N_SUB = 3
PAST_LEN = 16384
HALF_STEP = 0.5
EPS = 1e-6
MASK_VALUE = -1e30

COL_RQ, COL_RK, COL_RV, COL_RG = 0, RET_WIDTH, 2 * RET_WIDTH, 3 * RET_WIDTH
COL_AQ = 4 * RET_WIDTH
COL_AK = COL_AQ + ATT_WIDTH
COL_AV = COL_AK + ATT_WIDTH
COL_PU = COL_AV + ATT_WIDTH
IN_WIDTH = COL_PU + POOL_WIDTH

V7X_VMEM_BYTES = 64 * 1024 * 1024
VMEM_LIMIT = V7X_VMEM_BYTES - 8 * 1024 * 1024
SUBLANES = 8

LOG_GAMMA = tuple(math.log1p(-(2.0 ** (-5.0 - h))) for h in range(RET_HEADS))


def _params(*sem):
    return pltpu.CompilerParams(dimension_semantics=sem, vmem_limit_bytes=VMEM_LIMIT)


def _dot(a, b):
    return jnp.dot(a, b, preferred_element_type=F32)


def _dot_nt(a, b):
    return lax.dot_general(a, b, (((1,), (1,)), ((), ())), preferred_element_type=F32)


def _bf16_round(x):
    return x.astype(BF16).astype(F32)


def _ada_kernel(c_ref, w_ref, b_ref, od_ref, op_ref, *, n_dec, n_pr):
    c = c_ref[...]
    a = (c * jax.nn.sigmoid(c)).astype(BF16)
    res = _dot(a, w_ref[...].astype(BF16)) + b_ref[...]
    od_ref[...] = res[0:n_dec]
    for b in range(n_pr):
        op_ref[b] = res[n_dec + b:n_dec + b + 1]


def _ada(c_all, n_dec, n_pr, w_ada, b_ada, tn=1024):
    depth, d, n = w_ada.shape
    rows = c_all.shape[0]
    per = d // tn
    return pl.pallas_call(
        functools.partial(_ada_kernel, n_dec=n_dec, n_pr=n_pr),
        out_shape=(jax.ShapeDtypeStruct((depth, N_SUB * 3, n_dec, d), F32),
                   jax.ShapeDtypeStruct((depth, N_SUB * 3, n_pr, 1, d), F32)),
        grid=(depth, n // tn),
        in_specs=[
            pl.BlockSpec((rows, d), lambda l, j: (0, 0)),
            pl.BlockSpec((None, d, tn), lambda l, j: (l, 0, j)),
            pl.BlockSpec((None, 1, tn), lambda l, j: (l, 0, j)),
        ],
        out_specs=(
            pl.BlockSpec((None, None, n_dec, tn), lambda l, j: (l, j // per, 0, j % per)),
            pl.BlockSpec((None, None, n_pr, 1, tn), lambda l, j: (l, j // per, 0, 0, j % per)),
        ),
        compiler_params=_params("arbitrary", "arbitrary"),
        name="ada_mod",
    )(c_all, w_ada, b_ada.reshape(depth, 1, n))


def _rmsnorm(x, g):
    ms = jnp.mean(x * x, axis=-1, keepdims=True)
    return x * lax.rsqrt(ms + EPS) * g


def _row_chunks(tm):
    rc = min(tm, 128)
    return rc, tm // rc


def _mod_rows(ref, r, rc):
    return ref[...] if ref.shape[0] == 1 else ref[pl.ds(r, rc), :]


def _prenorm_to(h_ref, x_ref, g_ref, sh_ref, sc_ref, unrolled=False):
    rc, n = _row_chunks(x_ref.shape[0])

    def body(i, carry):
        r = i * rc if unrolled else pl.multiple_of(i * rc, rc)
        y = _rmsnorm(x_ref[pl.ds(r, rc), :], g_ref[...])
        h = y * (1.0 + _mod_rows(sc_ref, r, rc)) + _mod_rows(sh_ref, r, rc)
        h_ref[pl.ds(r, rc), :] = h.astype(BF16)
        return carry

    if unrolled:
        for i in range(n):
            body(i, 0)
    else:
        lax.fori_loop(0, n, body, 0)


def _mod_specs(mod, layer, sub, rows_per_batch, tm, grid_rank):
    d = mod.shape[-1]
    specs = []
    for k in range(3):
        j = sub * 3 + k
        if mod.ndim == 5:
            if grid_rank == 2:
                idx = (lambda j: lambda i, f: (layer, j, (i * tm) // rows_per_batch, 0, 0))(j)
            else:
                idx = (lambda j: lambda i: (layer, j, (i * tm) // rows_per_batch, 0, 0))(j)
            specs.append(pl.BlockSpec((None, None, None, 1, d), idx))
        else:
            if grid_rank == 2:
                idx = (lambda j: lambda i, f: (layer, j, 0, 0))(j)
            else:
                idx = (lambda j: lambda i: (layer, j, 0, 0))(j)
            specs.append(pl.BlockSpec((None, None, tm, d), idx))
    return specs


def _ffn_kernel(x_ref, sh_ref, sc_ref, gt_ref, g_ref, wg_ref, wu_ref, wd_ref, *rest,
                n_f, final_norm, emit_bf16):
    rest = list(rest)
    fg_ref = rest.pop(0) if final_norm else None
    o_ref = rest.pop(0)
    wb_refs = [rest.pop(0) for _ in range(3)] if emit_bf16 else None
    h_ref, = rest
    f = pl.program_id(1)

    def step(first):
        if first:
            _prenorm_to(h_ref, x_ref, g_ref, sh_ref, sc_ref, unrolled=True)
        wg = wg_ref[...].astype(BF16)
        wu = wu_ref[...].astype(BF16)
        wd = wd_ref[...].astype(BF16)
        if emit_bf16:
            wb_refs[0][...] = wg
            wb_refs[1][...] = wu
            wb_refs[2][...] = wd
        h = h_ref[...]
        gate = _dot(h, wg)
        up = _dot(h, wu)
        act = (gate * jax.nn.sigmoid(gate) * up).astype(BF16)
        d = o_ref.shape[1]
        dc = min(d, 512)
        for c0 in range(0, d, dc):
            y = _dot(act, wd[:, c0:c0 + dc])
            if first:
                o_ref[:, c0:c0 + dc] = y
            else:
                o_ref[:, c0:c0 + dc] += y

    pl.when(f == 0)(lambda: step(True))
    pl.when(f > 0)(lambda: step(False))

    @pl.when(f == n_f - 1)
    def _():
        rc, n = _row_chunks(x_ref.shape[0])

        def body(i, carry):
            r = pl.multiple_of(i * rc, rc)
            rows = pl.ds(r, rc)
            out = x_ref[rows, :] + HALF_STEP * _mod_rows(gt_ref, r, rc) * o_ref[rows, :]
            if final_norm:
                out = _rmsnorm(out, fg_ref[...])
            o_ref[rows, :] = out
            return carry

        lax.fori_loop(0, n, body, 0)


def _weight_spec(w, layer, block, index, resident=False):
    mode = {"pipeline_mode": pl.Buffered(1)} if resident else {}
    if w.ndim == 3:
        return pl.BlockSpec((None,) + block, lambda i, j: (layer,) + index(i, j), **mode)
    return pl.BlockSpec(block, index, **mode)


def _ffn(x, mod, layer, sub, rows_per_batch, norm_g, wg, wu, wd, final_g, tm, tf, emit_bf16=False):
    m, d = x.shape
    d_ff = wg.shape[-1]
    assert m % tm == 0 and d_ff % tf == 0
    assert not emit_bf16 or m == tm, "each weight block must be visited exactly once"
    n_f = d_ff // tf
    final_norm = final_g is not None
    col_block = lambda i, j: (0, j)
    row_block = lambda i, j: (j, 0)
    in_specs = [pl.BlockSpec((tm, d), lambda i, j: (i, 0))]
    in_specs += _mod_specs(mod, layer, sub, rows_per_batch, tm, 2)
    in_specs += [
        pl.BlockSpec((None, None, 1, d), lambda i, j: (layer, sub, 0, 0)),
        _weight_spec(wg, layer, (d, tf), col_block),
        _weight_spec(wu, layer, (d, tf), col_block),
        _weight_spec(wd, layer, (tf, d), row_block),
    ]
    args = [x, mod, mod, mod, norm_g, wg, wu, wd]
    if final_norm:
        in_specs.append(pl.BlockSpec((1, d), lambda i, j: (0, 0)))
        args.append(final_g)
    out_shape = [jax.ShapeDtypeStruct((m, d), F32)]
    out_specs = [pl.BlockSpec((tm, d), lambda i, j: (i, 0))]
    if emit_bf16:
        out_shape += [jax.ShapeDtypeStruct((d, d_ff), BF16), jax.ShapeDtypeStruct((d, d_ff), BF16),
                      jax.ShapeDtypeStruct((d_ff, d), BF16)]
        out_specs += [pl.BlockSpec((d, tf), col_block), pl.BlockSpec((d, tf), col_block),
                      pl.BlockSpec((tf, d), row_block)]
    out = pl.pallas_call(
        functools.partial(_ffn_kernel, n_f=n_f, final_norm=final_norm, emit_bf16=emit_bf16),
        out_shape=tuple(out_shape),
        grid=(m // tm, n_f),
        in_specs=in_specs,
        out_specs=tuple(out_specs),
        scratch_shapes=[pltpu.VMEM((tm, d), BF16)],
        compiler_params=_params("parallel", "arbitrary"),
        name="ffn",
    )(*args)
    return out if emit_bf16 else out[0]


def _rotate_projection_block(y, col, ret_tabs, att_tabs):
    if col < COL_RV:
        y = _ret_rope(y, *ret_tabs)
        return y * (RET_DIM ** -0.5) if col >= COL_RK else y
    if COL_AQ <= col < COL_AV:
        y = _att_rope(y, *att_tabs)
        return y * (ATT_DIM ** -0.5) if col < COL_AK else y
    return y


def _inproj_kernel(x_ref, sh_ref, sc_ref, g_ref, w_ref, *rest, emit_bf16, single_col_tile, rotate):
    rest = list(rest)
    tab_refs = [rest.pop(0) for _ in range(5)] if rotate else None
    o_ref = rest.pop(0)
    wb_ref = rest.pop(0) if emit_bf16 else None
    h_ref, = rest

    if single_col_tile:
        _prenorm_to(h_ref, x_ref, g_ref, sh_ref, sc_ref, unrolled=True)
    else:
        @pl.when(pl.program_id(1) == 0)
        def _():
            _prenorm_to(h_ref, x_ref, g_ref, sh_ref, sc_ref)

    if emit_bf16:
        wb_ref[...] = w_ref[...].astype(BF16)
    h = h_ref[...]
    tn = o_ref.shape[1]
    nc = min(tn, 512)
    for c0 in range(0, tn, nc):
        c1 = min(c0 + nc, tn)
        y = _dot(h, w_ref[:, c0:c1].astype(BF16))
        if rotate:
            ret_tabs = (tab_refs[0][...], tab_refs[1][...])
            att_tabs = (tab_refs[2][...], tab_refs[3][...], tab_refs[4][...])
            for b0 in range(0, c1 - c0, 128):
                o_ref[:, c0 + b0:c0 + b0 + 128] = _rotate_projection_block(
                    y[:, b0:b0 + 128], c0 + b0, ret_tabs, att_tabs)
        else:
            o_ref[:, c0:c1] = y


def _inproj(x, mod, layer, rows_per_batch, norm_g, w_in, tm, tn, emit_bf16=False, rope_tabs=None):
    m, d = x.shape
    n = w_in.shape[-1]
    rotate = rope_tabs is not None
    assert m % tm == 0 and n % tn == 0
    assert not emit_bf16 or m == tm, "each weight block must be visited exactly once"
    assert not rotate or (n == tn and rows_per_batch % tm == 0)
    sh, sc, _ = _mod_specs(mod, layer, 1, rows_per_batch, tm, 2)
    col_block = lambda i, j: (0, j)
    in_specs = [
        pl.BlockSpec((tm, d), lambda i, j: (i, 0)),
        sh, sc,
        pl.BlockSpec((None, None, 1, d), lambda i, j: (layer, 1, 0, 0)),
        _weight_spec(w_in, layer, (d, tn), col_block, resident=(n == tn)),
    ]
    args = [x, mod, mod, norm_g, w_in]
    if rotate:
        tiles_per_batch = rows_per_batch // tm
        in_specs += [pl.BlockSpec((tm, 128), lambda i, j: (i % tiles_per_batch, 0))] * len(rope_tabs)
        args += list(rope_tabs)
    out_shape = [jax.ShapeDtypeStruct((m, n), F32)]
    out_specs = [pl.BlockSpec((tm, tn), lambda i, j: (i, j))]
    if emit_bf16:
        out_shape.append(jax.ShapeDtypeStruct((d, n), BF16))
        out_specs.append(pl.BlockSpec((d, tn), col_block))
    out = pl.pallas_call(
        functools.partial(_inproj_kernel, emit_bf16=emit_bf16, single_col_tile=(n == tn),
                          rotate=rotate),
        out_shape=tuple(out_shape),
        grid=(m // tm, n // tn),
        in_specs=in_specs,
        out_specs=tuple(out_specs),
        scratch_shapes=[pltpu.VMEM((tm, d), BF16)],
        compiler_params=_params("parallel", "arbitrary"),
        name="in_proj",
    )(*args)
    return out if emit_bf16 else out[0]


def _inproj_rows_kernel(x_ref, sh_ref, sc_ref, g_ref, w_ref, o_ref, wb_ref, h_ref, *, tk):
    k = pl.program_id(0)

    @pl.when(k == 0)
    def _():
        h = _rmsnorm(x_ref[...], g_ref[...]) * (1.0 + sc_ref[...]) + sh_ref[...]
        for c in range(h_ref.shape[0]):
            h_ref[c] = h[:, c * tk:(c + 1) * tk].astype(BF16)
        o_ref[...] = jnp.zeros_like(o_ref)

    wb = w_ref[...].astype(BF16)
    wb_ref[...] = wb
    o_ref[...] += _dot(h_ref[k], wb)


def _inproj_rows(x, mod, layer, norm_g, w_in, tk=512):
    m, d = x.shape
    n = w_in.shape[-1]
    assert d % tk == 0
    sh, sc, _ = _mod_specs(mod, layer, 1, 1, m, 1)
    return pl.pallas_call(
        functools.partial(_inproj_rows_kernel, tk=tk),
        out_shape=(jax.ShapeDtypeStruct((m, n), F32), jax.ShapeDtypeStruct((d, n), BF16)),
        grid=(d // tk,),
        in_specs=[
            pl.BlockSpec((m, d), lambda k: (0, 0)),
            sh, sc,
            pl.BlockSpec((None, None, 1, d), lambda k: (layer, 1, 0, 0)),
            pl.BlockSpec((None, tk, n), lambda k: (layer, k, 0)),
        ],
        out_specs=(pl.BlockSpec((m, n), lambda k: (0, 0)), pl.BlockSpec((tk, n), lambda k: (k, 0))),
        scratch_shapes=[pltpu.VMEM((d // tk, m, tk), BF16)],
        compiler_params=_params("arbitrary"),
        name="in_proj_rows",
    )(x, mod, mod, norm_g, w_in)


def _outproj_kernel(x_ref, gt_ref, ro_ref, ao_ref, po_ref, w_ref, o_ref, wb_ref):
    @pl.when(pl.program_id(0) == 0)
    def _():
        wb_ref[...] = w_ref[...].astype(BF16)

    r0, r1 = RET_WIDTH, RET_WIDTH + ATT_GROUP
    y = _dot(ro_ref[...].astype(BF16), wb_ref[0:r0, :])
    y += _dot(ao_ref[...].astype(BF16), wb_ref[r0:r1, :])
    y += _dot(po_ref[...].astype(BF16), wb_ref[r1:, :])
    o_ref[...] = x_ref[...] + gt_ref[...] * y


def _outproj(x, mod, layer, rows_per_batch, ro, ao, po, w_out, tm):
    m, d = x.shape
    k = w_out.shape[1]
    _, _, gt = _mod_specs(mod, layer, 1, rows_per_batch, tm, 1)
    return pl.pallas_call(
        _outproj_kernel,
        out_shape=jax.ShapeDtypeStruct((m, d), F32),
        grid=(m // tm,),
        in_specs=[
            pl.BlockSpec((tm, d), lambda i: (i, 0)),
            gt,
            pl.BlockSpec((tm, RET_WIDTH), lambda i: (i, 0)),
            pl.BlockSpec((tm, ATT_GROUP), lambda i: (i, 0)),
            pl.BlockSpec((tm, POOL_WIDTH), lambda i: (i, 0)),
            pl.BlockSpec((None, k, d), lambda i: (layer, 0, 0), pipeline_mode=pl.Buffered(1)),
        ],
        out_specs=pl.BlockSpec((tm, d), lambda i: (i, 0)),
        scratch_shapes=[pltpu.VMEM((k, d), BF16)],
        compiler_params=_params("arbitrary"),
        name="out_proj",
    )(x, mod, ro, ao, po, w_out)


def _ret_rope_tables(pos):
    half = RET_DIM // 2
    freq = jnp.power(jnp.float32(RET_THETA), -jnp.arange(half, dtype=F32) / half)
    ang = pos[:, None] * freq[None, :]
    cos, sin = jnp.cos(ang), jnp.sin(ang)
    return jnp.concatenate([cos, cos], axis=-1), jnp.concatenate([-sin, sin], axis=-1)


def _att_rope_tables(pos):
    freq = jnp.power(jnp.float32(ROPE_THETA), -jnp.arange(ROPE_HALF, dtype=F32) / ROPE_HALF)
    ang = pos[:, None] * freq[None, :]
    cos, sin = jnp.cos(ang), jnp.sin(ang)
    s = pos.shape[0]
    rest = ATT_DIM - ROPE_DIMS
    c = jnp.concatenate([cos, cos, jnp.ones((s, rest), F32)], axis=-1)
    a = jnp.concatenate([-sin, jnp.zeros((s, ATT_DIM - ROPE_HALF), F32)], axis=-1)
    b = jnp.concatenate([jnp.zeros((s, ROPE_HALF), F32), sin, jnp.zeros((s, rest), F32)], axis=-1)
    tile = lambda t: jnp.tile(t, (1, 128 // ATT_DIM))
    return tile(c), tile(a), tile(b)


def _ret_rope(x, cos_t, sin_t):
    return x * cos_t + pltpu.roll(x, RET_DIM // 2, axis=1) * sin_t


def _att_rope(x, c, a, b):
    n = x.shape[-1]
    return x * c + pltpu.roll(x, n - ROPE_HALF, axis=1) * a + pltpu.roll(x, ROPE_HALF, axis=1) * b


def _head_norm(o, g):
    mu = jnp.mean(o, axis=-1, keepdims=True)
    oc = o - mu
    var = jnp.mean(oc * oc, axis=-1, keepdims=True)
    return oc * lax.rsqrt(var + EPS) * g


def _ret_kernel(zq_ref, zk_ref, zv_ref, zg_ref, gn_ref,
                ro_ref, so_ref, din_ref, dq_ref, dk_ref, *, chunk, per_step):
    c = pl.program_id(1)

    @pl.when(c == 0)
    def _():
        so_ref[...] = jnp.zeros_like(so_ref)
        row = lax.broadcasted_iota(jnp.int32, (chunk, chunk), 0).astype(F32)
        col = lax.broadcasted_iota(jnp.int32, (chunk, chunk), 1).astype(F32)
        diff = row - col
        rowd = lax.broadcasted_iota(jnp.int32, (chunk, RET_DIM), 0).astype(F32)
        for h in range(RET_HEADS):
            lg = LOG_GAMMA[h]
            din_ref[h] = jnp.where(diff >= 0, jnp.exp(jnp.maximum(diff, 0.0) * lg), 0.0)
            dq_ref[h] = jnp.exp((rowd + 1.0) * lg)
            dk_ref[h] = jnp.exp((chunk - 1.0 - rowd) * lg)

    for h in range(RET_HEADS):
        cols = slice(h * RET_DIM, (h + 1) * RET_DIM)
        s_cur = so_ref[h]
        for j in range(per_step):
            rows = slice(j * chunk, (j + 1) * chunk)
            k = zk_ref[rows, cols]
            qb = zq_ref[rows, cols].astype(BF16)
            kb = k.astype(BF16)
            vb = zv_ref[rows, cols].astype(BF16)
            a = _dot_nt(qb, kb) * din_ref[h]
            o = _dot(a.astype(BF16), vb) + _dot(qb, s_cur.astype(BF16)) * dq_ref[h]
            kd_t = (k * dk_ref[h]).T.astype(BF16)
            s_cur = math.exp(chunk * LOG_GAMMA[h]) * s_cur + _dot(kd_t, vb)

            on = _head_norm(o, gn_ref[:, cols])
            g = zg_ref[rows, cols]
            ro_ref[rows, cols] = (g * jax.nn.sigmoid(g) * on).astype(BF16)
        so_ref[h] = s_cur


def _retention_prompt(z3, ret_norm_g, layer, per_step=8):
    b, s, _ = z3.shape
    chunk = math.gcd(s, RET_CHUNK)
    per_step = math.gcd(s // chunk, per_step)
    rows = chunk * per_step
    zspec = lambda cb: pl.BlockSpec((None, rows, RET_WIDTH), lambda i, c: (i, c, cb))
    return pl.pallas_call(
        functools.partial(_ret_kernel, chunk=chunk, per_step=per_step),
        out_shape=(jax.ShapeDtypeStruct((b, s, RET_WIDTH), BF16),
                   jax.ShapeDtypeStruct((b, RET_HEADS, RET_DIM, RET_DIM), F32)),
        grid=(b, s // rows),
        in_specs=[
            zspec(COL_RQ // RET_WIDTH), zspec(COL_RK // RET_WIDTH),
            zspec(COL_RV // RET_WIDTH), zspec(COL_RG // RET_WIDTH),
            pl.BlockSpec((None, 1, RET_WIDTH), lambda i, c: (layer, 0, 0)),
        ],
        out_specs=(
            pl.BlockSpec((None, rows, RET_WIDTH), lambda i, c: (i, c, 0)),
            pl.BlockSpec((None, RET_HEADS, RET_DIM, RET_DIM), lambda i, c: (i, 0, 0, 0)),
        ),
        scratch_shapes=[
            pltpu.VMEM((RET_HEADS, chunk, chunk), F32),
            pltpu.VMEM((RET_HEADS, chunk, RET_DIM), F32),
            pltpu.VMEM((RET_HEADS, chunk, RET_DIM), F32),
        ],
        compiler_params=_params("parallel", "arbitrary"),
        name="retention",
    )(z3, z3, z3, z3, ret_norm_g)


def _ret_dec_kernel(z_ref, s0_ref, cos_ref, sin_ref, gn_ref, ro_ref, so_ref, *, nb):
    cos_t = cos_ref[...]
    sin_t = sin_ref[...]
    row = lax.broadcasted_iota(jnp.int32, (nb, RET_DIM), 0)
    for h in range(RET_HEADS):
        gamma = math.exp(LOG_GAMMA[h])
        q = _ret_rope(z_ref[:, COL_RQ + h * RET_DIM:COL_RQ + (h + 1) * RET_DIM], cos_t, sin_t)
        k = _ret_rope(z_ref[:, COL_RK + h * RET_DIM:COL_RK + (h + 1) * RET_DIM], cos_t, sin_t)
        k = k * (RET_DIM ** -0.5)
        v = z_ref[:, COL_RV + h * RET_DIM:COL_RV + (h + 1) * RET_DIM]
        qr, kr, vr = _bf16_round(q), _bf16_round(k), _bf16_round(v)
        qk = jnp.sum(qr * kr, axis=-1, keepdims=True)
        o = _bf16_round(qk) * vr
        cross = jnp.zeros((nb, RET_DIM), F32)
        for b in range(nb):
            s_old = s0_ref[b, h]
            res = _dot(qr.astype(BF16), s_old.astype(BF16))
            cross = cross + jnp.where(row == b, res, 0.0)
            k_col = jnp.broadcast_to(kr[b:b + 1, :], (RET_DIM, RET_DIM)).T
            so_ref[b, h] = gamma * s_old + k_col * vr[b:b + 1, :]
        o = o + cross * gamma
        cols = slice(h * RET_DIM, (h + 1) * RET_DIM)
        on = _head_norm(o, gn_ref[:, cols])
        g = z_ref[:, COL_RG + h * RET_DIM:COL_RG + (h + 1) * RET_DIM]
        ro_ref[:, cols] = g * jax.nn.sigmoid(g) * on


def _retention_decode(z, state, cos_t, sin_t, ret_norm_g, layer):
    nb = z.shape[0]
    sshape = (nb, RET_HEADS, RET_DIM, RET_DIM)
    return pl.pallas_call(
        functools.partial(_ret_dec_kernel, nb=nb),
        out_shape=(jax.ShapeDtypeStruct((nb, RET_WIDTH), F32),
                   jax.ShapeDtypeStruct(sshape, F32)),
        grid=(1,),
        in_specs=[
            pl.BlockSpec(z.shape, lambda i: (0, 0)),
            pl.BlockSpec((None,) + sshape, lambda i: (layer, 0, 0, 0, 0)),
            pl.BlockSpec((1, RET_DIM), lambda i: (0, 0)),
            pl.BlockSpec((1, RET_DIM), lambda i: (0, 0)),
            pl.BlockSpec((None, 1, RET_WIDTH), lambda i: (layer, 0, 0)),
        ],
        out_specs=(
            pl.BlockSpec((nb, RET_WIDTH), lambda i: (0, 0)),
            pl.BlockSpec(sshape, lambda i: (0, 0, 0, 0)),
        ),
        compiler_params=_params("arbitrary"),
        name="retention_decode",
    )(z, state, cos_t, sin_t, ret_norm_g)


ATT_HALF = 128
ATT_MAX_STRIDE = 4
N_HALF = ATT_GROUP // ATT_HALF
HEADS_PER_HALF = ATT_HALF // ATT_DIM


def _half_cols(hf):
    return slice(hf * ATT_HALF, (hf + 1) * ATT_HALF)


def _att_kernel(*refs, seq):
    zqkv = (refs[0:N_HALF], refs[N_HALF:2 * N_HALF], refs[2 * N_HALF:3 * N_HALF])
    ao_ref, o_ref, lse_ref, st_ref, ost_ref = refs[3 * N_HALF:]
    g = pl.program_id(1)
    qb = QUERY_BLOCK
    n_blocks = seq // qb

    tq = lax.broadcasted_iota(jnp.int32, (qb, qb), 0)
    tk = lax.broadcasted_iota(jnp.int32, (qb, qb), 1)
    cur_valid = tk <= tq
    prev_valid = tk >= tq
    lane = lax.broadcasted_iota(jnp.int32, (qb, ATT_HALF), 1)
    head_masks = [(lane // ATT_DIM) == hh for hh in range(HEADS_PER_HALF)]

    def group_body(gi, dil):
        has_prev = True
        d1 = min(dil, ATT_MAX_STRIDE)
        d2 = dil // d1
        staged = d2 > 1
        assert d2 <= ATT_MAX_STRIDE and d1 * d2 == dil
        sub_len = seq // d1

        def strided(start, n, stride):
            return pl.ds(start, n, stride=stride) if stride > 1 else pl.ds(start, n)

        if staged:
            for hf in range(N_HALF):
                for r1 in range(d1):
                    src = strided(r1, sub_len, d1)
                    for which in range(3):
                        st_ref[which, hf, r1] = zqkv[which][hf][src, :]

        def load(which, hf, cls, blk):
            if staged:
                rows = strided(cls // d1 + d2 * qb * blk, qb, d2)
                return st_ref[which, hf, cls % d1, rows, :].astype(BF16)
            rows = strided(cls + dil * qb * blk, qb, dil)
            return zqkv[which][hf][rows, :].astype(BF16)

        def store(which, hf, cls, blk, val):
            if staged:
                rows = strided(cls // d1 + d2 * qb * blk, qb, d2)
                ost_ref[which, hf, cls % d1, rows, :] = val
            else:
                rows = strided(cls + dil * qb * blk, qb, dil)
                (o_ref, lse_ref)[which][gi, hf, rows, :] = val

        def block_body(t, carry):
            cls = t % dil
            blk = t // dil
            valid = cur_valid
            if has_prev:
                pblk = jnp.maximum(blk - 1, 0)
                valid = jnp.concatenate([prev_valid & (blk > 0), cur_valid], axis=1)
            for hf in range(N_HALF):
                q = load(0, hf, cls, blk)
                keys = load(1, hf, cls, blk)
                vals = load(2, hf, cls, blk)
                if has_prev:
                    keys = jnp.concatenate([load(1, hf, cls, pblk), keys], axis=0)
                    vals = jnp.concatenate([load(2, hf, cls, pblk), vals], axis=0)
                vals_ones = jnp.concatenate([vals, jnp.ones_like(vals)], axis=1)
                num = jnp.zeros((qb, ATT_HALF), F32)
                den = jnp.ones((qb, ATT_HALF), F32)
                lse_acc = jnp.zeros((qb, ATT_HALF), F32)
                for hm in head_masks:
                    qh = jnp.where(hm, q, jnp.zeros_like(q))
                    s = jnp.where(valid, _dot_nt(qh, keys), MASK_VALUE)
                    m = jnp.max(s, axis=-1, keepdims=True)
                    e = jnp.exp(s - m).astype(BF16)
                    r = _dot(e, vals_ones)
                    l = r[:, ATT_HALF:]
                    num = jnp.where(hm, r[:, :ATT_HALF], num)
                    den = jnp.where(hm, l, den)
                    lse_acc = jnp.where(hm, m + jnp.log(l), lse_acc)
                store(0, hf, cls, blk, num / den)
                store(1, hf, cls, blk, lse_acc)
            return carry

        lax.fori_loop(0, n_blocks, block_body, 0, unroll=4)

        if staged:
            for hf in range(N_HALF):
                for r1 in range(d1):
                    dst = strided(r1, sub_len, d1)
                    o_ref[gi, hf, dst, :] = ost_ref[0, hf, r1]
                    lse_ref[gi, hf, dst, :] = ost_ref[1, hf, r1]

    for gi, (_, dil) in enumerate(ATT_PATTERNS):
        @pl.when(g == gi)
        def _(gi=gi, dil=dil):
            group_body(gi, dil)

    @pl.when(g == N_PAT - 1)
    def _():
        def merge_body(i, carry):
            r = pl.multiple_of(i * qb, qb)
            rows = pl.ds(r, qb)
            for hf in range(N_HALF):
                lses = [lse_ref[gi, hf, rows, :] for gi in range(N_PAT)]
                m = functools.reduce(jnp.maximum, lses)
                ws = [jnp.exp(l - m) for l in lses]
                den = functools.reduce(lambda x, y: x + y, ws)
                num = functools.reduce(lambda x, y: x + y,
                                       [w * o_ref[gi, hf, rows, :] for gi, w in enumerate(ws)])
                ao_ref[rows, _half_cols(hf)] = (num / den).astype(BF16)
            return carry

        lax.fori_loop(0, n_blocks, merge_body, 0)


def _attention_prompt(z3):
    b, s, _ = z3.shape
    assert all(s % (dil * QUERY_BLOCK) == 0 for _, dil in ATT_PATTERNS)
    zspec = lambda col, hf: pl.BlockSpec(
        (None, s, ATT_HALF), lambda i, g: (i, 0, col // ATT_HALF + N_HALF * g + hf))
    zspecs = [zspec(col, hf) for col in (COL_AQ, COL_AK, COL_AV) for hf in range(N_HALF)]
    return pl.pallas_call(
        functools.partial(_att_kernel, seq=s),
        out_shape=jax.ShapeDtypeStruct((b, s, ATT_GROUP), BF16),
        grid=(b, N_PAT),
        in_specs=zspecs,
        out_specs=pl.BlockSpec((None, s, ATT_GROUP), lambda i, g: (i, 0, 0)),
        scratch_shapes=[
            pltpu.VMEM((N_PAT, N_HALF, s, ATT_HALF), F32),
            pltpu.VMEM((N_PAT, N_HALF, s, ATT_HALF), F32),
            pltpu.VMEM((3, N_HALF, ATT_MAX_STRIDE, s // ATT_MAX_STRIDE, ATT_HALF), F32),
            pltpu.VMEM((2, N_HALF, ATT_MAX_STRIDE, s // ATT_MAX_STRIDE, ATT_HALF), F32),
        ],
        compiler_params=_params("parallel", "arbitrary"),
        name="dilated_attention",
    )(*([z3] * (3 * N_HALF)))


def _cache_shift_kernel(*refs):
    n = len(refs) // 2
    for src, dst in zip(refs[:n], refs[n:]):
        h, dim, w = src.shape
        dst[...] = pltpu.roll(src[...].reshape(h * dim, w), w - 1, axis=1).reshape(h, dim, w)


def _cache_shift(caches):
    depth, nb = caches[0].shape[:2]
    specs = [pl.BlockSpec((None, None) + c.shape[2:], lambda l, i: (l, i, 0, 0, 0)) for c in caches]
    return pl.pallas_call(
        _cache_shift_kernel,
        out_shape=tuple(jax.ShapeDtypeStruct(c.shape, c.dtype) for c in caches),
        grid=(depth, nb),
        in_specs=specs,
        out_specs=tuple(specs),
        compiler_params=_params("arbitrary", "arbitrary"),
        name="cache_shift",
    )(*caches)


def _row_to_col(row, eye):
    return jnp.sum(jnp.where(eye, row, 0.0), axis=1, keepdims=True)


def _col_to_row(col, eye):
    return jnp.sum(jnp.where(eye, col, 0.0), axis=0, keepdims=True)


def _att_dec_kernel(z_ref, c_ref, a_ref, b_ref, *refs):
    kc_refs = refs[0:N_PAT]
    vc_refs = refs[N_PAT:2 * N_PAT]
    kt_refs = refs[2 * N_PAT:3 * N_PAT]
    vt_refs = refs[3 * N_PAT:4 * N_PAT]
    ao_ref = refs[4 * N_PAT]
    kt_out_refs = refs[4 * N_PAT + 1:5 * N_PAT + 1]
    vt_out_refs = refs[5 * N_PAT + 1:6 * N_PAT + 1]
    rope =(c_ref[...], a_ref[...], b_ref[...])
    eye = (lax.broadcasted_iota(jnp.int32, (ATT_DIM, ATT_DIM), 0)
           == lax.broadcasted_iota(jnp.int32, (ATT_DIM, ATT_DIM), 1))
    last_lane = lax.broadcasted_iota(jnp.int32, (ATT_DIM, ATT_HALF), 1) == ATT_HALF - 1

    head_rows = []
    for h in range(ATT_HEADS):
        hf, lanes = h // HEADS_PER_HALF, slice((h % HEADS_PER_HALF) * ATT_DIM,
                                               (h % HEADS_PER_HALF + 1) * ATT_DIM)
        outs, lses = [], []
        for gi, (win, dil) in enumerate(ATT_PATTERNS):
            off = gi * ATT_GROUP + hf * ATT_HALF
            zrow = lambda col: z_ref[:, col + off:col + off + ATT_HALF]
            q = (_att_rope(zrow(COL_AQ), *rope) * (ATT_DIM ** -0.5))[:, lanes]
            k_new = _att_rope(zrow(COL_AK), *rope)[:, lanes]
            v_new = zrow(COL_AV)[:, lanes]
            k_col, v_col = _row_to_col(k_new, eye), _row_to_col(v_new, eye)
            keys, vals = kc_refs[gi][h], vc_refs[gi][h]
            w = keys.shape[1]
            pos = lax.broadcasted_iota(jnp.int32, (1, w), 1)
            s_old = jnp.sum(keys * _row_to_col(q, eye), axis=0, keepdims=True)
            s_old = jnp.where(pos % dil == 0, s_old, MASK_VALUE)
            s_new = jnp.sum(q * k_new, axis=1, keepdims=True)
            m = jnp.maximum(jnp.max(s_old, axis=1, keepdims=True), s_new)
            e_old = jnp.exp(s_old - m)
            e_new = jnp.exp(s_new - m)
            l = jnp.sum(e_old, axis=1, keepdims=True) + e_new
            pv = jnp.sum(vals * e_old, axis=1, keepdims=True) + e_new * v_col
            outs.append(pv / l)
            lses.append(m + jnp.log(l))
            kt_out_refs[gi][h] = jnp.where(last_lane, k_col, kt_refs[gi][h])
            vt_out_refs[gi][h] = jnp.where(last_lane, v_col, vt_refs[gi][h])

        m = functools.reduce(jnp.maximum, lses)
        ws = [jnp.exp(l - m) for l in lses]
        den = functools.reduce(lambda x, y: x + y, ws)
        num = functools.reduce(lambda x, y: x + y, [w * o for w, o in zip(ws, outs)])
        head_rows.append(_col_to_row(num / den, eye))
    ao_ref[...] = jnp.concatenate(head_rows, axis=1)


def _attention_decode(z, tabs, k_caches, v_caches, k_next, v_next, layer):
    nb = z.shape[0]
    n_fixed = 4
    cspec = lambda c: pl.BlockSpec((None, None) + c.shape[2:], lambda i: (layer, i, 0, 0, 0))
    tail = lambda c: pl.BlockSpec((None, None) + c.shape[2:4] + (ATT_HALF,),
                                  lambda i: (layer, i, 0, 0, c.shape[4] // ATT_HALF - 1))
    tspec = pl.BlockSpec((1, ATT_HALF), lambda i: (0, 0))
    caches = list(k_caches) + list(v_caches)
    nexts = list(k_next) + list(v_next)
    return pl.pallas_call(
        _att_dec_kernel,
        out_shape=tuple([jax.ShapeDtypeStruct((nb, 1, ATT_GROUP), F32)]
                        + [jax.ShapeDtypeStruct(c.shape, c.dtype) for c in nexts]),
        grid=(nb,),
        in_specs=[pl.BlockSpec((None, 1, z.shape[1]), lambda i: (i, 0, 0)), tspec, tspec, tspec]
                 + [cspec(c) for c in caches] + [tail(c) for c in nexts],
        out_specs=tuple([pl.BlockSpec((None, 1, ATT_GROUP), lambda i: (i, 0, 0))]
                        + [tail(c) for c in nexts]),
        input_output_aliases={n_fixed + len(caches) + j: 1 + j for j in range(len(nexts))},
        compiler_params=_params("arbitrary"),
        name="dilated_attention_decode",
    )(z.reshape(nb, 1, z.shape[1]), *tabs, *caches, *nexts)


def _pool_kernel(u_ref, w_ref, sc_ref, po_ref, a_ref, b_ref, *, seq):
    g = pl.program_id(1)
    body = pl.ds(POOL_PAD, seq)

    def window_mean_minus_token(win):
        x = u_ref[...]
        a_ref[0:POOL_PAD, :] = jnp.zeros((POOL_PAD, POOL_GROUP), F32)
        b_ref[0:POOL_PAD, :] = jnp.zeros((POOL_PAD, POOL_GROUP), F32)
        a_ref[body, :] = x
        src, dst = a_ref, b_ref
        k = 1
        while k < win:
            dst[body, :] = src[body, :] + src[pl.ds(POOL_PAD - k, seq), :]
            src, dst = dst, src
            k *= 2
        t = lax.broadcasted_iota(jnp.int32, (seq, POOL_GROUP), 0)
        cnt = jnp.minimum(t + 1, win).astype(F32)
        pooled = src[body, :] / cnt - x
        y = _dot(pooled.astype(BF16), w_ref[...].astype(BF16)) * sc_ref[...]
        po_ref[...] = y.astype(BF16)

    for gi, win in enumerate(POOL_WINDOWS):
        @pl.when(g == gi)
        def _(win=win):
            window_mean_minus_token(win)


def _pool_prompt(z3, w_pool, pool_scale, layer):
    b, s, _ = z3.shape
    ng = len(POOL_WINDOWS)
    return pl.pallas_call(
        functools.partial(_pool_kernel, seq=s),
        out_shape=jax.ShapeDtypeStruct((b, s, POOL_WIDTH), BF16),
        grid=(b, ng),
        in_specs=[
            pl.BlockSpec((None, s, POOL_GROUP), lambda i, g: (i, 0, COL_PU // POOL_GROUP + g)),
            pl.BlockSpec((None, None, POOL_GROUP, POOL_GROUP), lambda i, g: (layer, g, 0, 0)),
            pl.BlockSpec((None, 1, POOL_GROUP), lambda i, g: (layer, 0, g)),
        ],
        out_specs=pl.BlockSpec((None, s, POOL_GROUP), lambda i, g: (i, 0, g)),
        scratch_shapes=[pltpu.VMEM((POOL_PAD + s, POOL_GROUP), F32),
                        pltpu.VMEM((POOL_PAD + s, POOL_GROUP), F32)],
        compiler_params=_params("parallel", "arbitrary"),
        name="pool_mixer",
    )(z3, w_pool, pool_scale)


def _pool_dec_kernel(z_ref, buf_ref, w_ref, sc_ref, po_ref, bo_ref, pooled_ref, *, nb):
    row = lax.broadcasted_iota(jnp.int32, (POOL_BUF, POOL_GROUP), 0)
    for b in range(nb):
        u = z_ref[b:b + 1, COL_PU:COL_PU + POOL_WIDTH]
        old = buf_ref[b]
        for gi, win in enumerate(POOL_WINDOWS):
            cols = slice(gi * POOL_GROUP, (gi + 1) * POOL_GROUP)
            tail = jnp.where(row >= POOL_BUF - (win - 1), old[:, cols], 0.0)
            total = jnp.sum(tail, axis=0, keepdims=True) + u[:, cols]
            pooled_ref[b:b + 1, cols] = total / float(win) - u[:, cols]
        bo_ref[b, 0:POOL_BUF - 1, :] = old[1:POOL_BUF, :]
        bo_ref[b, POOL_BUF - 1:POOL_BUF, :] = u
    for gi in range(len(POOL_WINDOWS)):
        cols = slice(gi * POOL_GROUP, (gi + 1) * POOL_GROUP)
        y = _dot(pooled_ref[:, cols].astype(BF16), w_ref[gi].astype(BF16))
        po_ref[:, cols] = y * sc_ref[:, cols]


def _pool_decode(z, cache_pool, w_pool, pool_scale, layer):
    nb = z.shape[0]
    ng = len(POOL_WINDOWS)
    bshape = (nb, POOL_BUF, POOL_WIDTH)
    return pl.pallas_call(
        functools.partial(_pool_dec_kernel, nb=nb),
        out_shape=(jax.ShapeDtypeStruct((nb, POOL_WIDTH), F32),
                   jax.ShapeDtypeStruct(bshape, F32)),
        grid=(1,),
        in_specs=[
            pl.BlockSpec(z.shape, lambda i: (0, 0)),
            pl.BlockSpec((None,) + bshape, lambda i: (layer, 0, 0, 0)),
            pl.BlockSpec((None, ng, POOL_GROUP, POOL_GROUP), lambda i: (layer, 0, 0, 0)),
            pl.BlockSpec((None, 1, POOL_WIDTH), lambda i: (layer, 0, 0)),
        ],
        out_specs=(pl.BlockSpec((nb, POOL_WIDTH), lambda i: (0, 0)),
                   pl.BlockSpec(bshape, lambda i: (0, 0, 0))),
        scratch_shapes=[pltpu.VMEM((nb, POOL_WIDTH), F32)],
        compiler_params=_params("arbitrary"),
        name="pool_mixer_decode",
    )(z, cache_pool, w_pool, pool_scale)


def _trunk_prompt(x, mod, p, wb, final_g):
    b, s, d = x.shape
    depth = p["w_in"].shape[0]
    pos = jnp.arange(s, dtype=F32)
    ret_tabs = _ret_rope_tables(pos)
    att_tabs = _att_rope_tables(pos)
    keep = tuple(min(win, s) for win, _ in ATT_PATTERNS)
    tm = math.gcd(s, 1024)
    xf = x.reshape(b * s, d)
    rets, pools = [], []
    ks = [[] for _ in ATT_PATTERNS]
    vs = [[] for _ in ATT_PATTERNS]
    for l in range(depth):
        xf = _ffn(xf, mod, l, 0, s, p["norm_g"], *wb[l]["ffn1"], None, tm, 512)
        z = _inproj(xf, mod, l, s, p["norm_g"], wb[l]["w_in"], math.gcd(s, 256), IN_WIDTH,
                    rope_tabs=ret_tabs + att_tabs)
        z3 = z.reshape(b, s, IN_WIDTH)
        ro, ret_s = _retention_prompt(z3, p["ret_norm_g"], l)
        ao = _attention_prompt(z3)
        po = _pool_prompt(z3, p["w_pool"], p["pool_scale"], l)
        xf = _outproj(xf, mod, l, s, ro.reshape(b * s, -1), ao.reshape(b * s, -1),
                      po.reshape(b * s, -1), p["w_out"], math.gcd(s, 512))
        xf = _ffn(xf, mod, l, 2, s, p["norm_g"], *wb[l]["ffn2"],
                  final_g if l == depth - 1 else None, tm, 512)
        rets.append(ret_s)
        pools.append(z3[:, s - POOL_BUF:, COL_PU:])
        for g in range(N_PAT):
            for col, dst in ((COL_AK, ks), (COL_AV, vs)):
                c0 = col + g * ATT_GROUP
                dst[g].append(z3[:, s - keep[g]:, c0:c0 + ATT_GROUP]
                              .reshape(b, keep[g], ATT_HEADS, ATT_DIM))
    y = xf.reshape(b, s, d)
    return (y, jnp.stack(rets), [jnp.stack(k) for k in ks], [jnp.stack(v) for v in vs],
            jnp.stack(pools))


def _trunk_decode(x, mod, pos0, caches, p, final_g):
    nb, s, d = x.shape
    assert s == 1, "decode trunk handles one new token per batch row"
    depth = p["w_in"].shape[0]
    state_ret, cks, cvs, cpool = caches
    pos = pos0 + jnp.arange(s, dtype=F32)
    ret_tabs = _ret_rope_tables(pos)
    att_tabs = _att_rope_tables(pos)
    feature_major = lambda c: jnp.transpose(c, (0, 1, 3, 4, 2))
    position_major = lambda c: jnp.transpose(c, (0, 1, 4, 2, 3))
    cks = [feature_major(c) for c in cks]
    cvs = [feature_major(c) for c in cvs]
    advanced = _cache_shift(cks + cvs)
    k_next, v_next = list(advanced[:N_PAT]), list(advanced[N_PAT:])
    xf = x.reshape(nb, d)
    rets, pools, wb = [], [], []
    for l in range(depth):
        xf, *ffn1_bf16 = _ffn(xf, mod, l, 0, 1, p["norm_g"], p["w1_gate"], p["w1_up"], p["w1_down"],
                              None, nb, 512, emit_bf16=True)
        z, w_in_bf16 = _inproj_rows(xf, mod, l, p["norm_g"], p["w_in"])
        ro, ret_s = _retention_decode(z, state_ret, *ret_tabs, p["ret_norm_g"], l)
        att = _attention_decode(z, att_tabs, cks, cvs, k_next, v_next, l)
        ao, k_next, v_next = att[0].reshape(nb, ATT_GROUP), list(att[1:1 + N_PAT]), list(att[1 + N_PAT:])
        po, pool_new = _pool_decode(z, cpool, p["w_pool"], p["pool_scale"], l)
        xf = _outproj(xf, mod, l, 1, ro, ao, po, p["w_out"], nb)
        xf, *ffn2_bf16 = _ffn(xf, mod, l, 2, 1, p["norm_g"], p["w2_gate"], p["w2_up"], p["w2_down"],
                              final_g if l == depth - 1 else None, nb, 512, emit_bf16=True)
        wb.append({"ffn1": ffn1_bf16, "w_in": w_in_bf16, "ffn2": ffn2_bf16})
        rets.append(ret_s)
        pools.append(pool_new)
    y = xf.reshape(nb, s, d)
    return (y, jnp.stack(rets), [position_major(k) for k in k_next],
            [position_major(v) for v in v_next], jnp.stack(pools), wb)


def kernel(x_prompt, x_sample, state_ret, cache_k_w128, cache_v_w128, cache_k_w512, cache_v_w512,
           cache_k_w2048, cache_v_w2048, cache_pool, c_prompt, c_sample, w_ada, b_ada, norm_g, w_in,
           ret_norm_g, w_pool, pool_scale, w_out, w1_gate, w1_up, w1_down, w2_gate, w2_up, w2_down,
           final_norm_g):
    depth, d = norm_g.shape[0], norm_g.shape[-1]
    n_pr, n_dec = c_prompt.shape[0], c_sample.shape[0]
    p = {
        "norm_g": norm_g.reshape(depth, N_SUB, 1, d),
        "w_in": w_in,
        "ret_norm_g": ret_norm_g.reshape(depth, 1, RET_WIDTH),
        "w_pool": w_pool,
        "pool_scale": pool_scale.reshape(depth, 1, POOL_WIDTH),
        "w_out": w_out,
        "w1_gate": w1_gate, "w1_up": w1_up, "w1_down": w1_down,
        "w2_gate": w2_gate, "w2_up": w2_up, "w2_down": w2_down,
    }
    final_g = final_norm_g.reshape(1, d)

    pad = (-(n_dec + n_pr)) % SUBLANES
    c_all = jnp.concatenate([c_sample, c_prompt, jnp.zeros((pad, d), F32)], axis=0)
    mod_dec, mod_pr = _ada(c_all, n_dec, n_pr, w_ada, b_ada)

    caches = (state_ret,
              (cache_k_w128, cache_k_w512, cache_k_w2048),
              (cache_v_w128, cache_v_w512, cache_v_w2048),
              cache_pool)
    y_s, ret_s, ks, vs, pool_s, wb = _trunk_decode(x_sample, mod_dec, float(PAST_LEN), caches, p, final_g)
    y_p, ret_p, kp, vp, pool_p = _trunk_prompt(x_prompt, mod_pr, p, wb, final_g)
    return (y_p, y_s, ret_p, ret_s,
            kp[0], ks[0], vp[0], vs[0],
            kp[1], ks[1], vp[1], vs[1],
            kp[2], ks[2], vp[2], vs[2],
            pool_p, pool_s)
```

```python
import functools
import math

import jax
import jax.numpy as jnp
from jax import lax
from jax.experimental import pallas as pl
from jax.experimental.pallas import tpu as pltpu

F32 = jnp.float32
BF16 = jnp.bfloat16

RET_HEADS = 6
RET_DIM = 128
RET_WIDTH = RET_HEADS * RET_DIM
RET_CHUNK = 128
RET_THETA = 10000.0
ATT_HEADS = 4
ATT_DIM = 64
ATT_GROUP = ATT_HEADS * ATT_DIM
ATT_PATTERNS = ((128, 1), (512, 4), (2048, 16))
N_PAT = len(ATT_PATTERNS)
ATT_WIDTH = N_PAT * ATT_GROUP
ROPE_THETA = 500000.0
ROPE_DIMS = ATT_DIM // 4
ROPE_HALF = ROPE_DIMS // 2
QUERY_BLOCK = 128
POOL_WINDOWS = (2, 4, 8, 16)
POOL_GROUP = 128
POOL_WIDTH = len(POOL_WINDOWS) * POOL_GROUP
POOL_BUF = max(POOL_WINDOWS) - 1
POOL_PAD = 16
N_SUB = 3
PAST_LEN = 16384
HALF_STEP = 0.5
EPS = 1e-6
MASK_VALUE = -1e30

COL_RQ, COL_RK, COL_RV, COL_RG = 0, RET_WIDTH, 2 * RET_WIDTH, 3 * RET_WIDTH
COL_AQ = 4 * RET_WIDTH
COL_AK = COL_AQ + ATT_WIDTH
COL_AV = COL_AK + ATT_WIDTH
COL_PU = COL_AV + ATT_WIDTH
IN_WIDTH = COL_PU + POOL_WIDTH

V7X_VMEM_BYTES = 64 * 1024 * 1024
VMEM_LIMIT = V7X_VMEM_BYTES - 8 * 1024 * 1024
SUBLANES = 8

LOG_GAMMA = tuple(math.log1p(-(2.0 ** (-5.0 - h))) for h in range(RET_HEADS))


def _params(*sem):
    return pltpu.CompilerParams(dimension_semantics=sem, vmem_limit_bytes=VMEM_LIMIT)


def _dot(a, b):
    return jnp.dot(a, b, preferred_element_type=F32)


def _dot_nt(a, b):
    return lax.dot_general(a, b, (((1,), (1,)), ((), ())), preferred_element_type=F32)


def _bf16_round(x):
    return x.astype(BF16).astype(F32)


def _ada_kernel(c_ref, w_ref, b_ref, od_ref, op_ref, *, n_dec, n_pr):
    c = c_ref[...]
    a = (c * jax.nn.sigmoid(c)).astype(BF16)
    res = _dot(a, w_ref[...].astype(BF16)) + b_ref[...]
    od_ref[...] = res[0:n_dec]
    for b in range(n_pr):
        op_ref[b] = res[n_dec + b:n_dec + b + 1]


def _ada(c_all, n_dec, n_pr, w_ada, b_ada, tn=1024):
    depth, d, n = w_ada.shape
    rows = c_all.shape[0]
    per = d // tn
    return pl.pallas_call(
        functools.partial(_ada_kernel, n_dec=n_dec, n_pr=n_pr),
        out_shape=(jax.ShapeDtypeStruct((depth, N_SUB * 3, n_dec, d), F32),
                   jax.ShapeDtypeStruct((depth, N_SUB * 3, n_pr, 1, d), F32)),
        grid=(depth, n // tn),
        in_specs=[
            pl.BlockSpec((rows, d), lambda l, j: (0, 0)),
            pl.BlockSpec((None, d, tn), lambda l, j: (l, 0, j)),
            pl.BlockSpec((None, 1, tn), lambda l, j: (l, 0, j)),
        ],
        out_specs=(
            pl.BlockSpec((None, None, n_dec, tn), lambda l, j: (l, j // per, 0, j % per)),
            pl.BlockSpec((None, None, n_pr, 1, tn), lambda l, j: (l, j // per, 0, 0, j % per)),
        ),
        compiler_params=_params("arbitrary", "arbitrary"),
        name="ada_mod",
    )(c_all, w_ada, b_ada.reshape(depth, 1, n))


def _rmsnorm(x, g):
    ms = jnp.mean(x * x, axis=-1, keepdims=True)
    return x * lax.rsqrt(ms + EPS) * g


def _row_chunks(tm):
    rc = min(tm, 128)
    return rc, tm // rc


def _mod_rows(ref, r, rc):
    return ref[...] if ref.shape[0] == 1 else ref[pl.ds(r, rc), :]


def _prenorm_to(h_ref, x_ref, g_ref, sh_ref, sc_ref, unrolled=False):
    rc, n = _row_chunks(x_ref.shape[0])

    def body(i, carry):
        r = i * rc if unrolled else pl.multiple_of(i * rc, rc)
        y = _rmsnorm(x_ref[pl.ds(r, rc), :], g_ref[...])
        h = y * (1.0 + _mod_rows(sc_ref, r, rc)) + _mod_rows(sh_ref, r, rc)
        h_ref[pl.ds(r, rc), :] = h.astype(BF16)
        return carry

    if unrolled:
        for i in range(n):
            body(i, 0)
    else:
        lax.fori_loop(0, n, body, 0)


def _mod_specs(mod, layer, sub, rows_per_batch, tm, grid_rank):
    d = mod.shape[-1]
    specs = []
    for k in range(3):
        j = sub * 3 + k
        if mod.ndim == 5:
            if grid_rank == 2:
                idx = (lambda j: lambda i, f: (layer, j, (i * tm) // rows_per_batch, 0, 0))(j)
            else:
                idx = (lambda j: lambda i: (layer, j, (i * tm) // rows_per_batch, 0, 0))(j)
            specs.append(pl.BlockSpec((None, None, None, 1, d), idx))
        else:
            if grid_rank == 2:
                idx = (lambda j: lambda i, f: (layer, j, 0, 0))(j)
            else:
                idx = (lambda j: lambda i: (layer, j, 0, 0))(j)
            specs.append(pl.BlockSpec((None, None, tm, d), idx))
    return specs


def _ffn_kernel(x_ref, sh_ref, sc_ref, gt_ref, g_ref, wg_ref, wu_ref, wd_ref, *rest,
                n_f, final_norm, emit_bf16):
    rest = list(rest)
    fg_ref = rest.pop(0) if final_norm else None
    o_ref = rest.pop(0)
    wb_refs = [rest.pop(0) for _ in range(3)] if emit_bf16 else None
    h_ref, = rest
    f = pl.program_id(1)

    assert n_f >= 2
    fuse_residual = not final_norm

    def step(first, last):
        if first:
            _prenorm_to(h_ref, x_ref, g_ref, sh_ref, sc_ref, unrolled=True)
        wg = wg_ref[...].astype(BF16)
        wu = wu_ref[...].astype(BF16)
        wd = wd_ref[...].astype(BF16)
        if emit_bf16:
            wb_refs[0][...] = wg
            wb_refs[1][...] = wu
            wb_refs[2][...] = wd
        h = h_ref[...]
        gate = _dot(h, wg)
        up = _dot(h, wu)
        act = (gate * jax.nn.sigmoid(gate) * up).astype(BF16)
        d = o_ref.shape[1]
        dc = min(d, 512)
        for c0 in range(0, d, dc):
            cols = slice(c0, c0 + dc)
            y = _dot(act, wd[:, cols])
            if first:
                o_ref[:, cols] = y
            elif last and fuse_residual:
                o_ref[:, cols] = x_ref[:, cols] + HALF_STEP * gt_ref[:, cols] * (o_ref[:, cols] + y)
            else:
                o_ref[:, cols] += y

    pl.when(f == 0)(lambda: step(True, False))
    pl.when((f > 0) & (f < n_f - 1))(lambda: step(False, False))
    pl.when(f == n_f - 1)(lambda: step(False, True))

    if not fuse_residual:
        @pl.when(f == n_f - 1)
        def _():
            rc, n = _row_chunks(x_ref.shape[0])

            def body(i, carry):
                r = pl.multiple_of(i * rc, rc)
                rows = pl.ds(r, rc)
                out = x_ref[rows, :] + HALF_STEP * _mod_rows(gt_ref, r, rc) * o_ref[rows, :]
                o_ref[rows, :] = _rmsnorm(out, fg_ref[...])
                return carry

            lax.fori_loop(0, n, body, 0)


def _weight_spec(w, layer, block, index, resident=False):
    mode = {"pipeline_mode": pl.Buffered(1)} if resident else {}
    if w.ndim == 3:
        return pl.BlockSpec((None,) + block, lambda i, j: (layer,) + index(i, j), **mode)
    return pl.BlockSpec(block, index, **mode)


def _ffn(x, mod, layer, sub, rows_per_batch, norm_g, wg, wu, wd, final_g, tm, tf, emit_bf16=False):
    m, d = x.shape
    d_ff = wg.shape[-1]
    assert m % tm == 0 and d_ff % tf == 0
    assert not emit_bf16 or m == tm, "each weight block must be visited exactly once"
    n_f = d_ff // tf
    final_norm = final_g is not None
    col_block = lambda i, j: (0, j)
    row_block = lambda i, j: (j, 0)
    in_specs = [pl.BlockSpec((tm, d), lambda i, j: (i, 0))]
    in_specs += _mod_specs(mod, layer, sub, rows_per_batch, tm, 2)
    in_specs += [
        pl.BlockSpec((None, None, 1, d), lambda i, j: (layer, sub, 0, 0)),
        _weight_spec(wg, layer, (d, tf), col_block),
        _weight_spec(wu, layer, (d, tf), col_block),
        _weight_spec(wd, layer, (tf, d), row_block),
    ]
    args = [x, mod, mod, mod, norm_g, wg, wu, wd]
    if final_norm:
        in_specs.append(pl.BlockSpec((1, d), lambda i, j: (0, 0)))
        args.append(final_g)
    out_shape = [jax.ShapeDtypeStruct((m, d), F32)]
    out_specs = [pl.BlockSpec((tm, d), lambda i, j: (i, 0))]
    if emit_bf16:
        out_shape += [jax.ShapeDtypeStruct((d, d_ff), BF16), jax.ShapeDtypeStruct((d, d_ff), BF16),
                      jax.ShapeDtypeStruct((d_ff, d), BF16)]
        out_specs += [pl.BlockSpec((d, tf), col_block), pl.BlockSpec((d, tf), col_block),
                      pl.BlockSpec((tf, d), row_block)]
    out = pl.pallas_call(
        functools.partial(_ffn_kernel, n_f=n_f, final_norm=final_norm, emit_bf16=emit_bf16),
        out_shape=tuple(out_shape),
        grid=(m // tm, n_f),
        in_specs=in_specs,
        out_specs=tuple(out_specs),
        scratch_shapes=[pltpu.VMEM((tm, d), BF16)],
        compiler_params=_params("parallel", "arbitrary"),
        name="ffn",
    )(*args)
    return out if emit_bf16 else out[0]


def _rotate_projection_block(y, col, ret_tabs, att_tabs):
    if col < COL_RV:
        y = _ret_rope(y, *ret_tabs)
        return y * (RET_DIM ** -0.5) if col >= COL_RK else y
    if COL_AQ <= col < COL_AV:
        y = _att_rope(y, *att_tabs)
        return y * (ATT_DIM ** -0.5) if col < COL_AK else y
    return y


def _inproj_kernel(x_ref, sh_ref, sc_ref, g_ref, w_ref, *rest, emit_bf16, single_col_tile, rotate):
    rest = list(rest)
    tab_refs = [rest.pop(0) for _ in range(5)] if rotate else None
    o_ref = rest.pop(0)
    wb_ref = rest.pop(0) if emit_bf16 else None
    h_ref, = rest

    if single_col_tile:
        _prenorm_to(h_ref, x_ref, g_ref, sh_ref, sc_ref, unrolled=True)
    else:
        @pl.when(pl.program_id(1) == 0)
        def _():
            _prenorm_to(h_ref, x_ref, g_ref, sh_ref, sc_ref)

    if emit_bf16:
        wb_ref[...] = w_ref[...].astype(BF16)
    h = h_ref[...]
    tn = o_ref.shape[1]
    nc = min(tn, 512)
    for c0 in range(0, tn, nc):
        c1 = min(c0 + nc, tn)
        y = _dot(h, w_ref[:, c0:c1].astype(BF16))
        if rotate:
            ret_tabs = (tab_refs[0][...], tab_refs[1][...])
            att_tabs = (tab_refs[2][...], tab_refs[3][...], tab_refs[4][...])
            for b0 in range(0, c1 - c0, 128):
                o_ref[:, c0 + b0:c0 + b0 + 128] = _rotate_projection_block(
                    y[:, b0:b0 + 128], c0 + b0, ret_tabs, att_tabs)
        else:
            o_ref[:, c0:c1] = y


def _inproj(x, mod, layer, rows_per_batch, norm_g, w_in, tm, tn, emit_bf16=False, rope_tabs=None):
    m, d = x.shape
    n = w_in.shape[-1]
    rotate = rope_tabs is not None
    assert m % tm == 0 and n % tn == 0
    assert not emit_bf16 or m == tm, "each weight block must be visited exactly once"
    assert not rotate or (n == tn and rows_per_batch % tm == 0)
    sh, sc, _ = _mod_specs(mod, layer, 1, rows_per_batch, tm, 2)
    col_block = lambda i, j: (0, j)
    in_specs = [
        pl.BlockSpec((tm, d), lambda i, j: (i, 0)),
        sh, sc,
        pl.BlockSpec((None, None, 1, d), lambda i, j: (layer, 1, 0, 0)),
        _weight_spec(w_in, layer, (d, tn), col_block, resident=(n == tn)),
    ]
    args = [x, mod, mod, norm_g, w_in]
    if rotate:
        tiles_per_batch = rows_per_batch // tm
        in_specs += [pl.BlockSpec((tm, 128), lambda i, j: (i % tiles_per_batch, 0))] * len(rope_tabs)
        args += list(rope_tabs)
    out_shape = [jax.ShapeDtypeStruct((m, n), F32)]
    out_specs = [pl.BlockSpec((tm, tn), lambda i, j: (i, j))]
    if emit_bf16:
        out_shape.append(jax.ShapeDtypeStruct((d, n), BF16))
        out_specs.append(pl.BlockSpec((d, tn), col_block))
    out = pl.pallas_call(
        functools.partial(_inproj_kernel, emit_bf16=emit_bf16, single_col_tile=(n == tn),
                          rotate=rotate),
        out_shape=tuple(out_shape),
        grid=(m // tm, n // tn),
        in_specs=in_specs,
        out_specs=tuple(out_specs),
        scratch_shapes=[pltpu.VMEM((tm, d), BF16)],
        compiler_params=_params("parallel", "arbitrary"),
        name="in_proj",
    )(*args)
    return out if emit_bf16 else out[0]


def _inproj_rows_kernel(x_ref, sh_ref, sc_ref, g_ref, w_ref, o_ref, wb_ref, h_ref, *, tk):
    k = pl.program_id(0)

    @pl.when(k == 0)
    def _():
        h = _rmsnorm(x_ref[...], g_ref[...]) * (1.0 + sc_ref[...]) + sh_ref[...]
        for c in range(h_ref.shape[0]):
            h_ref[c] = h[:, c * tk:(c + 1) * tk].astype(BF16)
        o_ref[...] = jnp.zeros_like(o_ref)

    wb = w_ref[...].astype(BF16)
    wb_ref[...] = wb
    o_ref[...] += _dot(h_ref[k], wb)


def _inproj_rows(x, mod, layer, norm_g, w_in, tk=512):
    m, d = x.shape
    n = w_in.shape[-1]
    assert d % tk == 0
    sh, sc, _ = _mod_specs(mod, layer, 1, 1, m, 1)
    return pl.pallas_call(
        functools.partial(_inproj_rows_kernel, tk=tk),
        out_shape=(jax.ShapeDtypeStruct((m, n), F32), jax.ShapeDtypeStruct((d, n), BF16)),
        grid=(d // tk,),
        in_specs=[
            pl.BlockSpec((m, d), lambda k: (0, 0)),
            sh, sc,
            pl.BlockSpec((None, None, 1, d), lambda k: (layer, 1, 0, 0)),
            pl.BlockSpec((None, tk, n), lambda k: (layer, k, 0)),
        ],
        out_specs=(pl.BlockSpec((m, n), lambda k: (0, 0)), pl.BlockSpec((tk, n), lambda k: (k, 0))),
        scratch_shapes=[pltpu.VMEM((d // tk, m, tk), BF16)],
        compiler_params=_params("arbitrary"),
        name="in_proj_rows",
    )(x, mod, mod, norm_g, w_in)


def _outproj_kernel(x_ref, gt_ref, ro_ref, ao_ref, po_ref, w_ref, o_ref, wb_ref):
    @pl.when(pl.program_id(0) == 0)
    def _():
        wb_ref[...] = w_ref[...].astype(BF16)

    mix = jnp.concatenate([ro_ref[...].astype(BF16), ao_ref[...].astype(BF16),
                           po_ref[...].astype(BF16)], axis=1)
    o_ref[...] = x_ref[...] + gt_ref[...] * _dot(mix, wb_ref[...])


def _outproj(x, mod, layer, rows_per_batch, ro, ao, po, w_out, tm):
    m, d = x.shape
    k = w_out.shape[1]
    _, _, gt = _mod_specs(mod, layer, 1, rows_per_batch, tm, 1)
    return pl.pallas_call(
        _outproj_kernel,
        out_shape=jax.ShapeDtypeStruct((m, d), F32),
        grid=(m // tm,),
        in_specs=[
            pl.BlockSpec((tm, d), lambda i: (i, 0)),
            gt,
            pl.BlockSpec((tm, RET_WIDTH), lambda i: (i, 0)),
            pl.BlockSpec((tm, ATT_GROUP), lambda i: (i, 0)),
            pl.BlockSpec((tm, POOL_WIDTH), lambda i: (i, 0)),
            pl.BlockSpec((None, k, d), lambda i: (layer, 0, 0), pipeline_mode=pl.Buffered(1)),
        ],
        out_specs=pl.BlockSpec((tm, d), lambda i: (i, 0)),
        scratch_shapes=[pltpu.VMEM((k, d), BF16)],
        compiler_params=_params("arbitrary"),
        name="out_proj",
    )(x, mod, ro, ao, po, w_out)


def _ret_rope_tables(pos):
    half = RET_DIM // 2
    freq = jnp.power(jnp.float32(RET_THETA), -jnp.arange(half, dtype=F32) / half)
    ang = pos[:, None] * freq[None, :]
    cos, sin = jnp.cos(ang), jnp.sin(ang)
    return jnp.concatenate([cos, cos], axis=-1), jnp.concatenate([-sin, sin], axis=-1)


def _att_rope_tables(pos):
    freq = jnp.power(jnp.float32(ROPE_THETA), -jnp.arange(ROPE_HALF, dtype=F32) / ROPE_HALF)
    ang = pos[:, None] * freq[None, :]
    cos, sin = jnp.cos(ang), jnp.sin(ang)
    s = pos.shape[0]
    rest = ATT_DIM - ROPE_DIMS
    c = jnp.concatenate([cos, cos, jnp.ones((s, rest), F32)], axis=-1)
    a = jnp.concatenate([-sin, jnp.zeros((s, ATT_DIM - ROPE_HALF), F32)], axis=-1)
    b = jnp.concatenate([jnp.zeros((s, ROPE_HALF), F32), sin, jnp.zeros((s, rest), F32)], axis=-1)
    tile = lambda t: jnp.tile(t, (1, 128 // ATT_DIM))
    return tile(c), tile(a), tile(b)


def _ret_rope(x, cos_t, sin_t):
    return x * cos_t + pltpu.roll(x, RET_DIM // 2, axis=1) * sin_t


def _att_rope(x, c, a, b):
    n = x.shape[-1]
    return x * c + pltpu.roll(x, n - ROPE_HALF, axis=1) * a + pltpu.roll(x, ROPE_HALF, axis=1) * b


def _head_norm(o, g):
    mu = jnp.mean(o, axis=-1, keepdims=True)
    oc = o - mu
    var = jnp.mean(oc * oc, axis=-1, keepdims=True)
    return oc * lax.rsqrt(var + EPS) * g


def _ret_kernel(zq_ref, zk_ref, zv_ref, zg_ref, gn_ref,
                ro_ref, so_ref, din_ref, dq_ref, dk_ref, *, chunk, per_step):
    c = pl.program_id(1)

    @pl.when(c == 0)
    def _():
        so_ref[...] = jnp.zeros_like(so_ref)
        row = lax.broadcasted_iota(jnp.int32, (chunk, chunk), 0).astype(F32)
        col = lax.broadcasted_iota(jnp.int32, (chunk, chunk), 1).astype(F32)
        diff = row - col
        rowd = lax.broadcasted_iota(jnp.int32, (chunk, RET_DIM), 0).astype(F32)
        for h in range(RET_HEADS):
            lg = LOG_GAMMA[h]
            din_ref[h] = jnp.where(diff >= 0, jnp.exp(jnp.maximum(diff, 0.0) * lg), 0.0)
            dq_ref[h] = jnp.exp((rowd + 1.0) * lg)
            dk_ref[h] = jnp.exp((chunk - 1.0 - rowd) * lg)

    for h in range(RET_HEADS):
        cols = slice(h * RET_DIM, (h + 1) * RET_DIM)
        s_cur = so_ref[h]
        for j in range(per_step):
            rows = slice(j * chunk, (j + 1) * chunk)
            k = zk_ref[rows, cols]
            qb = zq_ref[rows, cols].astype(BF16)
            kb = k.astype(BF16)
            vb = zv_ref[rows, cols].astype(BF16)
            a = _dot_nt(qb, kb) * din_ref[h]
            o = _dot(a.astype(BF16), vb) + _dot(qb, s_cur.astype(BF16)) * dq_ref[h]
            kd_t = (k * dk_ref[h]).T.astype(BF16)
            s_cur = math.exp(chunk * LOG_GAMMA[h]) * s_cur + _dot(kd_t, vb)

            on = _head_norm(o, gn_ref[:, cols])
            g = zg_ref[rows, cols]
            ro_ref[rows, cols] = (g * jax.nn.sigmoid(g) * on).astype(BF16)
        so_ref[h] = s_cur


def _retention_prompt(z3, ret_norm_g, layer, per_step=8):
    b, s, _ = z3.shape
    chunk = math.gcd(s, RET_CHUNK)
    per_step = math.gcd(s // chunk, per_step)
    rows = chunk * per_step
    zspec = lambda cb: pl.BlockSpec((None, rows, RET_WIDTH), lambda i, c: (i, c, cb))
    return pl.pallas_call(
        functools.partial(_ret_kernel, chunk=chunk, per_step=per_step),
        out_shape=(jax.ShapeDtypeStruct((b, s, RET_WIDTH), BF16),
                   jax.ShapeDtypeStruct((b, RET_HEADS, RET_DIM, RET_DIM), F32)),
        grid=(b, s // rows),
        in_specs=[
            zspec(COL_RQ // RET_WIDTH), zspec(COL_RK // RET_WIDTH),
            zspec(COL_RV // RET_WIDTH), zspec(COL_RG // RET_WIDTH),
            pl.BlockSpec((None, 1, RET_WIDTH), lambda i, c: (layer, 0, 0)),
        ],
        out_specs=(
            pl.BlockSpec((None, rows, RET_WIDTH), lambda i, c: (i, c, 0)),
            pl.BlockSpec((None, RET_HEADS, RET_DIM, RET_DIM), lambda i, c: (i, 0, 0, 0)),
        ),
        scratch_shapes=[
            pltpu.VMEM((RET_HEADS, chunk, chunk), F32),
            pltpu.VMEM((RET_HEADS, chunk, RET_DIM), F32),
            pltpu.VMEM((RET_HEADS, chunk, RET_DIM), F32),
        ],
        compiler_params=_params("parallel", "arbitrary"),
        name="retention",
    )(z3, z3, z3, z3, ret_norm_g)


def _ret_dec_kernel(z_ref, s0_ref, cos_ref, sin_ref, gn_ref, ro_ref, so_ref, *, nb):
    cos_t = cos_ref[...]
    sin_t = sin_ref[...]
    row = lax.broadcasted_iota(jnp.int32, (nb, RET_DIM), 0)
    for h in range(RET_HEADS):
        gamma = math.exp(LOG_GAMMA[h])
        q = _ret_rope(z_ref[:, COL_RQ + h * RET_DIM:COL_RQ + (h + 1) * RET_DIM], cos_t, sin_t)
        k = _ret_rope(z_ref[:, COL_RK + h * RET_DIM:COL_RK + (h + 1) * RET_DIM], cos_t, sin_t)
        k = k * (RET_DIM ** -0.5)
        v = z_ref[:, COL_RV + h * RET_DIM:COL_RV + (h + 1) * RET_DIM]
        qr, kr, vr = _bf16_round(q), _bf16_round(k), _bf16_round(v)
        qk = jnp.sum(qr * kr, axis=-1, keepdims=True)
        o = _bf16_round(qk) * vr
        cross = jnp.zeros((nb, RET_DIM), F32)
        for b in range(nb):
            s_old = s0_ref[b, h]
            res = _dot(qr.astype(BF16), s_old.astype(BF16))
            cross = cross + jnp.where(row == b, res, 0.0)
            k_col = jnp.broadcast_to(kr[b:b + 1, :], (RET_DIM, RET_DIM)).T
            so_ref[b, h] = gamma * s_old + k_col * vr[b:b + 1, :]
        o = o + cross * gamma
        cols = slice(h * RET_DIM, (h + 1) * RET_DIM)
        on = _head_norm(o, gn_ref[:, cols])
        g = z_ref[:, COL_RG + h * RET_DIM:COL_RG + (h + 1) * RET_DIM]
        ro_ref[:, cols] = g * jax.nn.sigmoid(g) * on


def _retention_decode(z, state, cos_t, sin_t, ret_norm_g, layer):
    nb = z.shape[0]
    sshape = (nb, RET_HEADS, RET_DIM, RET_DIM)
    return pl.pallas_call(
        functools.partial(_ret_dec_kernel, nb=nb),
        out_shape=(jax.ShapeDtypeStruct((nb, RET_WIDTH), F32),
                   jax.ShapeDtypeStruct(sshape, F32)),
        grid=(1,),
        in_specs=[
            pl.BlockSpec(z.shape, lambda i: (0, 0)),
            pl.BlockSpec((None,) + sshape, lambda i: (layer, 0, 0, 0, 0)),
            pl.BlockSpec((1, RET_DIM), lambda i: (0, 0)),
            pl.BlockSpec((1, RET_DIM), lambda i: (0, 0)),
            pl.BlockSpec((None, 1, RET_WIDTH), lambda i: (layer, 0, 0)),
        ],
        out_specs=(
            pl.BlockSpec((nb, RET_WIDTH), lambda i: (0, 0)),
            pl.BlockSpec(sshape, lambda i: (0, 0, 0, 0)),
        ),
        compiler_params=_params("arbitrary"),
        name="retention_decode",
    )(z, state, cos_t, sin_t, ret_norm_g)


ATT_HALF = 128
ATT_MAX_STRIDE = 4
N_HALF = ATT_GROUP // ATT_HALF
HEADS_PER_HALF = ATT_HALF // ATT_DIM


def _half_cols(hf):
    return slice(hf * ATT_HALF, (hf + 1) * ATT_HALF)


def _att_kernel(*refs, seq):
    zqkv = (refs[0:N_HALF], refs[N_HALF:2 * N_HALF], refs[2 * N_HALF:3 * N_HALF])
    ao_ref, o_ref, lse_ref, st_ref, ost_ref = refs[3 * N_HALF:]
    g = pl.program_id(1)
    qb = QUERY_BLOCK
    n_blocks = seq // qb

    tq = lax.broadcasted_iota(jnp.int32, (qb, qb), 0)
    tk = lax.broadcasted_iota(jnp.int32, (qb, qb), 1)
    cur_valid = tk <= tq
    prev_valid = tk >= tq
    lane = lax.broadcasted_iota(jnp.int32, (qb, ATT_HALF), 1)
    head_masks = [(lane // ATT_DIM) == hh for hh in range(HEADS_PER_HALF)]

    def group_body(gi, dil):
        has_prev = True
        d1 = min(dil, ATT_MAX_STRIDE)
        d2 = dil // d1
        staged = d2 > 1
        assert d2 <= ATT_MAX_STRIDE and d1 * d2 == dil
        sub_len = seq // d1

        def strided(start, n, stride):
            return pl.ds(start, n, stride=stride) if stride > 1 else pl.ds(start, n)

        if staged:
            for hf in range(N_HALF):
                for r1 in range(d1):
                    src = strided(r1, sub_len, d1)
                    for which in range(3):
                        st_ref[which, hf, r1] = zqkv[which][hf][src, :]

        def load(which, hf, cls, blk):
            if staged:
                rows = strided(cls // d1 + d2 * qb * blk, qb, d2)
                return st_ref[which, hf, cls % d1, rows, :].astype(BF16)
            rows = strided(cls + dil * qb * blk, qb, dil)
            return zqkv[which][hf][rows, :].astype(BF16)

        def store(which, hf, cls, blk, val):
            if staged:
                rows = strided(cls // d1 + d2 * qb * blk, qb, d2)
                ost_ref[which, hf, cls % d1, rows, :] = val
            else:
                rows = strided(cls + dil * qb * blk, qb, dil)
                (o_ref, lse_ref)[which][gi, hf, rows, :] = val

        def block_body(t, carry):
            cls = t % dil
            blk = t // dil
            valid = cur_valid
            if has_prev:
                pblk = jnp.maximum(blk - 1, 0)
                valid = jnp.concatenate([prev_valid & (blk > 0), cur_valid], axis=1)
            for hf in range(N_HALF):
                q = load(0, hf, cls, blk)
                keys = load(1, hf, cls, blk)
                vals = load(2, hf, cls, blk)
                if has_prev:
                    keys = jnp.concatenate([load(1, hf, cls, pblk), keys], axis=0)
                    vals = jnp.concatenate([load(2, hf, cls, pblk), vals], axis=0)
                vals_ones = jnp.concatenate([vals, jnp.ones_like(vals)], axis=1)
                num = jnp.zeros((qb, ATT_HALF), F32)
                den = jnp.ones((qb, ATT_HALF), F32)
                lse_acc = jnp.zeros((qb, ATT_HALF), F32)
                for hm in head_masks:
                    qh = jnp.where(hm, q, jnp.zeros_like(q))
                    s = jnp.where(valid, _dot_nt(qh, keys), MASK_VALUE)
                    m = jnp.max(s, axis=-1, keepdims=True)
                    e = jnp.exp(s - m).astype(BF16)
                    r = _dot(e, vals_ones)
                    l = r[:, ATT_HALF:]
                    num = jnp.where(hm, r[:, :ATT_HALF], num)
                    den = jnp.where(hm, l, den)
                    lse_acc = jnp.where(hm, m + jnp.log(l), lse_acc)
                store(0, hf, cls, blk, num / den)
                store(1, hf, cls, blk, lse_acc)
            return carry

        lax.fori_loop(0, n_blocks, block_body, 0, unroll=4)

        if staged:
            for hf in range(N_HALF):
                for r1 in range(d1):
                    dst = strided(r1, sub_len, d1)
                    o_ref[gi, hf, dst, :] = ost_ref[0, hf, r1]
                    lse_ref[gi, hf, dst, :] = ost_ref[1, hf, r1]

    for gi, (_, dil) in enumerate(ATT_PATTERNS):
        @pl.when(g == gi)
        def _(gi=gi, dil=dil):
            group_body(gi, dil)

    @pl.when(g == N_PAT - 1)
    def _():
        def merge_body(i, carry):
            r = pl.multiple_of(i * qb, qb)
            rows = pl.ds(r, qb)
            for hf in range(N_HALF):
                lses = [lse_ref[gi, hf, rows, :] for gi in range(N_PAT)]
                m = functools.reduce(jnp.maximum, lses)
                ws = [jnp.exp(l - m) for l in lses]
                den = functools.reduce(lambda x, y: x + y, ws)
                num = functools.reduce(lambda x, y: x + y,
                                       [w * o_ref[gi, hf, rows, :] for gi, w in enumerate(ws)])
                ao_ref[rows, _half_cols(hf)] = (num / den).astype(BF16)
            return carry

        lax.fori_loop(0, n_blocks, merge_body, 0)


def _attention_prompt(z3):
    b, s, _ = z3.shape
    assert all(s % (dil * QUERY_BLOCK) == 0 for _, dil in ATT_PATTERNS)
    zspec = lambda col, hf: pl.BlockSpec(
        (None, s, ATT_HALF), lambda i, g: (i, 0, col // ATT_HALF + N_HALF * g + hf))
    zspecs = [zspec(col, hf) for col in (COL_AQ, COL_AK, COL_AV) for hf in range(N_HALF)]
    return pl.pallas_call(
        functools.partial(_att_kernel, seq=s),
        out_shape=jax.ShapeDtypeStruct((b, s, ATT_GROUP), BF16),
        grid=(b, N_PAT),
        in_specs=zspecs,
        out_specs=pl.BlockSpec((None, s, ATT_GROUP), lambda i, g: (i, 0, 0)),
        scratch_shapes=[
            pltpu.VMEM((N_PAT, N_HALF, s, ATT_HALF), F32),
            pltpu.VMEM((N_PAT, N_HALF, s, ATT_HALF), F32),
            pltpu.VMEM((3, N_HALF, ATT_MAX_STRIDE, s // ATT_MAX_STRIDE, ATT_HALF), F32),
            pltpu.VMEM((2, N_HALF, ATT_MAX_STRIDE, s // ATT_MAX_STRIDE, ATT_HALF), F32),
        ],
        compiler_params=_params("parallel", "arbitrary"),
        name="dilated_attention",
    )(*([z3] * (3 * N_HALF)))


def _cache_shift_kernel(*refs):
    n = len(refs) // 2
    for src, dst in zip(refs[:n], refs[n:]):
        h, dim, w = src.shape
        dst[...] = pltpu.roll(src[...].reshape(h * dim, w), w - 1, axis=1).reshape(h, dim, w)


def _cache_shift(caches):
    depth, nb = caches[0].shape[:2]
    specs = [pl.BlockSpec((None, None) + c.shape[2:], lambda l, i: (l, i, 0, 0, 0)) for c in caches]
    return pl.pallas_call(
        _cache_shift_kernel,
        out_shape=tuple(jax.ShapeDtypeStruct(c.shape, c.dtype) for c in caches),
        grid=(depth, nb),
        in_specs=specs,
        out_specs=tuple(specs),
        compiler_params=_params("arbitrary", "arbitrary"),
        name="cache_shift",
    )(*caches)


def _row_to_col(row, eye):
    return jnp.sum(jnp.where(eye, row, 0.0), axis=1, keepdims=True)


def _col_to_row(col, eye):
    return jnp.sum(jnp.where(eye, col, 0.0), axis=0, keepdims=True)


def _att_dec_kernel(z_ref, c_ref, a_ref, b_ref, *refs):
    kc_refs = refs[0:N_PAT]
    vc_refs = refs[N_PAT:2 * N_PAT]
    kt_refs = refs[2 * N_PAT:3 * N_PAT]
    vt_refs = refs[3 * N_PAT:4 * N_PAT]
    ao_ref = refs[4 * N_PAT]
    kt_out_refs = refs[4 * N_PAT + 1:5 * N_PAT + 1]
    vt_out_refs = refs[5 * N_PAT + 1:6 * N_PAT + 1]
    rope =(c_ref[...], a_ref[...], b_ref[...])
    eye = (lax.broadcasted_iota(jnp.int32, (ATT_DIM, ATT_DIM), 0)
           == lax.broadcasted_iota(jnp.int32, (ATT_DIM, ATT_DIM), 1))
    last_lane = lax.broadcasted_iota(jnp.int32, (ATT_DIM, ATT_HALF), 1) == ATT_HALF - 1

    head_rows = []
    for h in range(ATT_HEADS):
        hf, lanes = h // HEADS_PER_HALF, slice((h % HEADS_PER_HALF) * ATT_DIM,
                                               (h % HEADS_PER_HALF + 1) * ATT_DIM)
        outs, lses = [], []
        for gi, (win, dil) in enumerate(ATT_PATTERNS):
            off = gi * ATT_GROUP + hf * ATT_HALF
            zrow = lambda col: z_ref[:, col + off:col + off + ATT_HALF]
            q = (_att_rope(zrow(COL_AQ), *rope) * (ATT_DIM ** -0.5))[:, lanes]
            k_new = _att_rope(zrow(COL_AK), *rope)[:, lanes]
            v_new = zrow(COL_AV)[:, lanes]
            k_col, v_col = _row_to_col(k_new, eye), _row_to_col(v_new, eye)
            keys, vals = kc_refs[gi][h], vc_refs[gi][h]
            w = keys.shape[1]
            pos = lax.broadcasted_iota(jnp.int32, (1, w), 1)
            s_old = jnp.sum(keys * _row_to_col(q, eye), axis=0, keepdims=True)
            s_old = jnp.where(pos % dil == 0, s_old, MASK_VALUE)
            s_new = jnp.sum(q * k_new, axis=1, keepdims=True)
            m = jnp.maximum(jnp.max(s_old, axis=1, keepdims=True), s_new)
            e_old = jnp.exp(s_old - m)
            e_new = jnp.exp(s_new - m)
            l = jnp.sum(e_old, axis=1, keepdims=True) + e_new
            pv = jnp.sum(vals * e_old, axis=1, keepdims=True) + e_new * v_col
            outs.append(pv / l)
            lses.append(m + jnp.log(l))
            kt_out_refs[gi][h] = jnp.where(last_lane, k_col, kt_refs[gi][h])
            vt_out_refs[gi][h] = jnp.where(last_lane, v_col, vt_refs[gi][h])

        m = functools.reduce(jnp.maximum, lses)
        ws = [jnp.exp(l - m) for l in lses]
        den = functools.reduce(lambda x, y: x + y, ws)
        num = functools.reduce(lambda x, y: x + y, [w * o for w, o in zip(ws, outs)])
        head_rows.append(_col_to_row(num / den, eye))
    ao_ref[...] = jnp.concatenate(head_rows, axis=1)


def _attention_decode(z, tabs, k_caches, v_caches, k_next, v_next, layer):
    nb = z.shape[0]
    n_fixed = 4
    cspec = lambda c: pl.BlockSpec((None, None) + c.shape[2:], lambda i: (layer, i, 0, 0, 0))
    tail = lambda c: pl.BlockSpec((None, None) + c.shape[2:4] + (ATT_HALF,),
                                  lambda i: (layer, i, 0, 0, c.shape[4] // ATT_HALF - 1))
    tspec = pl.BlockSpec((1, ATT_HALF), lambda i: (0, 0))
    caches = list(k_caches) + list(v_caches)
    nexts = list(k_next) + list(v_next)
    return pl.pallas_call(
        _att_dec_kernel,
        out_shape=tuple([jax.ShapeDtypeStruct((nb, 1, ATT_GROUP), F32)]
                        + [jax.ShapeDtypeStruct(c.shape, c.dtype) for c in nexts]),
        grid=(nb,),
        in_specs=[pl.BlockSpec((None, 1, z.shape[1]), lambda i: (i, 0, 0)), tspec, tspec, tspec]
                 + [cspec(c) for c in caches] + [tail(c) for c in nexts],
        out_specs=tuple([pl.BlockSpec((None, 1, ATT_GROUP), lambda i: (i, 0, 0))]
                        + [tail(c) for c in nexts]),
        input_output_aliases={n_fixed + len(caches) + j: 1 + j for j in range(len(nexts))},
        compiler_params=_params("arbitrary"),
        name="dilated_attention_decode",
    )(z.reshape(nb, 1, z.shape[1]), *tabs, *caches, *nexts)


def _pool_kernel(u_ref, w_ref, sc_ref, po_ref, a_ref, b_ref, *, seq):
    g = pl.program_id(1)
    body = pl.ds(POOL_PAD, seq)

    def window_mean_minus_token(win):
        x = u_ref[...]
        a_ref[0:POOL_PAD, :] = jnp.zeros((POOL_PAD, POOL_GROUP), F32)
        b_ref[0:POOL_PAD, :] = jnp.zeros((POOL_PAD, POOL_GROUP), F32)
        a_ref[body, :] = x
        src, dst = a_ref, b_ref
        k = 1
        while k < win:
            dst[body, :] = src[body, :] + src[pl.ds(POOL_PAD - k, seq), :]
            src, dst = dst, src
            k *= 2
        t = lax.broadcasted_iota(jnp.int32, (seq, POOL_GROUP), 0)
        cnt = jnp.minimum(t + 1, win).astype(F32)
        pooled = src[body, :] / cnt - x
        y = _dot(pooled.astype(BF16), w_ref[...].astype(BF16)) * sc_ref[...]
        po_ref[...] = y.astype(BF16)

    for gi, win in enumerate(POOL_WINDOWS):
        @pl.when(g == gi)
        def _(win=win):
            window_mean_minus_token(win)


def _pool_prompt(z3, w_pool, pool_scale, layer):
    b, s, _ = z3.shape
    ng = len(POOL_WINDOWS)
    return pl.pallas_call(
        functools.partial(_pool_kernel, seq=s),
        out_shape=jax.ShapeDtypeStruct((b, s, POOL_WIDTH), BF16),
        grid=(b, ng),
        in_specs=[
            pl.BlockSpec((None, s, POOL_GROUP), lambda i, g: (i, 0, COL_PU // POOL_GROUP + g)),
            pl.BlockSpec((None, None, POOL_GROUP, POOL_GROUP), lambda i, g: (layer, g, 0, 0)),
            pl.BlockSpec((None, 1, POOL_GROUP), lambda i, g: (layer, 0, g)),
        ],
        out_specs=pl.BlockSpec((None, s, POOL_GROUP), lambda i, g: (i, 0, g)),
        scratch_shapes=[pltpu.VMEM((POOL_PAD + s, POOL_GROUP), F32),
                        pltpu.VMEM((POOL_PAD + s, POOL_GROUP), F32)],
        compiler_params=_params("parallel", "arbitrary"),
        name="pool_mixer",
    )(z3, w_pool, pool_scale)


def _pool_dec_kernel(z_ref, buf_ref, w_ref, sc_ref, po_ref, bo_ref, pooled_ref, *, nb):
    row = lax.broadcasted_iota(jnp.int32, (POOL_BUF, POOL_GROUP), 0)
    for b in range(nb):
        u = z_ref[b:b + 1, COL_PU:COL_PU + POOL_WIDTH]
        old = buf_ref[b]
        for gi, win in enumerate(POOL_WINDOWS):
            cols = slice(gi * POOL_GROUP, (gi + 1) * POOL_GROUP)
            tail = jnp.where(row >= POOL_BUF - (win - 1), old[:, cols], 0.0)
            total = jnp.sum(tail, axis=0, keepdims=True) + u[:, cols]
            pooled_ref[b:b + 1, cols] = total / float(win) - u[:, cols]
        bo_ref[b, 0:POOL_BUF - 1, :] = old[1:POOL_BUF, :]
        bo_ref[b, POOL_BUF - 1:POOL_BUF, :] = u
    for gi in range(len(POOL_WINDOWS)):
        cols = slice(gi * POOL_GROUP, (gi + 1) * POOL_GROUP)
        y = _dot(pooled_ref[:, cols].astype(BF16), w_ref[gi].astype(BF16))
        po_ref[:, cols] = y * sc_ref[:, cols]


def _pool_decode(z, cache_pool, w_pool, pool_scale, layer):
    nb = z.shape[0]
    ng = len(POOL_WINDOWS)
    bshape = (nb, POOL_BUF, POOL_WIDTH)
    return pl.pallas_call(
        functools.partial(_pool_dec_kernel, nb=nb),
        out_shape=(jax.ShapeDtypeStruct((nb, POOL_WIDTH), F32),
                   jax.ShapeDtypeStruct(bshape, F32)),
        grid=(1,),
        in_specs=[
            pl.BlockSpec(z.shape, lambda i: (0, 0)),
            pl.BlockSpec((None,) + bshape, lambda i: (layer, 0, 0, 0)),
            pl.BlockSpec((None, ng, POOL_GROUP, POOL_GROUP), lambda i: (layer, 0, 0, 0)),
            pl.BlockSpec((None, 1, POOL_WIDTH), lambda i: (layer, 0, 0)),
        ],
        out_specs=(pl.BlockSpec((nb, POOL_WIDTH), lambda i: (0, 0)),
                   pl.BlockSpec(bshape, lambda i: (0, 0, 0))),
        scratch_shapes=[pltpu.VMEM((nb, POOL_WIDTH), F32)],
        compiler_params=_params("arbitrary"),
        name="pool_mixer_decode",
    )(z, cache_pool, w_pool, pool_scale)


def _trunk_prompt(x, mod, p, wb, final_g):
    b, s, d = x.shape
    depth = p["w_in"].shape[0]
    pos = jnp.arange(s, dtype=F32)
    ret_tabs = _ret_rope_tables(pos)
    att_tabs = _att_rope_tables(pos)
    keep = tuple(min(win, s) for win, _ in ATT_PATTERNS)
    tm = math.gcd(s, 1024)
    xf = x.reshape(b * s, d)
    rets, pools = [], []
    ks = [[] for _ in ATT_PATTERNS]
    vs = [[] for _ in ATT_PATTERNS]
    for l in range(depth):
        xf = _ffn(xf, mod, l, 0, s, p["norm_g"], *wb[l]["ffn1"], None, tm, 512)
        z = _inproj(xf, mod, l, s, p["norm_g"], wb[l]["w_in"], math.gcd(s, 256), IN_WIDTH,
                    rope_tabs=ret_tabs + att_tabs)
        z3 = z.reshape(b, s, IN_WIDTH)
        ro, ret_s = _retention_prompt(z3, p["ret_norm_g"], l)
        ao = _attention_prompt(z3)
        po = _pool_prompt(z3, p["w_pool"], p["pool_scale"], l)
        xf = _outproj(xf, mod, l, s, ro.reshape(b * s, -1), ao.reshape(b * s, -1),
                      po.reshape(b * s, -1), p["w_out"], math.gcd(s, 512))
        xf = _ffn(xf, mod, l, 2, s, p["norm_g"], *wb[l]["ffn2"],
                  final_g if l == depth - 1 else None, tm, 512)
        rets.append(ret_s)
        pools.append(z3[:, s - POOL_BUF:, COL_PU:])
        for g in range(N_PAT):
            for col, dst in ((COL_AK, ks), (COL_AV, vs)):
                c0 = col + g * ATT_GROUP
                dst[g].append(z3[:, s - keep[g]:, c0:c0 + ATT_GROUP]
                              .reshape(b, keep[g], ATT_HEADS, ATT_DIM))
    y = xf.reshape(b, s, d)
    return (y, jnp.stack(rets), [jnp.stack(k) for k in ks], [jnp.stack(v) for v in vs],
            jnp.stack(pools))


def _trunk_decode(x, mod, pos0, caches, p, final_g):
    nb, s, d = x.shape
    assert s == 1, "decode trunk handles one new token per batch row"
    depth = p["w_in"].shape[0]
    state_ret, cks, cvs, cpool = caches
    pos = pos0 + jnp.arange(s, dtype=F32)
    ret_tabs = _ret_rope_tables(pos)
    att_tabs = _att_rope_tables(pos)
    feature_major = lambda c: jnp.transpose(c, (0, 1, 3, 4, 2))
    position_major = lambda c: jnp.transpose(c, (0, 1, 4, 2, 3))
    cks = [feature_major(c) for c in cks]
    cvs = [feature_major(c) for c in cvs]
    advanced = _cache_shift(cks + cvs)
    k_next, v_next = list(advanced[:N_PAT]), list(advanced[N_PAT:])
    xf = x.reshape(nb, d)
    rets, pools, wb = [], [], []
    for l in range(depth):
        xf, *ffn1_bf16 = _ffn(xf, mod, l, 0, 1, p["norm_g"], p["w1_gate"], p["w1_up"], p["w1_down"],
                              None, nb, 512, emit_bf16=True)
        z, w_in_bf16 = _inproj_rows(xf, mod, l, p["norm_g"], p["w_in"])
        ro, ret_s = _retention_decode(z, state_ret, *ret_tabs, p["ret_norm_g"], l)
        att = _attention_decode(z, att_tabs, cks, cvs, k_next, v_next, l)
        ao, k_next, v_next = att[0].reshape(nb, ATT_GROUP), list(att[1:1 + N_PAT]), list(att[1 + N_PAT:])
        po, pool_new = _pool_decode(z, cpool, p["w_pool"], p["pool_scale"], l)
        xf = _outproj(xf, mod, l, 1, ro, ao, po, p["w_out"], nb)
        xf, *ffn2_bf16 = _ffn(xf, mod, l, 2, 1, p["norm_g"], p["w2_gate"], p["w2_up"], p["w2_down"],
                              final_g if l == depth - 1 else None, nb, 512, emit_bf16=True)
        wb.append({"ffn1": ffn1_bf16, "w_in": w_in_bf16, "ffn2": ffn2_bf16})
        rets.append(ret_s)
        pools.append(pool_new)
    y = xf.reshape(nb, s, d)
    return (y, jnp.stack(rets), [position_major(k) for k in k_next],
            [position_major(v) for v in v_next], jnp.stack(pools), wb)


def kernel(x_prompt, x_sample, state_ret, cache_k_w128, cache_v_w128, cache_k_w512, cache_v_w512,
           cache_k_w2048, cache_v_w2048, cache_pool, c_prompt, c_sample, w_ada, b_ada, norm_g, w_in,
           ret_norm_g, w_pool, pool_scale, w_out, w1_gate, w1_up, w1_down, w2_gate, w2_up, w2_down,
           final_norm_g):
    depth, d = norm_g.shape[0], norm_g.shape[-1]
    n_pr, n_dec = c_prompt.shape[0], c_sample.shape[0]
    p = {
        "norm_g": norm_g.reshape(depth, N_SUB, 1, d),
        "w_in": w_in,
        "ret_norm_g": ret_norm_g.reshape(depth, 1, RET_WIDTH),
        "w_pool": w_pool,
        "pool_scale": pool_scale.reshape(depth, 1, POOL_WIDTH),
        "w_out": w_out,
        "w1_gate": w1_gate, "w1_up": w1_up, "w1_down": w1_down,
        "w2_gate": w2_gate, "w2_up": w2_up, "w2_down": w2_down,
    }
    final_g = final_norm_g.reshape(1, d)

    pad = (-(n_dec + n_pr)) % SUBLANES
    c_all = jnp.concatenate([c_sample, c_prompt, jnp.zeros((pad, d), F32)], axis=0)
    mod_dec, mod_pr = _ada(c_all, n_dec, n_pr, w_ada, b_ada)

    caches = (state_ret,
              (cache_k_w128, cache_k_w512, cache_k_w2048),
              (cache_v_w128, cache_v_w512, cache_v_w2048),
              cache_pool)
    y_s, ret_s, ks, vs, pool_s, wb = _trunk_decode(x_sample, mod_dec, float(PAST_LEN), caches, p, final_g)
    y_p, ret_p, kp, vp, pool_p = _trunk_prompt(x_prompt, mod_pr, p, wb, final_g)
    return (y_p, y_s, ret_p, ret_s,
            kp[0], ks[0], vp[0], vs[0],
            kp[1], ks[1], vp[1], vs[1],
            kp[2], ks[2], vp[2], vs[2],
            pool_p, pool_s)
```

```python
import functools
import math

import jax
import jax.numpy as jnp
from jax import lax
from jax.experimental import pallas as pl
from jax.experimental.pallas import tpu as pltpu

F32 = jnp.float32
BF16 = jnp.bfloat16

RET_HEADS = 6
RET_DIM = 128
RET_WIDTH = RET_HEADS * RET_DIM
RET_CHUNK = 128
RET_THETA = 10000.0
ATT_HEADS = 4
ATT_DIM = 64
ATT_GROUP = ATT_HEADS * ATT_DIM
ATT_PATTERNS = ((128, 1), (512, 4), (2048, 16))
N_PAT = len(ATT_PATTERNS)
ATT_WIDTH = N_PAT * ATT_GROUP
ROPE_THETA = 500000.0
ROPE_DIMS = ATT_DIM // 4
ROPE_HALF = ROPE_DIMS // 2
QUERY_BLOCK = 128
POOL_WINDOWS = (2, 4, 8, 16)
POOL_GROUP = 128
POOL_WIDTH = len(POOL_WINDOWS) * POOL_GROUP
POOL_BUF = max(POOL_WINDOWS) - 1
POOL_PAD = 16
N_SUB = 3
PAST_LEN = 16384
HALF_STEP = 0.5
EPS = 1e-6
MASK_VALUE = -1e30

COL_RQ, COL_RK, COL_RV, COL_RG = 0, RET_WIDTH, 2 * RET_WIDTH, 3 * RET_WIDTH
COL_AQ = 4 * RET_WIDTH
COL_AK = COL_AQ + ATT_WIDTH
COL_AV = COL_AK + ATT_WIDTH
COL_PU = COL_AV + ATT_WIDTH
IN_WIDTH = COL_PU + POOL_WIDTH

V7X_VMEM_BYTES = 64 * 1024 * 1024
VMEM_LIMIT = V7X_VMEM_BYTES - 8 * 1024 * 1024
SUBLANES = 8
LANES = 128
ROW_CHUNK = 128
RESULT_COLS = 512

LOG_GAMMA = tuple(math.log1p(-(2.0 ** (-5.0 - h))) for h in range(RET_HEADS))


def _params(*sem):
    return pltpu.CompilerParams(dimension_semantics=sem, vmem_limit_bytes=VMEM_LIMIT)


def _dot(a, b):
    return jnp.dot(a, b, preferred_element_type=F32)


def _dot_nt(a, b):
    return lax.dot_general(a, b, (((1,), (1,)), ((), ())), preferred_element_type=F32)


def _bf16_round(x):
    return x.astype(BF16).astype(F32)


def _ada_kernel(c_ref, w_ref, b_ref, od_ref, op_ref, *, n_dec, n_pr):
    c = c_ref[...]
    a = (c * jax.nn.sigmoid(c)).astype(BF16)
    res = _dot(a, w_ref[...].astype(BF16)) + b_ref[...]
    od_ref[...] = res[0:n_dec]
    for b in range(n_pr):
        op_ref[b] = res[n_dec + b:n_dec + b + 1]


def _ada(c_all, n_dec, n_pr, w_ada, b_ada, tn=1024):
    depth, d, n = w_ada.shape
    rows = c_all.shape[0]
    per = d // tn
    return pl.pallas_call(
        functools.partial(_ada_kernel, n_dec=n_dec, n_pr=n_pr),
        out_shape=(jax.ShapeDtypeStruct((depth, N_SUB * 3, n_dec, d), F32),
                   jax.ShapeDtypeStruct((depth, N_SUB * 3, n_pr, 1, d), F32)),
        grid=(depth, n // tn),
        in_specs=[
            pl.BlockSpec((rows, d), lambda l, j: (0, 0)),
            pl.BlockSpec((None, d, tn), lambda l, j: (l, 0, j)),
            pl.BlockSpec((None, 1, tn), lambda l, j: (l, 0, j)),
        ],
        out_specs=(
            pl.BlockSpec((None, None, n_dec, tn), lambda l, j: (l, j // per, 0, j % per)),
            pl.BlockSpec((None, None, n_pr, 1, tn), lambda l, j: (l, j // per, 0, 0, j % per)),
        ),
        compiler_params=_params("arbitrary", "arbitrary"),
        name="ada_mod",
    )(c_all, w_ada, b_ada.reshape(depth, 1, n))


def _rmsnorm(x, g):
    ms = jnp.mean(x * x, axis=-1, keepdims=True)
    return x * lax.rsqrt(ms + EPS) * g


def _row_chunks(tm):
    rc = min(tm, ROW_CHUNK)
    return rc, tm // rc


def _mod_rows(ref, r, rc):
    return ref[...] if ref.shape[0] == 1 else ref[pl.ds(r, rc), :]


def _prenorm_to(h_ref, x_ref, g_ref, sh_ref, sc_ref, unrolled=False):
    rc, n = _row_chunks(x_ref.shape[0])

    def body(i, carry):
        r = i * rc if unrolled else pl.multiple_of(i * rc, rc)
        y = _rmsnorm(x_ref[pl.ds(r, rc), :], g_ref[...])
        h = y * (1.0 + _mod_rows(sc_ref, r, rc)) + _mod_rows(sh_ref, r, rc)
        h_ref[pl.ds(r, rc), :] = h.astype(BF16)
        return carry

    if unrolled:
        for i in range(n):
            body(i, 0)
    else:
        lax.fori_loop(0, n, body, 0)


def _mod_specs(mod, layer, sub, rows_per_batch, tm, grid_rank):
    d = mod.shape[-1]
    specs = []
    for k in range(3):
        j = sub * 3 + k
        if mod.ndim == 5:
            if grid_rank == 2:
                idx = (lambda j: lambda i, f: (layer, j, (i * tm) // rows_per_batch, 0, 0))(j)
            else:
                idx = (lambda j: lambda i: (layer, j, (i * tm) // rows_per_batch, 0, 0))(j)
            specs.append(pl.BlockSpec((None, None, None, 1, d), idx))
        else:
            if grid_rank == 2:
                idx = (lambda j: lambda i, f: (layer, j, 0, 0))(j)
            else:
                idx = (lambda j: lambda i: (layer, j, 0, 0))(j)
            specs.append(pl.BlockSpec((None, None, tm, d), idx))
    return specs


def _ffn_kernel(x_ref, sh_ref, sc_ref, gt_ref, g_ref, wg_ref, wu_ref, wd_ref, *rest,
                n_f, final_norm, emit_bf16):
    rest = list(rest)
    fg_ref = rest.pop(0) if final_norm else None
    o_ref = rest.pop(0)
    wb_refs = [rest.pop(0) for _ in range(3)] if emit_bf16 else None
    h_ref, = rest
    f = pl.program_id(1)

    assert n_f >= 2
    fuse_residual = not final_norm

    def step(first, last):
        if first:
            _prenorm_to(h_ref, x_ref, g_ref, sh_ref, sc_ref, unrolled=True)
        wg = wg_ref[...].astype(BF16)
        wu = wu_ref[...].astype(BF16)
        wd = wd_ref[...].astype(BF16)
        if emit_bf16:
            wb_refs[0][...] = wg
            wb_refs[1][...] = wu
            wb_refs[2][...] = wd
        h = h_ref[...]
        gate = _dot(h, wg)
        up = _dot(h, wu)
        act = (gate * jax.nn.sigmoid(gate) * up).astype(BF16)
        d = o_ref.shape[1]
        dc = min(d, RESULT_COLS)
        for c0 in range(0, d, dc):
            cols = slice(c0, c0 + dc)
            y = _dot(act, wd[:, cols])
            if first:
                o_ref[:, cols] = y
            elif last and fuse_residual:
                o_ref[:, cols] = x_ref[:, cols] + HALF_STEP * gt_ref[:, cols] * (o_ref[:, cols] + y)
            else:
                o_ref[:, cols] += y

    pl.when(f == 0)(lambda: step(True, False))
    pl.when((f > 0) & (f < n_f - 1))(lambda: step(False, False))
    pl.when(f == n_f - 1)(lambda: step(False, True))

    if not fuse_residual:
        @pl.when(f == n_f - 1)
        def _():
            rc, n = _row_chunks(x_ref.shape[0])

            def body(i, carry):
                r = pl.multiple_of(i * rc, rc)
                rows = pl.ds(r, rc)
                out = x_ref[rows, :] + HALF_STEP * _mod_rows(gt_ref, r, rc) * o_ref[rows, :]
                o_ref[rows, :] = _rmsnorm(out, fg_ref[...])
                return carry

            lax.fori_loop(0, n, body, 0)


def _weight_spec(w, layer, block, index):
    if w.ndim == 3:
        return pl.BlockSpec((None,) + block, lambda i, j: (layer,) + index(i, j))
    return pl.BlockSpec(block, index)


def _ffn(x, mod, layer, sub, rows_per_batch, norm_g, wg, wu, wd, final_g, tm, tf, emit_bf16=False):
    m, d = x.shape
    d_ff = wg.shape[-1]
    assert m % tm == 0 and d_ff % tf == 0
    assert not emit_bf16 or m == tm, "each weight block must be visited exactly once"
    n_f = d_ff // tf
    final_norm = final_g is not None
    col_block = lambda i, j: (0, j)
    row_block = lambda i, j: (j, 0)
    in_specs = [pl.BlockSpec((tm, d), lambda i, j: (i, 0))]
    in_specs += _mod_specs(mod, layer, sub, rows_per_batch, tm, 2)
    in_specs += [
        pl.BlockSpec((None, None, 1, d), lambda i, j: (layer, sub, 0, 0)),
        _weight_spec(wg, layer, (d, tf), col_block),
        _weight_spec(wu, layer, (d, tf), col_block),
        _weight_spec(wd, layer, (tf, d), row_block),
    ]
    args = [x, mod, mod, mod, norm_g, wg, wu, wd]
    if final_norm:
        in_specs.append(pl.BlockSpec((1, d), lambda i, j: (0, 0)))
        args.append(final_g)
    out_shape = [jax.ShapeDtypeStruct((m, d), F32)]
    out_specs = [pl.BlockSpec((tm, d), lambda i, j: (i, 0))]
    if emit_bf16:
        out_shape += [jax.ShapeDtypeStruct((d, d_ff), BF16), jax.ShapeDtypeStruct((d, d_ff), BF16),
                      jax.ShapeDtypeStruct((d_ff, d), BF16)]
        out_specs += [pl.BlockSpec((d, tf), col_block), pl.BlockSpec((d, tf), col_block),
                      pl.BlockSpec((tf, d), row_block)]
    out = pl.pallas_call(
        functools.partial(_ffn_kernel, n_f=n_f, final_norm=final_norm, emit_bf16=emit_bf16),
        out_shape=tuple(out_shape),
        grid=(m // tm, n_f),
        in_specs=in_specs,
        out_specs=tuple(out_specs),
        scratch_shapes=[pltpu.VMEM((tm, d), BF16)],
        compiler_params=_params("parallel", "arbitrary"),
        name="ffn",
    )(*args)
    return out if emit_bf16 else out[0]


def _rotate_projection_block(y, col, ret_tabs, att_tabs):
    if col < COL_RV:
        y = _ret_rope(y, *ret_tabs)
        return y * (RET_DIM ** -0.5) if col >= COL_RK else y
    if COL_AQ <= col < COL_AV:
        y = _att_rope(y, *att_tabs)
        return y * (ATT_DIM ** -0.5) if col < COL_AK else y
    return y


def _inproj_kernel(x_ref, sh_ref, sc_ref, g_ref, w_ref, rc_ref, rs_ref, ac_ref, aa_ref, ab_ref,
                   o_ref, h_ref):
    _prenorm_to(h_ref, x_ref, g_ref, sh_ref, sc_ref, unrolled=True)
    h = h_ref[...]
    ret_tabs = (rc_ref[...], rs_ref[...])
    att_tabs = (ac_ref[...], aa_ref[...], ab_ref[...])
    n = o_ref.shape[1]
    for c0 in range(0, n, RESULT_COLS):
        c1 = min(c0 + RESULT_COLS, n)
        y = _dot(h, w_ref[:, c0:c1])
        for b0 in range(0, c1 - c0, LANES):
            o_ref[:, c0 + b0:c0 + b0 + LANES] = _rotate_projection_block(
                y[:, b0:b0 + LANES], c0 + b0, ret_tabs, att_tabs)


def _inproj(x, mod, layer, rows_per_batch, norm_g, w_in_bf16, rope_tabs, tm):
    m, d = x.shape
    n = w_in_bf16.shape[-1]
    assert m % tm == 0 and rows_per_batch % tm == 0 and n % LANES == 0
    sh, sc, _ = _mod_specs(mod, layer, 1, rows_per_batch, tm, 1)
    tiles_per_batch = rows_per_batch // tm
    return pl.pallas_call(
        _inproj_kernel,
        out_shape=jax.ShapeDtypeStruct((m, n), F32),
        grid=(m // tm,),
        in_specs=[
            pl.BlockSpec((tm, d), lambda i: (i, 0)),
            sh, sc,
            pl.BlockSpec((None, None, 1, d), lambda i: (layer, 1, 0, 0)),
            pl.BlockSpec((d, n), lambda i: (0, 0), pipeline_mode=pl.Buffered(1)),
        ] + [pl.BlockSpec((tm, LANES), lambda i: (i % tiles_per_batch, 0))] * len(rope_tabs),
        out_specs=pl.BlockSpec((tm, n), lambda i: (i, 0)),
        scratch_shapes=[pltpu.VMEM((tm, d), BF16)],
        compiler_params=_params("parallel"),
        name="in_proj",
    )(x, mod, mod, norm_g, w_in_bf16, *rope_tabs)


def _inproj_rows_kernel(x_ref, sh_ref, sc_ref, g_ref, w_ref, o_ref, wb_ref, h_ref, *, tk):
    k = pl.program_id(0)

    @pl.when(k == 0)
    def _():
        h = _rmsnorm(x_ref[...], g_ref[...]) * (1.0 + sc_ref[...]) + sh_ref[...]
        for c in range(h_ref.shape[0]):
            h_ref[c] = h[:, c * tk:(c + 1) * tk].astype(BF16)
        o_ref[...] = jnp.zeros_like(o_ref)

    wb = w_ref[...].astype(BF16)
    wb_ref[...] = wb
    o_ref[...] += _dot(h_ref[k], wb)


def _inproj_rows(x, mod, layer, norm_g, w_in, tk=512):
    m, d = x.shape
    n = w_in.shape[-1]
    assert d % tk == 0
    sh, sc, _ = _mod_specs(mod, layer, 1, 1, m, 1)
    return pl.pallas_call(
        functools.partial(_inproj_rows_kernel, tk=tk),
        out_shape=(jax.ShapeDtypeStruct((m, n), F32), jax.ShapeDtypeStruct((d, n), BF16)),
        grid=(d // tk,),
        in_specs=[
            pl.BlockSpec((m, d), lambda k: (0, 0)),
            sh, sc,
            pl.BlockSpec((None, None, 1, d), lambda k: (layer, 1, 0, 0)),
            pl.BlockSpec((None, tk, n), lambda k: (layer, k, 0)),
        ],
        out_specs=(pl.BlockSpec((m, n), lambda k: (0, 0)), pl.BlockSpec((tk, n), lambda k: (k, 0))),
        scratch_shapes=[pltpu.VMEM((d // tk, m, tk), BF16)],
        compiler_params=_params("arbitrary"),
        name="in_proj_rows",
    )(x, mod, mod, norm_g, w_in)


def _outproj_kernel(x_ref, gt_ref, ro_ref, ao_ref, po_ref, w_ref, o_ref, wb_ref):
    @pl.when(pl.program_id(0) == 0)
    def _():
        wb_ref[...] = w_ref[...].astype(BF16)

    mix = jnp.concatenate([ro_ref[...].astype(BF16), ao_ref[...].astype(BF16),
                           po_ref[...].astype(BF16)], axis=1)
    o_ref[...] = x_ref[...] + gt_ref[...] * _dot(mix, wb_ref[...])


def _outproj(x, mod, layer, rows_per_batch, ro, ao, po, w_out, tm):
    m, d = x.shape
    k = w_out.shape[1]
    _, _, gt = _mod_specs(mod, layer, 1, rows_per_batch, tm, 1)
    return pl.pallas_call(
        _outproj_kernel,
        out_shape=jax.ShapeDtypeStruct((m, d), F32),
        grid=(m // tm,),
        in_specs=[
            pl.BlockSpec((tm, d), lambda i: (i, 0)),
            gt,
            pl.BlockSpec((tm, RET_WIDTH), lambda i: (i, 0)),
            pl.BlockSpec((tm, ATT_GROUP), lambda i: (i, 0)),
            pl.BlockSpec((tm, POOL_WIDTH), lambda i: (i, 0)),
            pl.BlockSpec((None, k, d), lambda i: (layer, 0, 0), pipeline_mode=pl.Buffered(1)),
        ],
        out_specs=pl.BlockSpec((tm, d), lambda i: (i, 0)),
        scratch_shapes=[pltpu.VMEM((k, d), BF16)],
        compiler_params=_params("arbitrary"),
        name="out_proj",
    )(x, mod, ro, ao, po, w_out)


def _ret_rope_tables(pos):
    half = RET_DIM // 2
    freq = jnp.power(jnp.float32(RET_THETA), -jnp.arange(half, dtype=F32) / half)
    ang = pos[:, None] * freq[None, :]
    cos, sin = jnp.cos(ang), jnp.sin(ang)
    return jnp.concatenate([cos, cos], axis=-1), jnp.concatenate([-sin, sin], axis=-1)


def _att_rope_tables(pos):
    freq = jnp.power(jnp.float32(ROPE_THETA), -jnp.arange(ROPE_HALF, dtype=F32) / ROPE_HALF)
    ang = pos[:, None] * freq[None, :]
    cos, sin = jnp.cos(ang), jnp.sin(ang)
    s = pos.shape[0]
    rest = ATT_DIM - ROPE_DIMS
    c = jnp.concatenate([cos, cos, jnp.ones((s, rest), F32)], axis=-1)
    a = jnp.concatenate([-sin, jnp.zeros((s, ATT_DIM - ROPE_HALF), F32)], axis=-1)
    b = jnp.concatenate([jnp.zeros((s, ROPE_HALF), F32), sin, jnp.zeros((s, rest), F32)], axis=-1)
    tile = lambda t: jnp.tile(t, (1, LANES // ATT_DIM))
    return tile(c), tile(a), tile(b)


def _ret_rope(x, cos_t, sin_t):
    return x * cos_t + pltpu.roll(x, RET_DIM // 2, axis=1) * sin_t


def _att_rope(x, c, a, b):
    n = x.shape[-1]
    return x * c + pltpu.roll(x, n - ROPE_HALF, axis=1) * a + pltpu.roll(x, ROPE_HALF, axis=1) * b


def _head_norm(o, g):
    mu = jnp.mean(o, axis=-1, keepdims=True)
    oc = o - mu
    var = jnp.mean(oc * oc, axis=-1, keepdims=True)
    return oc * lax.rsqrt(var + EPS) * g


def _ret_kernel(zq_ref, zk_ref, zv_ref, zg_ref, gn_ref,
                ro_ref, so_ref, din_ref, dq_ref, dk_ref, *, chunk, per_step):
    c = pl.program_id(1)

    @pl.when(c == 0)
    def _():
        so_ref[...] = jnp.zeros_like(so_ref)
        row = lax.broadcasted_iota(jnp.int32, (chunk, chunk), 0).astype(F32)
        col = lax.broadcasted_iota(jnp.int32, (chunk, chunk), 1).astype(F32)
        diff = row - col
        rowd = lax.broadcasted_iota(jnp.int32, (chunk, RET_DIM), 0).astype(F32)
        for h in range(RET_HEADS):
            lg = LOG_GAMMA[h]
            din_ref[h] = jnp.where(diff >= 0, jnp.exp(jnp.maximum(diff, 0.0) * lg), 0.0)
            dq_ref[h] = jnp.exp((rowd + 1.0) * lg)
            dk_ref[h] = jnp.exp((chunk - 1.0 - rowd) * lg)

    for h in range(RET_HEADS):
        cols = slice(h * RET_DIM, (h + 1) * RET_DIM)
        s_cur = so_ref[h]
        for j in range(per_step):
            rows = slice(j * chunk, (j + 1) * chunk)
            k = zk_ref[rows, cols]
            qb = zq_ref[rows, cols].astype(BF16)
            kb = k.astype(BF16)
            vb = zv_ref[rows, cols].astype(BF16)
            a = _dot_nt(qb, kb) * din_ref[h]
            o = _dot(a.astype(BF16), vb) + _dot(qb, s_cur.astype(BF16)) * dq_ref[h]
            kd_t = (k * dk_ref[h]).T.astype(BF16)
            s_cur = math.exp(chunk * LOG_GAMMA[h]) * s_cur + _dot(kd_t, vb)

            on = _head_norm(o, gn_ref[:, cols])
            g = zg_ref[rows, cols]
            ro_ref[rows, cols] = (g * jax.nn.sigmoid(g) * on).astype(BF16)
        so_ref[h] = s_cur


def _retention_prompt(z3, ret_norm_g, layer, per_step):
    b, s, _ = z3.shape
    chunk = math.gcd(s, RET_CHUNK)
    per_step = math.gcd(s // chunk, per_step)
    rows = chunk * per_step
    zspec = lambda cb: pl.BlockSpec((None, rows, RET_WIDTH), lambda i, c: (i, c, cb))
    return pl.pallas_call(
        functools.partial(_ret_kernel, chunk=chunk, per_step=per_step),
        out_shape=(jax.ShapeDtypeStruct((b, s, RET_WIDTH), BF16),
                   jax.ShapeDtypeStruct((b, RET_HEADS, RET_DIM, RET_DIM), F32)),
        grid=(b, s // rows),
        in_specs=[
            zspec(COL_RQ // RET_WIDTH), zspec(COL_RK // RET_WIDTH),
            zspec(COL_RV // RET_WIDTH), zspec(COL_RG // RET_WIDTH),
            pl.BlockSpec((None, 1, RET_WIDTH), lambda i, c: (layer, 0, 0)),
        ],
        out_specs=(
            pl.BlockSpec((None, rows, RET_WIDTH), lambda i, c: (i, c, 0)),
            pl.BlockSpec((None, RET_HEADS, RET_DIM, RET_DIM), lambda i, c: (i, 0, 0, 0)),
        ),
        scratch_shapes=[
            pltpu.VMEM((RET_HEADS, chunk, chunk), F32),
            pltpu.VMEM((RET_HEADS, chunk, RET_DIM), F32),
            pltpu.VMEM((RET_HEADS, chunk, RET_DIM), F32),
        ],
        compiler_params=_params("parallel", "arbitrary"),
        name="retention",
    )(z3, z3, z3, z3, ret_norm_g)


def _ret_dec_kernel(z_ref, s0_ref, cos_ref, sin_ref, gn_ref, ro_ref, so_ref, *, nb):
    cos_t = cos_ref[...]
    sin_t = sin_ref[...]
    row = lax.broadcasted_iota(jnp.int32, (nb, RET_DIM), 0)
    for h in range(RET_HEADS):
        gamma = math.exp(LOG_GAMMA[h])
        q = _ret_rope(z_ref[:, COL_RQ + h * RET_DIM:COL_RQ + (h + 1) * RET_DIM], cos_t, sin_t)
        k = _ret_rope(z_ref[:, COL_RK + h * RET_DIM:COL_RK + (h + 1) * RET_DIM], cos_t, sin_t)
        k = k * (RET_DIM ** -0.5)
        v = z_ref[:, COL_RV + h * RET_DIM:COL_RV + (h + 1) * RET_DIM]
        qr, kr, vr = _bf16_round(q), _bf16_round(k), _bf16_round(v)
        qk = jnp.sum(qr * kr, axis=-1, keepdims=True)
        o = _bf16_round(qk) * vr
        cross = jnp.zeros((nb, RET_DIM), F32)
        for b in range(nb):
            s_old = s0_ref[b, h]
            res = _dot(qr.astype(BF16), s_old.astype(BF16))
            cross = cross + jnp.where(row == b, res, 0.0)
            k_col = jnp.broadcast_to(kr[b:b + 1, :], (RET_DIM, RET_DIM)).T
            so_ref[b, h] = gamma * s_old + k_col * vr[b:b + 1, :]
        o = o + cross * gamma
        cols = slice(h * RET_DIM, (h + 1) * RET_DIM)
        on = _head_norm(o, gn_ref[:, cols])
        g = z_ref[:, COL_RG + h * RET_DIM:COL_RG + (h + 1) * RET_DIM]
        ro_ref[:, cols] = g * jax.nn.sigmoid(g) * on


def _retention_decode(z, state, cos_t, sin_t, ret_norm_g, layer):
    nb = z.shape[0]
    sshape = (nb, RET_HEADS, RET_DIM, RET_DIM)
    return pl.pallas_call(
        functools.partial(_ret_dec_kernel, nb=nb),
        out_shape=(jax.ShapeDtypeStruct((nb, RET_WIDTH), F32),
                   jax.ShapeDtypeStruct(sshape, F32)),
        grid=(1,),
        in_specs=[
            pl.BlockSpec(z.shape, lambda i: (0, 0)),
            pl.BlockSpec((None,) + sshape, lambda i: (layer, 0, 0, 0, 0)),
            pl.BlockSpec((1, RET_DIM), lambda i: (0, 0)),
            pl.BlockSpec((1, RET_DIM), lambda i: (0, 0)),
            pl.BlockSpec((None, 1, RET_WIDTH), lambda i: (layer, 0, 0)),
        ],
        out_specs=(
            pl.BlockSpec((nb, RET_WIDTH), lambda i: (0, 0)),
            pl.BlockSpec(sshape, lambda i: (0, 0, 0, 0)),
        ),
        compiler_params=_params("arbitrary"),
        name="retention_decode",
    )(z, state, cos_t, sin_t, ret_norm_g)


ATT_HALF = 128
ATT_MAX_STRIDE = 4
N_HALF = ATT_GROUP // ATT_HALF
HEADS_PER_HALF = ATT_HALF // ATT_DIM


def _half_cols(hf):
    return slice(hf * ATT_HALF, (hf + 1) * ATT_HALF)


def _att_kernel(*refs, seq):
    zqkv = (refs[0:N_HALF], refs[N_HALF:2 * N_HALF], refs[2 * N_HALF:3 * N_HALF])
    ao_ref, o_ref, lse_ref, st_ref, ost_ref = refs[3 * N_HALF:]
    g = pl.program_id(1)
    qb = QUERY_BLOCK
    n_blocks = seq // qb

    tq = lax.broadcasted_iota(jnp.int32, (qb, qb), 0)
    tk = lax.broadcasted_iota(jnp.int32, (qb, qb), 1)
    cur_valid = tk <= tq
    prev_valid = tk >= tq
    lane = lax.broadcasted_iota(jnp.int32, (qb, ATT_HALF), 1)
    head_masks = [(lane // ATT_DIM) == hh for hh in range(HEADS_PER_HALF)]

    def group_body(gi, dil):
        d1 = min(dil, ATT_MAX_STRIDE)
        d2 = dil // d1
        staged = d2 > 1
        assert d2 <= ATT_MAX_STRIDE and d1 * d2 == dil
        sub_len = seq // d1

        def strided(start, n, stride):
            return pl.ds(start, n, stride=stride) if stride > 1 else pl.ds(start, n)

        if staged:
            for hf in range(N_HALF):
                for r1 in range(d1):
                    src = strided(r1, sub_len, d1)
                    for which in range(3):
                        st_ref[which, hf, r1] = zqkv[which][hf][src, :]

        def load(which, hf, cls, blk):
            if staged:
                rows = strided(cls // d1 + d2 * qb * blk, qb, d2)
                return st_ref[which, hf, cls % d1, rows, :].astype(BF16)
            rows = strided(cls + dil * qb * blk, qb, dil)
            return zqkv[which][hf][rows, :].astype(BF16)

        def store(which, hf, cls, blk, val):
            if staged:
                rows = strided(cls // d1 + d2 * qb * blk, qb, d2)
                ost_ref[which, hf, cls % d1, rows, :] = val
            else:
                rows = strided(cls + dil * qb * blk, qb, dil)
                (o_ref, lse_ref)[which][gi, hf, rows, :] = val

        def block_body(t, carry):
            cls = t % dil
            blk = t // dil
            pblk = jnp.maximum(blk - 1, 0)
            valid = jnp.concatenate([prev_valid & (blk > 0), cur_valid], axis=1)
            for hf in range(N_HALF):
                q = load(0, hf, cls, blk)
                keys = jnp.concatenate([load(1, hf, cls, pblk), load(1, hf, cls, blk)], axis=0)
                vals = jnp.concatenate([load(2, hf, cls, pblk), load(2, hf, cls, blk)], axis=0)
                vals_ones = jnp.concatenate([vals, jnp.ones_like(vals)], axis=1)
                num = jnp.zeros((qb, ATT_HALF), F32)
                den = jnp.ones((qb, ATT_HALF), F32)
                lse_acc = jnp.zeros((qb, ATT_HALF), F32)
                for hm in head_masks:
                    qh = jnp.where(hm, q, jnp.zeros_like(q))
                    s = jnp.where(valid, _dot_nt(qh, keys), MASK_VALUE)
                    m = jnp.max(s, axis=-1, keepdims=True)
                    e = jnp.exp(s - m).astype(BF16)
                    r = _dot(e, vals_ones)
                    l = r[:, ATT_HALF:]
                    num = jnp.where(hm, r[:, :ATT_HALF], num)
                    den = jnp.where(hm, l, den)
                    lse_acc = jnp.where(hm, m + jnp.log(l), lse_acc)
                store(0, hf, cls, blk, num / den)
                store(1, hf, cls, blk, lse_acc)
            return carry

        lax.fori_loop(0, n_blocks, block_body, 0, unroll=4)

        if staged:
            for hf in range(N_HALF):
                for r1 in range(d1):
                    dst = strided(r1, sub_len, d1)
                    o_ref[gi, hf, dst, :] = ost_ref[0, hf, r1]
                    lse_ref[gi, hf, dst, :] = ost_ref[1, hf, r1]

    for gi, (_, dil) in enumerate(ATT_PATTERNS):
        @pl.when(g == gi)
        def _(gi=gi, dil=dil):
            group_body(gi, dil)

    @pl.when(g == N_PAT - 1)
    def _():
        def merge_body(i, carry):
            r = pl.multiple_of(i * qb, qb)
            rows = pl.ds(r, qb)
            for hf in range(N_HALF):
                lses = [lse_ref[gi, hf, rows, :] for gi in range(N_PAT)]
                m = functools.reduce(jnp.maximum, lses)
                ws = [jnp.exp(l - m) for l in lses]
                den = functools.reduce(lambda x, y: x + y, ws)
                num = functools.reduce(lambda x, y: x + y,
                                       [w * o_ref[gi, hf, rows, :] for gi, w in enumerate(ws)])
                ao_ref[rows, _half_cols(hf)] = (num / den).astype(BF16)
            return carry

        lax.fori_loop(0, n_blocks, merge_body, 0)


def _attention_prompt(z3):
    b, s, _ = z3.shape
    assert all(s % (dil * QUERY_BLOCK) == 0 for _, dil in ATT_PATTERNS)
    zspec = lambda col, hf: pl.BlockSpec(
        (None, s, ATT_HALF), lambda i, g: (i, 0, col // ATT_HALF + N_HALF * g + hf))
    zspecs = [zspec(col, hf) for col in (COL_AQ, COL_AK, COL_AV) for hf in range(N_HALF)]
    return pl.pallas_call(
        functools.partial(_att_kernel, seq=s),
        out_shape=jax.ShapeDtypeStruct((b, s, ATT_GROUP), BF16),
        grid=(b, N_PAT),
        in_specs=zspecs,
        out_specs=pl.BlockSpec((None, s, ATT_GROUP), lambda i, g: (i, 0, 0)),
        scratch_shapes=[
            pltpu.VMEM((N_PAT, N_HALF, s, ATT_HALF), F32),
            pltpu.VMEM((N_PAT, N_HALF, s, ATT_HALF), F32),
            pltpu.VMEM((3, N_HALF, ATT_MAX_STRIDE, s // ATT_MAX_STRIDE, ATT_HALF), F32),
            pltpu.VMEM((2, N_HALF, ATT_MAX_STRIDE, s // ATT_MAX_STRIDE, ATT_HALF), F32),
        ],
        compiler_params=_params("parallel", "arbitrary"),
        name="dilated_attention",
    )(*([z3] * (3 * N_HALF)))


def _cache_shift_kernel(*refs):
    n = len(refs) // 2
    for src, dst in zip(refs[:n], refs[n:]):
        h, dim, w = src.shape
        dst[...] = pltpu.roll(src[...].reshape(h * dim, w), w - 1, axis=1).reshape(h, dim, w)


def _cache_shift(caches):
    depth, nb = caches[0].shape[:2]
    specs = [pl.BlockSpec((None, None) + c.shape[2:], lambda l, i: (l, i, 0, 0, 0)) for c in caches]
    return pl.pallas_call(
        _cache_shift_kernel,
        out_shape=tuple(jax.ShapeDtypeStruct(c.shape, c.dtype) for c in caches),
        grid=(depth, nb),
        in_specs=specs,
        out_specs=tuple(specs),
        compiler_params=_params("arbitrary", "arbitrary"),
        name="cache_shift",
    )(*caches)


def _row_to_col(row, eye):
    return jnp.sum(jnp.where(eye, row, 0.0), axis=1, keepdims=True)


def _col_to_row(col, eye):
    return jnp.sum(jnp.where(eye, col, 0.0), axis=0, keepdims=True)


def _att_dec_kernel(z_ref, c_ref, a_ref, b_ref, *refs):
    kc_refs = refs[0:N_PAT]
    vc_refs = refs[N_PAT:2 * N_PAT]
    kt_refs = refs[2 * N_PAT:3 * N_PAT]
    vt_refs = refs[3 * N_PAT:4 * N_PAT]
    ao_ref = refs[4 * N_PAT]
    kt_out_refs = refs[4 * N_PAT + 1:5 * N_PAT + 1]
    vt_out_refs = refs[5 * N_PAT + 1:6 * N_PAT + 1]
    rope =(c_ref[...], a_ref[...], b_ref[...])
    eye = (lax.broadcasted_iota(jnp.int32, (ATT_DIM, ATT_DIM), 0)
           == lax.broadcasted_iota(jnp.int32, (ATT_DIM, ATT_DIM), 1))
    last_lane = lax.broadcasted_iota(jnp.int32, (ATT_DIM, ATT_HALF), 1) == ATT_HALF - 1

    head_rows = []
    for h in range(ATT_HEADS):
        hf, lanes = h // HEADS_PER_HALF, slice((h % HEADS_PER_HALF) * ATT_DIM,
                                               (h % HEADS_PER_HALF + 1) * ATT_DIM)
        outs, lses = [], []
        for gi, (win, dil) in enumerate(ATT_PATTERNS):
            off = gi * ATT_GROUP + hf * ATT_HALF
            zrow = lambda col: z_ref[:, col + off:col + off + ATT_HALF]
            q = (_att_rope(zrow(COL_AQ), *rope) * (ATT_DIM ** -0.5))[:, lanes]
            k_new = _att_rope(zrow(COL_AK), *rope)[:, lanes]
            v_new = zrow(COL_AV)[:, lanes]
            k_col, v_col = _row_to_col(k_new, eye), _row_to_col(v_new, eye)
            keys, vals = kc_refs[gi][h], vc_refs[gi][h]
            w = keys.shape[1]
            pos = lax.broadcasted_iota(jnp.int32, (1, w), 1)
            s_old = jnp.sum(keys * _row_to_col(q, eye), axis=0, keepdims=True)
            s_old = jnp.where(pos % dil == 0, s_old, MASK_VALUE)
            s_new = jnp.sum(q * k_new, axis=1, keepdims=True)
            m = jnp.maximum(jnp.max(s_old, axis=1, keepdims=True), s_new)
            e_old = jnp.exp(s_old - m)
            e_new = jnp.exp(s_new - m)
            l = jnp.sum(e_old, axis=1, keepdims=True) + e_new
            pv = jnp.sum(vals * e_old, axis=1, keepdims=True) + e_new * v_col
            outs.append(pv / l)
            lses.append(m + jnp.log(l))
            kt_out_refs[gi][h] = jnp.where(last_lane, k_col, kt_refs[gi][h])
            vt_out_refs[gi][h] = jnp.where(last_lane, v_col, vt_refs[gi][h])

        m = functools.reduce(jnp.maximum, lses)
        ws = [jnp.exp(l - m) for l in lses]
        den = functools.reduce(lambda x, y: x + y, ws)
        num = functools.reduce(lambda x, y: x + y, [w * o for w, o in zip(ws, outs)])
        head_rows.append(_col_to_row(num / den, eye))
    ao_ref[...] = jnp.concatenate(head_rows, axis=1)


def _attention_decode(z, tabs, k_caches, v_caches, k_next, v_next, layer):
    nb = z.shape[0]
    n_fixed = 4
    cspec = lambda c: pl.BlockSpec((None, None) + c.shape[2:], lambda i: (layer, i, 0, 0, 0))
    tail = lambda c: pl.BlockSpec((None, None) + c.shape[2:4] + (ATT_HALF,),
                                  lambda i: (layer, i, 0, 0, c.shape[4] // ATT_HALF - 1))
    tspec = pl.BlockSpec((1, ATT_HALF), lambda i: (0, 0))
    caches = list(k_caches) + list(v_caches)
    nexts = list(k_next) + list(v_next)
    return pl.pallas_call(
        _att_dec_kernel,
        out_shape=tuple([jax.ShapeDtypeStruct((nb, 1, ATT_GROUP), F32)]
                        + [jax.ShapeDtypeStruct(c.shape, c.dtype) for c in nexts]),
        grid=(nb,),
        in_specs=[pl.BlockSpec((None, 1, z.shape[1]), lambda i: (i, 0, 0)), tspec, tspec, tspec]
                 + [cspec(c) for c in caches] + [tail(c) for c in nexts],
        out_specs=tuple([pl.BlockSpec((None, 1, ATT_GROUP), lambda i: (i, 0, 0))]
                        + [tail(c) for c in nexts]),
        input_output_aliases={n_fixed + len(caches) + j: 1 + j for j in range(len(nexts))},
        compiler_params=_params("arbitrary"),
        name="dilated_attention_decode",
    )(z.reshape(nb, 1, z.shape[1]), *tabs, *caches, *nexts)


def _pool_kernel(u_ref, w_ref, sc_ref, po_ref, a_ref, b_ref, *, seq):
    g = pl.program_id(1)
    body = pl.ds(POOL_PAD, seq)

    def window_mean_minus_token(win):
        x = u_ref[...]
        a_ref[0:POOL_PAD, :] = jnp.zeros((POOL_PAD, POOL_GROUP), F32)
        b_ref[0:POOL_PAD, :] = jnp.zeros((POOL_PAD, POOL_GROUP), F32)
        a_ref[body, :] = x
        src, dst = a_ref, b_ref
        k = 1
        while k < win:
            dst[body, :] = src[body, :] + src[pl.ds(POOL_PAD - k, seq), :]
            src, dst = dst, src
            k *= 2
        t = lax.broadcasted_iota(jnp.int32, (seq, POOL_GROUP), 0)
        cnt = jnp.minimum(t + 1, win).astype(F32)
        pooled = src[body, :] / cnt - x
        y = _dot(pooled.astype(BF16), w_ref[...].astype(BF16)) * sc_ref[...]
        po_ref[...] = y.astype(BF16)

    for gi, win in enumerate(POOL_WINDOWS):
        @pl.when(g == gi)
        def _(win=win):
            window_mean_minus_token(win)


def _pool_prompt(z3, w_pool, pool_scale, layer):
    b, s, _ = z3.shape
    ng = len(POOL_WINDOWS)
    return pl.pallas_call(
        functools.partial(_pool_kernel, seq=s),
        out_shape=jax.ShapeDtypeStruct((b, s, POOL_WIDTH), BF16),
        grid=(b, ng),
        in_specs=[
            pl.BlockSpec((None, s, POOL_GROUP), lambda i, g: (i, 0, COL_PU // POOL_GROUP + g)),
            pl.BlockSpec((None, None, POOL_GROUP, POOL_GROUP), lambda i, g: (layer, g, 0, 0)),
            pl.BlockSpec((None, 1, POOL_GROUP), lambda i, g: (layer, 0, g)),
        ],
        out_specs=pl.BlockSpec((None, s, POOL_GROUP), lambda i, g: (i, 0, g)),
        scratch_shapes=[pltpu.VMEM((POOL_PAD + s, POOL_GROUP), F32),
                        pltpu.VMEM((POOL_PAD + s, POOL_GROUP), F32)],
        compiler_params=_params("parallel", "arbitrary"),
        name="pool_mixer",
    )(z3, w_pool, pool_scale)


def _pool_dec_kernel(z_ref, buf_ref, w_ref, sc_ref, po_ref, bo_ref, pooled_ref, *, nb):
    row = lax.broadcasted_iota(jnp.int32, (POOL_BUF, POOL_GROUP), 0)
    for b in range(nb):
        u = z_ref[b:b + 1, COL_PU:COL_PU + POOL_WIDTH]
        old = buf_ref[b]
        for gi, win in enumerate(POOL_WINDOWS):
            cols = slice(gi * POOL_GROUP, (gi + 1) * POOL_GROUP)
            tail = jnp.where(row >= POOL_BUF - (win - 1), old[:, cols], 0.0)
            total = jnp.sum(tail, axis=0, keepdims=True) + u[:, cols]
            pooled_ref[b:b + 1, cols] = total / float(win) - u[:, cols]
        bo_ref[b, 0:POOL_BUF - 1, :] = old[1:POOL_BUF, :]
        bo_ref[b, POOL_BUF - 1:POOL_BUF, :] = u
    for gi in range(len(POOL_WINDOWS)):
        cols = slice(gi * POOL_GROUP, (gi + 1) * POOL_GROUP)
        y = _dot(pooled_ref[:, cols].astype(BF16), w_ref[gi].astype(BF16))
        po_ref[:, cols] = y * sc_ref[:, cols]


def _pool_decode(z, cache_pool, w_pool, pool_scale, layer):
    nb = z.shape[0]
    ng = len(POOL_WINDOWS)
    bshape = (nb, POOL_BUF, POOL_WIDTH)
    return pl.pallas_call(
        functools.partial(_pool_dec_kernel, nb=nb),
        out_shape=(jax.ShapeDtypeStruct((nb, POOL_WIDTH), F32),
                   jax.ShapeDtypeStruct(bshape, F32)),
        grid=(1,),
        in_specs=[
            pl.BlockSpec(z.shape, lambda i: (0, 0)),
            pl.BlockSpec((None,) + bshape, lambda i: (layer, 0, 0, 0)),
            pl.BlockSpec((None, ng, POOL_GROUP, POOL_GROUP), lambda i: (layer, 0, 0, 0)),
            pl.BlockSpec((None, 1, POOL_WIDTH), lambda i: (layer, 0, 0)),
        ],
        out_specs=(pl.BlockSpec((nb, POOL_WIDTH), lambda i: (0, 0)),
                   pl.BlockSpec(bshape, lambda i: (0, 0, 0))),
        scratch_shapes=[pltpu.VMEM((nb, POOL_WIDTH), F32)],
        compiler_params=_params("arbitrary"),
        name="pool_mixer_decode",
    )(z, cache_pool, w_pool, pool_scale)


FFN_COLS = 512
RET_CHUNKS_PER_STEP = 8


def _prompt_tiles(seq):
    return {
        "ffn_rows": math.gcd(seq, 1024),
        "inproj_rows": math.gcd(seq, 256),
        "outproj_rows": math.gcd(seq, 512),
    }


def _trunk_prompt(x, mod, p, wb, final_g):
    b, s, d = x.shape
    depth = p["w_in"].shape[0]
    pos = jnp.arange(s, dtype=F32)
    ret_tabs = _ret_rope_tables(pos)
    att_tabs = _att_rope_tables(pos)
    keep = tuple(min(win, s) for win, _ in ATT_PATTERNS)
    t = _prompt_tiles(s)
    xf = x.reshape(b * s, d)
    rets, pools = [], []
    ks = [[] for _ in ATT_PATTERNS]
    vs = [[] for _ in ATT_PATTERNS]
    for l in range(depth):
        xf = _ffn(xf, mod, l, 0, s, p["norm_g"], *wb[l]["ffn1"], None, t["ffn_rows"], FFN_COLS)
        z = _inproj(xf, mod, l, s, p["norm_g"], wb[l]["w_in"], ret_tabs + att_tabs, t["inproj_rows"])
        z3 = z.reshape(b, s, IN_WIDTH)
        ro, ret_s = _retention_prompt(z3, p["ret_norm_g"], l, RET_CHUNKS_PER_STEP)
        ao = _attention_prompt(z3)
        po = _pool_prompt(z3, p["w_pool"], p["pool_scale"], l)
        xf = _outproj(xf, mod, l, s, ro.reshape(b * s, -1), ao.reshape(b * s, -1),
                      po.reshape(b * s, -1), p["w_out"], t["outproj_rows"])
        xf = _ffn(xf, mod, l, 2, s, p["norm_g"], *wb[l]["ffn2"],
                  final_g if l == depth - 1 else None, t["ffn_rows"], FFN_COLS)
        rets.append(ret_s)
        pools.append(z3[:, s - POOL_BUF:, COL_PU:])
        for g in range(N_PAT):
            for col, dst in ((COL_AK, ks), (COL_AV, vs)):
                c0 = col + g * ATT_GROUP
                dst[g].append(z3[:, s - keep[g]:, c0:c0 + ATT_GROUP]
                              .reshape(b, keep[g], ATT_HEADS, ATT_DIM))
    y = xf.reshape(b, s, d)
    return (y, jnp.stack(rets), [jnp.stack(k) for k in ks], [jnp.stack(v) for v in vs],
            jnp.stack(pools))


def _trunk_decode(x, mod, pos0, caches, p, final_g):
    nb, s, d = x.shape
    assert s == 1, "decode trunk handles one new token per batch row"
    depth = p["w_in"].shape[0]
    state_ret, cks, cvs, cpool = caches
    pos = pos0 + jnp.arange(s, dtype=F32)
    ret_tabs = _ret_rope_tables(pos)
    att_tabs = _att_rope_tables(pos)
    feature_major = lambda c: jnp.transpose(c, (0, 1, 3, 4, 2))
    position_major = lambda c: jnp.transpose(c, (0, 1, 4, 2, 3))
    cks = [feature_major(c) for c in cks]
    cvs = [feature_major(c) for c in cvs]
    advanced = _cache_shift(cks + cvs)
    k_next, v_next = list(advanced[:N_PAT]), list(advanced[N_PAT:])
    xf = x.reshape(nb, d)
    rets, pools, wb = [], [], []
    for l in range(depth):
        xf, *ffn1_bf16 = _ffn(xf, mod, l, 0, 1, p["norm_g"], p["w1_gate"], p["w1_up"], p["w1_down"],
                              None, nb, FFN_COLS, emit_bf16=True)
        z, w_in_bf16 = _inproj_rows(xf, mod, l, p["norm_g"], p["w_in"])
        ro, ret_s = _retention_decode(z, state_ret, *ret_tabs, p["ret_norm_g"], l)
        att = _attention_decode(z, att_tabs, cks, cvs, k_next, v_next, l)
        ao, k_next, v_next = att[0].reshape(nb, ATT_GROUP), list(att[1:1 + N_PAT]), list(att[1 + N_PAT:])
        po, pool_new = _pool_decode(z, cpool, p["w_pool"], p["pool_scale"], l)
        xf = _outproj(xf, mod, l, 1, ro, ao, po, p["w_out"], nb)
        xf, *ffn2_bf16 = _ffn(xf, mod, l, 2, 1, p["norm_g"], p["w2_gate"], p["w2_up"], p["w2_down"],
                              final_g if l == depth - 1 else None, nb, FFN_COLS, emit_bf16=True)
        wb.append({"ffn1": ffn1_bf16, "w_in": w_in_bf16, "ffn2": ffn2_bf16})
        rets.append(ret_s)
        pools.append(pool_new)
    y = xf.reshape(nb, s, d)
    return (y, jnp.stack(rets), [position_major(k) for k in k_next],
            [position_major(v) for v in v_next], jnp.stack(pools), wb)


def kernel(x_prompt, x_sample, state_ret, cache_k_w128, cache_v_w128, cache_k_w512, cache_v_w512,
           cache_k_w2048, cache_v_w2048, cache_pool, c_prompt, c_sample, w_ada, b_ada, norm_g, w_in,
           ret_norm_g, w_pool, pool_scale, w_out, w1_gate, w1_up, w1_down, w2_gate, w2_up, w2_down,
           final_norm_g):
    depth, d = norm_g.shape[0], norm_g.shape[-1]
    n_pr, n_dec = c_prompt.shape[0], c_sample.shape[0]
    p = {
        "norm_g": norm_g.reshape(depth, N_SUB, 1, d),
        "w_in": w_in,
        "ret_norm_g": ret_norm_g.reshape(depth, 1, RET_WIDTH),
        "w_pool": w_pool,
        "pool_scale": pool_scale.reshape(depth, 1, POOL_WIDTH),
        "w_out": w_out,
        "w1_gate": w1_gate, "w1_up": w1_up, "w1_down": w1_down,
        "w2_gate": w2_gate, "w2_up": w2_up, "w2_down": w2_down,
    }
    final_g = final_norm_g.reshape(1, d)

    pad = (-(n_dec + n_pr)) % SUBLANES
    c_all = jnp.concatenate([c_sample, c_prompt, jnp.zeros((pad, d), F32)], axis=0)
    mod_dec, mod_pr = _ada(c_all, n_dec, n_pr, w_ada, b_ada)

    caches = (state_ret,
              (cache_k_w128, cache_k_w512, cache_k_w2048),
              (cache_v_w128, cache_v_w512, cache_v_w2048),
              cache_pool)
    y_s, ret_s, ks, vs, pool_s, wb = _trunk_decode(x_sample, mod_dec, float(PAST_LEN), caches, p, final_g)
    y_p, ret_p, kp, vp, pool_p = _trunk_prompt(x_prompt, mod_pr, p, wb, final_g)
    return (y_p, y_s, ret_p, ret_s,
            kp[0], ks[0], vp[0], vs[0],
            kp[1], ks[1], vp[1], vs[1],
            kp[2], ks[2], vp[2], vs[2],
            pool_p, pool_s)
```

```python
import functools
import math

import jax
import jax.numpy as jnp
from jax import lax
from jax.experimental import pallas as pl
from jax.experimental.pallas import tpu as pltpu

F32 = jnp.float32
BF16 = jnp.bfloat16

RET_HEADS = 6
RET_DIM = 128
RET_WIDTH = RET_HEADS * RET_DIM
RET_CHUNK = 128
RET_THETA = 10000.0
ATT_HEADS = 4
ATT_DIM = 64
ATT_GROUP = ATT_HEADS * ATT_DIM
ATT_PATTERNS = ((128, 1), (512, 4), (2048, 16))
N_PAT = len(ATT_PATTERNS)
ATT_WIDTH = N_PAT * ATT_GROUP
ROPE_THETA = 500000.0
ROPE_DIMS = ATT_DIM // 4
ROPE_HALF = ROPE_DIMS // 2
QUERY_BLOCK = 128
POOL_WINDOWS = (2, 4, 8, 16)
POOL_GROUP = 128
POOL_WIDTH = len(POOL_WINDOWS) * POOL_GROUP
POOL_BUF = max(POOL_WINDOWS) - 1
POOL_PAD = 16
N_SUB = 3
PAST_LEN = 16384
HALF_STEP = 0.5
EPS = 1e-6
MASK_VALUE = -1e30

COL_RQ, COL_RK, COL_RV, COL_RG = 0, RET_WIDTH, 2 * RET_WIDTH, 3 * RET_WIDTH
COL_AQ = 4 * RET_WIDTH
COL_AK = COL_AQ + ATT_WIDTH
COL_AV = COL_AK + ATT_WIDTH
COL_PU = COL_AV + ATT_WIDTH
IN_WIDTH = COL_PU + POOL_WIDTH

V7X_VMEM_BYTES = 64 * 1024 * 1024
VMEM_LIMIT = V7X_VMEM_BYTES - 4 * 1024 * 1024
SUBLANES = 8
LANES = 128
ROW_CHUNK = 128
RESULT_COLS = 512

LOG_GAMMA = tuple(math.log1p(-(2.0 ** (-5.0 - h))) for h in range(RET_HEADS))


def _params(*sem):
    return pltpu.CompilerParams(dimension_semantics=sem, vmem_limit_bytes=VMEM_LIMIT)


def _dot(a, b):
    return jnp.dot(a, b, preferred_element_type=F32)


def _dot_nt(a, b):
    return lax.dot_general(a, b, (((1,), (1,)), ((), ())), preferred_element_type=F32)


def _bf16_round(x):
    return x.astype(BF16).astype(F32)


def _ada_kernel(c_ref, w_ref, b_ref, od_ref, op_ref, *, n_dec, n_pr):
    c = c_ref[...]
    a = (c * jax.nn.sigmoid(c)).astype(BF16)
    res = _dot(a, w_ref[...].astype(BF16)) + b_ref[...]
    od_ref[...] = res[0:n_dec]
    for b in range(n_pr):
        op_ref[b] = res[n_dec + b:n_dec + b + 1]


def _ada(c_all, n_dec, n_pr, w_ada, b_ada, tn=1024):
    depth, d, n = w_ada.shape
    rows = c_all.shape[0]
    per = d // tn
    return pl.pallas_call(
        functools.partial(_ada_kernel, n_dec=n_dec, n_pr=n_pr),
        out_shape=(jax.ShapeDtypeStruct((depth, N_SUB * 3, n_dec, d), F32),
                   jax.ShapeDtypeStruct((depth, N_SUB * 3, n_pr, 1, d), F32)),
        grid=(depth, n // tn),
        in_specs=[
            pl.BlockSpec((rows, d), lambda l, j: (0, 0)),
            pl.BlockSpec((None, d, tn), lambda l, j: (l, 0, j)),
            pl.BlockSpec((None, 1, tn), lambda l, j: (l, 0, j)),
        ],
        out_specs=(
            pl.BlockSpec((None, None, n_dec, tn), lambda l, j: (l, j // per, 0, j % per)),
            pl.BlockSpec((None, None, n_pr, 1, tn), lambda l, j: (l, j // per, 0, 0, j % per)),
        ),
        compiler_params=_params("arbitrary", "arbitrary"),
        name="ada_mod",
    )(c_all, w_ada, b_ada.reshape(depth, 1, n))


def _rmsnorm(x, g):
    ms = jnp.mean(x * x, axis=-1, keepdims=True)
    return x * lax.rsqrt(ms + EPS) * g


def _row_chunks(tm):
    rc = min(tm, ROW_CHUNK)
    return rc, tm // rc


def _mod_rows(ref, r, rc):
    return ref[...] if ref.shape[0] == 1 else ref[pl.ds(r, rc), :]


def _prenorm_to(h_ref, x_ref, g_ref, sh_ref, sc_ref, unrolled=False):
    rc, n = _row_chunks(x_ref.shape[0])

    def body(i, carry):
        r = i * rc if unrolled else pl.multiple_of(i * rc, rc)
        y = _rmsnorm(x_ref[pl.ds(r, rc), :], g_ref[...])
        h = y * (1.0 + _mod_rows(sc_ref, r, rc)) + _mod_rows(sh_ref, r, rc)
        h_ref[pl.ds(r, rc), :] = h.astype(BF16)
        return carry

    if unrolled:
        for i in range(n):
            body(i, 0)
    else:
        lax.fori_loop(0, n, body, 0)


def _mod_specs(mod, layer, sub, rows_per_batch, tm, grid_rank):
    d = mod.shape[-1]
    specs = []
    for k in range(3):
        j = sub * 3 + k
        if mod.ndim == 5:
            if grid_rank == 2:
                idx = (lambda j: lambda i, f: (layer, j, (i * tm) // rows_per_batch, 0, 0))(j)
            else:
                idx = (lambda j: lambda i: (layer, j, (i * tm) // rows_per_batch, 0, 0))(j)
            specs.append(pl.BlockSpec((None, None, None, 1, d), idx))
        else:
            if grid_rank == 2:
                idx = (lambda j: lambda i, f: (layer, j, 0, 0))(j)
            else:
                idx = (lambda j: lambda i: (layer, j, 0, 0))(j)
            specs.append(pl.BlockSpec((None, None, tm, d), idx))
    return specs


def _ffn_kernel(x_ref, sh_ref, sc_ref, gt_ref, g_ref, wg_ref, wu_ref, wd_ref, *rest,
                n_f, final_norm, emit_bf16, cast_next):
    rest = list(rest)
    fg_ref = rest.pop(0) if final_norm else None
    cast_in = [rest.pop(0) for _ in range(3)] if cast_next else []
    o_ref = rest.pop(0)
    wb_refs = [rest.pop(0) for _ in range(3)] if emit_bf16 else None
    cast_out = [rest.pop(0) for _ in range(3)] if cast_next else []
    h_ref, = rest
    f = pl.program_id(1)

    assert n_f >= 2
    fuse_residual = not final_norm

    def step(first, last):
        if first:
            _prenorm_to(h_ref, x_ref, g_ref, sh_ref, sc_ref, unrolled=True)
        wg = wg_ref[...].astype(BF16)
        wu = wu_ref[...].astype(BF16)
        wd = wd_ref[...].astype(BF16)
        if emit_bf16:
            wb_refs[0][...] = wg
            wb_refs[1][...] = wu
            wb_refs[2][...] = wd
        for src, dst in zip(cast_in, cast_out):
            dst[...] = src[...].astype(BF16)
        h = h_ref[...]
        gate = _dot(h, wg)
        up = _dot(h, wu)
        act = (gate * jax.nn.sigmoid(gate) * up).astype(BF16)
        d = o_ref.shape[1]
        dc = min(d, RESULT_COLS)
        for c0 in range(0, d, dc):
            cols = slice(c0, c0 + dc)
            y = _dot(act, wd[:, cols])
            if first:
                o_ref[:, cols] = y
            elif last and fuse_residual:
                o_ref[:, cols] = x_ref[:, cols] + HALF_STEP * gt_ref[:, cols] * (o_ref[:, cols] + y)
            else:
                o_ref[:, cols] += y

    pl.when(f == 0)(lambda: step(True, False))
    pl.when((f > 0) & (f < n_f - 1))(lambda: step(False, False))
    pl.when(f == n_f - 1)(lambda: step(False, True))

    if not fuse_residual:
        @pl.when(f == n_f - 1)
        def _():
            rc, n = _row_chunks(x_ref.shape[0])

            def body(i, carry):
                r = pl.multiple_of(i * rc, rc)
                rows = pl.ds(r, rc)
                out = x_ref[rows, :] + HALF_STEP * _mod_rows(gt_ref, r, rc) * o_ref[rows, :]
                o_ref[rows, :] = _rmsnorm(out, fg_ref[...])
                return carry

            lax.fori_loop(0, n, body, 0)


def _weight_spec(w, layer, block, index):
    if w.ndim == 3:
        return pl.BlockSpec((None,) + block, lambda i, j: (layer,) + index(i, j))
    return pl.BlockSpec(block, index)


def _row_slices(rows, max_steps):
    steps = max(s for s in range(1, max_steps + 1) if rows % s == 0 and (rows // s) % 16 == 0)
    return steps, rows // steps


def _ffn(x, mod, layer, sub, rows_per_batch, norm_g, wg, wu, wd, final_g, tm, tf, emit_bf16=False,
         cast_next=None):
    m, d = x.shape
    d_ff = wg.shape[-1]
    assert m % tm == 0 and d_ff % tf == 0
    assert not emit_bf16 or m == tm, "each weight block must be visited exactly once"
    n_f = d_ff // tf
    final_norm = final_g is not None
    col_block = lambda i, j: (0, j)
    row_block = lambda i, j: (j, 0)
    in_specs = [pl.BlockSpec((tm, d), lambda i, j: (i, 0))]
    in_specs += _mod_specs(mod, layer, sub, rows_per_batch, tm, 2)
    in_specs += [
        pl.BlockSpec((None, None, 1, d), lambda i, j: (layer, sub, 0, 0)),
        _weight_spec(wg, layer, (d, tf), col_block),
        _weight_spec(wu, layer, (d, tf), col_block),
        _weight_spec(wd, layer, (tf, d), row_block),
    ]
    args = [x, mod, mod, mod, norm_g, wg, wu, wd]
    if final_norm:
        in_specs.append(pl.BlockSpec((1, d), lambda i, j: (0, 0)))
        args.append(final_g)
    out_shape = [jax.ShapeDtypeStruct((m, d), F32)]
    out_specs = [pl.BlockSpec((tm, d), lambda i, j: (i, 0))]
    if emit_bf16:
        out_shape += [jax.ShapeDtypeStruct((d, d_ff), BF16), jax.ShapeDtypeStruct((d, d_ff), BF16),
                      jax.ShapeDtypeStruct((d_ff, d), BF16)]
        out_specs += [pl.BlockSpec((d, tf), col_block), pl.BlockSpec((d, tf), col_block),
                      pl.BlockSpec((tf, d), row_block)]
    if cast_next is not None:
        *next_w, layer_next = cast_next
        n_steps = (m // tm) * n_f
        for w in next_w:
            rows, cols = w.shape[1:]
            steps, blk = _row_slices(rows, n_steps)
            at = (lambda steps: lambda i, j: jnp.minimum(i * n_f + j, steps - 1))(steps)
            in_specs.append(pl.BlockSpec((None, blk, cols),
                                         (lambda at: lambda i, j: (layer_next, at(i, j), 0))(at)))
            args.append(w)
            out_shape.append(jax.ShapeDtypeStruct((rows, cols), BF16))
            out_specs.append(pl.BlockSpec((blk, cols), (lambda at: lambda i, j: (at(i, j), 0))(at)))
    many = emit_bf16 or cast_next is not None
    out = pl.pallas_call(
        functools.partial(_ffn_kernel, n_f=n_f, final_norm=final_norm, emit_bf16=emit_bf16,
                          cast_next=cast_next is not None),
        out_shape=tuple(out_shape),
        grid=(m // tm, n_f),
        in_specs=in_specs,
        out_specs=tuple(out_specs),
        scratch_shapes=[pltpu.VMEM((tm, d), BF16)],
        compiler_params=_params("arbitrary", "arbitrary"),
        name="ffn",
    )(*args)
    return out if many else out[0]


def _rotate_projection_block(y, col, ret_tabs, att_tabs):
    if col < COL_RV:
        y = _ret_rope(y, *ret_tabs)
        return y * (RET_DIM ** -0.5) if col >= COL_RK else y
    if COL_AQ <= col < COL_AV:
        y = _att_rope(y, *att_tabs)
        return y * (ATT_DIM ** -0.5) if col < COL_AK else y
    return y


def _inproj_kernel(x_ref, sh_ref, sc_ref, g_ref, w_ref, rc_ref, rs_ref, ac_ref, aa_ref, ab_ref,
                   o_ref, h_ref):
    _prenorm_to(h_ref, x_ref, g_ref, sh_ref, sc_ref, unrolled=True)
    h = h_ref[...]
    ret_tabs = (rc_ref[...], rs_ref[...])
    att_tabs = (ac_ref[...], aa_ref[...], ab_ref[...])
    n = o_ref.shape[1]
    for c0 in range(0, n, RESULT_COLS):
        c1 = min(c0 + RESULT_COLS, n)
        y = _dot(h, w_ref[:, c0:c1])
        for b0 in range(0, c1 - c0, LANES):
            o_ref[:, c0 + b0:c0 + b0 + LANES] = _rotate_projection_block(
                y[:, b0:b0 + LANES], c0 + b0, ret_tabs, att_tabs)


def _inproj(x, mod, layer, rows_per_batch, norm_g, w_in_bf16, rope_tabs, tm):
    m, d = x.shape
    n = w_in_bf16.shape[-1]
    assert m % tm == 0 and rows_per_batch % tm == 0 and n % LANES == 0
    sh, sc, _ = _mod_specs(mod, layer, 1, rows_per_batch, tm, 1)
    tiles_per_batch = rows_per_batch // tm
    return pl.pallas_call(
        _inproj_kernel,
        out_shape=jax.ShapeDtypeStruct((m, n), F32),
        grid=(m // tm,),
        in_specs=[
            pl.BlockSpec((tm, d), lambda i: (i, 0)),
            sh, sc,
            pl.BlockSpec((None, None, 1, d), lambda i: (layer, 1, 0, 0)),
            pl.BlockSpec((d, n), lambda i: (0, 0), pipeline_mode=pl.Buffered(1)),
        ] + [pl.BlockSpec((tm, LANES), lambda i: (i % tiles_per_batch, 0))] * len(rope_tabs),
        out_specs=pl.BlockSpec((tm, n), lambda i: (i, 0)),
        scratch_shapes=[pltpu.VMEM((tm, d), BF16)],
        compiler_params=_params("parallel"),
        name="in_proj",
    )(x, mod, mod, norm_g, w_in_bf16, *rope_tabs)


def _inproj_rows_kernel(x_ref, sh_ref, sc_ref, g_ref, w_ref, o_ref, wb_ref, h_ref, *, tk):
    k = pl.program_id(0)

    @pl.when(k == 0)
    def _():
        h = _rmsnorm(x_ref[...], g_ref[...]) * (1.0 + sc_ref[...]) + sh_ref[...]
        for c in range(h_ref.shape[0]):
            h_ref[c] = h[:, c * tk:(c + 1) * tk].astype(BF16)
        o_ref[...] = jnp.zeros_like(o_ref)

    wb = w_ref[...].astype(BF16)
    wb_ref[...] = wb
    o_ref[...] += _dot(h_ref[k], wb)


def _inproj_rows(x, mod, layer, norm_g, w_in, tk=512):
    m, d = x.shape
    n = w_in.shape[-1]
    assert d % tk == 0
    sh, sc, _ = _mod_specs(mod, layer, 1, 1, m, 1)
    return pl.pallas_call(
        functools.partial(_inproj_rows_kernel, tk=tk),
        out_shape=(jax.ShapeDtypeStruct((m, n), F32), jax.ShapeDtypeStruct((d, n), BF16)),
        grid=(d // tk,),
        in_specs=[
            pl.BlockSpec((m, d), lambda k: (0, 0)),
            sh, sc,
            pl.BlockSpec((None, None, 1, d), lambda k: (layer, 1, 0, 0)),
            pl.BlockSpec((None, tk, n), lambda k: (layer, k, 0)),
        ],
        out_specs=(pl.BlockSpec((m, n), lambda k: (0, 0)), pl.BlockSpec((tk, n), lambda k: (k, 0))),
        scratch_shapes=[pltpu.VMEM((d // tk, m, tk), BF16)],
        compiler_params=_params("arbitrary"),
        name="in_proj_rows",
    )(x, mod, mod, norm_g, w_in)


def _outproj_kernel(x_ref, gt_ref, ro_ref, ao_ref, po_ref, w_ref, o_ref, wb_ref):
    @pl.when(pl.program_id(0) == 0)
    def _():
        wb_ref[...] = w_ref[...].astype(BF16)

    mix = jnp.concatenate([ro_ref[...].astype(BF16), ao_ref[...].astype(BF16),
                           po_ref[...].astype(BF16)], axis=1)
    o_ref[...] = x_ref[...] + gt_ref[...] * _dot(mix, wb_ref[...])


def _outproj(x, mod, layer, rows_per_batch, ro, ao, po, w_out, tm):
    m, d = x.shape
    k = w_out.shape[1]
    _, _, gt = _mod_specs(mod, layer, 1, rows_per_batch, tm, 1)
    return pl.pallas_call(
        _outproj_kernel,
        out_shape=jax.ShapeDtypeStruct((m, d), F32),
        grid=(m // tm,),
        in_specs=[
            pl.BlockSpec((tm, d), lambda i: (i, 0)),
            gt,
            pl.BlockSpec((tm, RET_WIDTH), lambda i: (i, 0)),
            pl.BlockSpec((tm, ATT_GROUP), lambda i: (i, 0)),
            pl.BlockSpec((tm, POOL_WIDTH), lambda i: (i, 0)),
            pl.BlockSpec((None, k, d), lambda i: (layer, 0, 0), pipeline_mode=pl.Buffered(1)),
        ],
        out_specs=pl.BlockSpec((tm, d), lambda i: (i, 0)),
        scratch_shapes=[pltpu.VMEM((k, d), BF16)],
        compiler_params=_params("arbitrary"),
        name="out_proj",
    )(x, mod, ro, ao, po, w_out)


def _ret_rope_tables(pos):
    half = RET_DIM // 2
    freq = jnp.power(jnp.float32(RET_THETA), -jnp.arange(half, dtype=F32) / half)
    ang = pos[:, None] * freq[None, :]
    cos, sin = jnp.cos(ang), jnp.sin(ang)
    return jnp.concatenate([cos, cos], axis=-1), jnp.concatenate([-sin, sin], axis=-1)


def _att_rope_tables(pos):
    freq = jnp.power(jnp.float32(ROPE_THETA), -jnp.arange(ROPE_HALF, dtype=F32) / ROPE_HALF)
    ang = pos[:, None] * freq[None, :]
    cos, sin = jnp.cos(ang), jnp.sin(ang)
    s = pos.shape[0]
    rest = ATT_DIM - ROPE_DIMS
    c = jnp.concatenate([cos, cos, jnp.ones((s, rest), F32)], axis=-1)
    a = jnp.concatenate([-sin, jnp.zeros((s, ATT_DIM - ROPE_HALF), F32)], axis=-1)
    b = jnp.concatenate([jnp.zeros((s, ROPE_HALF), F32), sin, jnp.zeros((s, rest), F32)], axis=-1)
    tile = lambda t: jnp.tile(t, (1, LANES // ATT_DIM))
    return tile(c), tile(a), tile(b)


def _ret_rope(x, cos_t, sin_t):
    return x * cos_t + pltpu.roll(x, RET_DIM // 2, axis=1) * sin_t


def _att_rope(x, c, a, b):
    n = x.shape[-1]
    return x * c + pltpu.roll(x, n - ROPE_HALF, axis=1) * a + pltpu.roll(x, ROPE_HALF, axis=1) * b


def _head_norm(o, g):
    mu = jnp.mean(o, axis=-1, keepdims=True)
    oc = o - mu
    var = jnp.mean(oc * oc, axis=-1, keepdims=True)
    return oc * lax.rsqrt(var + EPS) * g


def _ret_kernel(zq_ref, zk_ref, zv_ref, zg_ref, gn_ref,
                ro_ref, so_ref, din_ref, dq_ref, dk_ref, *, chunk, per_step):
    c = pl.program_id(1)

    @pl.when(c == 0)
    def _():
        so_ref[...] = jnp.zeros_like(so_ref)
        row = lax.broadcasted_iota(jnp.int32, (chunk, chunk), 0).astype(F32)
        col = lax.broadcasted_iota(jnp.int32, (chunk, chunk), 1).astype(F32)
        diff = row - col
        rowd = lax.broadcasted_iota(jnp.int32, (chunk, RET_DIM), 0).astype(F32)
        for h in range(RET_HEADS):
            lg = LOG_GAMMA[h]
            din_ref[h] = jnp.where(diff >= 0, jnp.exp(jnp.maximum(diff, 0.0) * lg), 0.0)
            dq_ref[h] = jnp.exp((rowd + 1.0) * lg)
            dk_ref[h] = jnp.exp((chunk - 1.0 - rowd) * lg)

    for h in range(RET_HEADS):
        cols = slice(h * RET_DIM, (h + 1) * RET_DIM)
        s_cur = so_ref[h]
        for j in range(per_step):
            rows = slice(j * chunk, (j + 1) * chunk)
            k = zk_ref[rows, cols]
            qb = zq_ref[rows, cols].astype(BF16)
            kb = k.astype(BF16)
            vb = zv_ref[rows, cols].astype(BF16)
            a = _dot_nt(qb, kb) * din_ref[h]
            o = _dot(a.astype(BF16), vb) + _dot(qb, s_cur.astype(BF16)) * dq_ref[h]
            kd_t = (k * dk_ref[h]).T.astype(BF16)
            s_cur = math.exp(chunk * LOG_GAMMA[h]) * s_cur + _dot(kd_t, vb)

            on = _head_norm(o, gn_ref[:, cols])
            g = zg_ref[rows, cols]
            ro_ref[rows, cols] = (g * jax.nn.sigmoid(g) * on).astype(BF16)
        so_ref[h] = s_cur


def _retention_prompt(z3, ret_norm_g, layer, per_step):
    b, s, _ = z3.shape
    chunk = math.gcd(s, RET_CHUNK)
    per_step = math.gcd(s // chunk, per_step)
    rows = chunk * per_step
    zspec = lambda cb: pl.BlockSpec((None, rows, RET_WIDTH), lambda i, c: (i, c, cb))
    return pl.pallas_call(
        functools.partial(_ret_kernel, chunk=chunk, per_step=per_step),
        out_shape=(jax.ShapeDtypeStruct((b, s, RET_WIDTH), BF16),
                   jax.ShapeDtypeStruct((b, RET_HEADS, RET_DIM, RET_DIM), F32)),
        grid=(b, s // rows),
        in_specs=[
            zspec(COL_RQ // RET_WIDTH), zspec(COL_RK // RET_WIDTH),
            zspec(COL_RV // RET_WIDTH), zspec(COL_RG // RET_WIDTH),
            pl.BlockSpec((None, 1, RET_WIDTH), lambda i, c: (layer, 0, 0)),
        ],
        out_specs=(
            pl.BlockSpec((None, rows, RET_WIDTH), lambda i, c: (i, c, 0)),
            pl.BlockSpec((None, RET_HEADS, RET_DIM, RET_DIM), lambda i, c: (i, 0, 0, 0)),
        ),
        scratch_shapes=[
            pltpu.VMEM((RET_HEADS, chunk, chunk), F32),
            pltpu.VMEM((RET_HEADS, chunk, RET_DIM), F32),
            pltpu.VMEM((RET_HEADS, chunk, RET_DIM), F32),
        ],
        compiler_params=_params("parallel", "arbitrary"),
        name="retention",
    )(z3, z3, z3, z3, ret_norm_g)


def _ret_dec_kernel(z_ref, s0_ref, cos_ref, sin_ref, gn_ref, ro_ref, so_ref, *, nb):
    cos_t = cos_ref[...]
    sin_t = sin_ref[...]
    row = lax.broadcasted_iota(jnp.int32, (nb, RET_DIM), 0)
    for h in range(RET_HEADS):
        gamma = math.exp(LOG_GAMMA[h])
        q = _ret_rope(z_ref[:, COL_RQ + h * RET_DIM:COL_RQ + (h + 1) * RET_DIM], cos_t, sin_t)
        k = _ret_rope(z_ref[:, COL_RK + h * RET_DIM:COL_RK + (h + 1) * RET_DIM], cos_t, sin_t)
        k = k * (RET_DIM ** -0.5)
        v = z_ref[:, COL_RV + h * RET_DIM:COL_RV + (h + 1) * RET_DIM]
        qr, kr, vr = _bf16_round(q), _bf16_round(k), _bf16_round(v)
        qk = jnp.sum(qr * kr, axis=-1, keepdims=True)
        o = _bf16_round(qk) * vr
        cross = jnp.zeros((nb, RET_DIM), F32)
        for b in range(nb):
            s_old = s0_ref[b, h]
            res = _dot(qr.astype(BF16), s_old.astype(BF16))
            cross = cross + jnp.where(row == b, res, 0.0)
            k_col = jnp.broadcast_to(kr[b:b + 1, :], (RET_DIM, RET_DIM)).T
            so_ref[b, h] = gamma * s_old + k_col * vr[b:b + 1, :]
        o = o + cross * gamma
        cols = slice(h * RET_DIM, (h + 1) * RET_DIM)
        on = _head_norm(o, gn_ref[:, cols])
        g = z_ref[:, COL_RG + h * RET_DIM:COL_RG + (h + 1) * RET_DIM]
        ro_ref[:, cols] = g * jax.nn.sigmoid(g) * on


def _retention_decode(z, state, cos_t, sin_t, ret_norm_g, layer):
    nb = z.shape[0]
    sshape = (nb, RET_HEADS, RET_DIM, RET_DIM)
    return pl.pallas_call(
        functools.partial(_ret_dec_kernel, nb=nb),
        out_shape=(jax.ShapeDtypeStruct((nb, RET_WIDTH), F32),
                   jax.ShapeDtypeStruct(sshape, F32)),
        grid=(1,),
        in_specs=[
            pl.BlockSpec(z.shape, lambda i: (0, 0)),
            pl.BlockSpec((None,) + sshape, lambda i: (layer, 0, 0, 0, 0)),
            pl.BlockSpec((1, RET_DIM), lambda i: (0, 0)),
            pl.BlockSpec((1, RET_DIM), lambda i: (0, 0)),
            pl.BlockSpec((None, 1, RET_WIDTH), lambda i: (layer, 0, 0)),
        ],
        out_specs=(
            pl.BlockSpec((nb, RET_WIDTH), lambda i: (0, 0)),
            pl.BlockSpec(sshape, lambda i: (0, 0, 0, 0)),
        ),
        compiler_params=_params("arbitrary"),
        name="retention_decode",
    )(z, state, cos_t, sin_t, ret_norm_g)


ATT_HALF = 128
ATT_MAX_STRIDE = 4
N_HALF = ATT_GROUP // ATT_HALF
HEADS_PER_HALF = ATT_HALF // ATT_DIM


def _half_cols(hf):
    return slice(hf * ATT_HALF, (hf + 1) * ATT_HALF)


def _att_kernel(*refs, seq):
    zqkv = (refs[0:N_HALF], refs[N_HALF:2 * N_HALF], refs[2 * N_HALF:3 * N_HALF])
    ao_ref, o_ref, lse_ref, st_ref, ost_ref = refs[3 * N_HALF:]
    g = pl.program_id(1)
    qb = QUERY_BLOCK
    n_blocks = seq // qb

    tq = lax.broadcasted_iota(jnp.int32, (qb, qb), 0)
    tk = lax.broadcasted_iota(jnp.int32, (qb, qb), 1)
    cur_valid = tk <= tq
    prev_valid = tk >= tq
    lane = lax.broadcasted_iota(jnp.int32, (qb, ATT_HALF), 1)
    head_masks = [(lane // ATT_DIM) == hh for hh in range(HEADS_PER_HALF)]

    def group_body(gi, dil):
        d1 = min(dil, ATT_MAX_STRIDE)
        d2 = dil // d1
        staged = d2 > 1
        assert d2 <= ATT_MAX_STRIDE and d1 * d2 == dil
        sub_len = seq // d1

        def strided(start, n, stride):
            return pl.ds(start, n, stride=stride) if stride > 1 else pl.ds(start, n)

        if staged:
            for hf in range(N_HALF):
                for r1 in range(d1):
                    src = strided(r1, sub_len, d1)
                    for which in range(3):
                        st_ref[which, hf, r1] = zqkv[which][hf][src, :]

        def load(which, hf, cls, blk):
            if staged:
                rows = strided(cls // d1 + d2 * qb * blk, qb, d2)
                return st_ref[which, hf, cls % d1, rows, :].astype(BF16)
            rows = strided(cls + dil * qb * blk, qb, dil)
            return zqkv[which][hf][rows, :].astype(BF16)

        def store(which, hf, cls, blk, val):
            if staged:
                rows = strided(cls // d1 + d2 * qb * blk, qb, d2)
                ost_ref[which, hf, cls % d1, rows, :] = val
            else:
                rows = strided(cls + dil * qb * blk, qb, dil)
                (o_ref, lse_ref)[which][gi, hf, rows, :] = val

        def block_body(t, carry):
            cls = t % dil
            blk = t // dil
            pblk = jnp.maximum(blk - 1, 0)
            valid = jnp.concatenate([prev_valid & (blk > 0), cur_valid], axis=1)
            for hf in range(N_HALF):
                q = load(0, hf, cls, blk)
                keys = jnp.concatenate([load(1, hf, cls, pblk), load(1, hf, cls, blk)], axis=0)
                vals = jnp.concatenate([load(2, hf, cls, pblk), load(2, hf, cls, blk)], axis=0)
                vals_ones = jnp.concatenate([vals, jnp.ones_like(vals)], axis=1)
                num = jnp.zeros((qb, ATT_HALF), F32)
                den = jnp.ones((qb, ATT_HALF), F32)
                lse_acc = jnp.zeros((qb, ATT_HALF), F32)
                for hm in head_masks:
                    qh = jnp.where(hm, q, jnp.zeros_like(q))
                    s = jnp.where(valid, _dot_nt(qh, keys), MASK_VALUE)
                    m = jnp.max(s, axis=-1, keepdims=True)
                    e = jnp.exp(s - m).astype(BF16)
                    r = _dot(e, vals_ones)
                    l = r[:, ATT_HALF:]
                    num = jnp.where(hm, r[:, :ATT_HALF], num)
                    den = jnp.where(hm, l, den)
                    lse_acc = jnp.where(hm, m + jnp.log(l), lse_acc)
                store(0, hf, cls, blk, num / den)
                store(1, hf, cls, blk, lse_acc)
            return carry

        lax.fori_loop(0, n_blocks, block_body, 0, unroll=4)

        if staged:
            for hf in range(N_HALF):
                for r1 in range(d1):
                    dst = strided(r1, sub_len, d1)
                    o_ref[gi, hf, dst, :] = ost_ref[0, hf, r1]
                    lse_ref[gi, hf, dst, :] = ost_ref[1, hf, r1]

    for gi, (_, dil) in enumerate(ATT_PATTERNS):
        @pl.when(g == gi)
        def _(gi=gi, dil=dil):
            group_body(gi, dil)

    @pl.when(g == N_PAT - 1)
    def _():
        def merge_body(i, carry):
            r = pl.multiple_of(i * qb, qb)
            rows = pl.ds(r, qb)
            for hf in range(N_HALF):
                lses = [lse_ref[gi, hf, rows, :] for gi in range(N_PAT)]
                m = functools.reduce(jnp.maximum, lses)
                ws = [jnp.exp(l - m) for l in lses]
                den = functools.reduce(lambda x, y: x + y, ws)
                num = functools.reduce(lambda x, y: x + y,
                                       [w * o_ref[gi, hf, rows, :] for gi, w in enumerate(ws)])
                ao_ref[rows, _half_cols(hf)] = (num / den).astype(BF16)
            return carry

        lax.fori_loop(0, n_blocks, merge_body, 0)


def _attention_prompt(z3):
    b, s, _ = z3.shape
    assert all(s % (dil * QUERY_BLOCK) == 0 for _, dil in ATT_PATTERNS)
    zspec = lambda col, hf: pl.BlockSpec(
        (None, s, ATT_HALF), lambda i, g: (i, 0, col // ATT_HALF + N_HALF * g + hf))
    zspecs = [zspec(col, hf) for col in (COL_AQ, COL_AK, COL_AV) for hf in range(N_HALF)]
    return pl.pallas_call(
        functools.partial(_att_kernel, seq=s),
        out_shape=jax.ShapeDtypeStruct((b, s, ATT_GROUP), BF16),
        grid=(b, N_PAT),
        in_specs=zspecs,
        out_specs=pl.BlockSpec((None, s, ATT_GROUP), lambda i, g: (i, 0, 0)),
        scratch_shapes=[
            pltpu.VMEM((N_PAT, N_HALF, s, ATT_HALF), F32),
            pltpu.VMEM((N_PAT, N_HALF, s, ATT_HALF), F32),
            pltpu.VMEM((3, N_HALF, ATT_MAX_STRIDE, s // ATT_MAX_STRIDE, ATT_HALF), F32),
            pltpu.VMEM((2, N_HALF, ATT_MAX_STRIDE, s // ATT_MAX_STRIDE, ATT_HALF), F32),
        ],
        compiler_params=_params("parallel", "arbitrary"),
        name="dilated_attention",
    )(*([z3] * (3 * N_HALF)))


def _cache_shift_kernel(*refs):
    n = len(refs) // 2
    for src, dst in zip(refs[:n], refs[n:]):
        h, dim, w = src.shape
        dst[...] = pltpu.roll(src[...].reshape(h * dim, w), w - 1, axis=1).reshape(h, dim, w)


def _cache_shift(caches):
    depth, nb = caches[0].shape[:2]
    specs = [pl.BlockSpec((None, None) + c.shape[2:], lambda l, i: (l, i, 0, 0, 0)) for c in caches]
    return pl.pallas_call(
        _cache_shift_kernel,
        out_shape=tuple(jax.ShapeDtypeStruct(c.shape, c.dtype) for c in caches),
        grid=(depth, nb),
        in_specs=specs,
        out_specs=tuple(specs),
        compiler_params=_params("arbitrary", "arbitrary"),
        name="cache_shift",
    )(*caches)


def _row_to_col(row, eye):
    return jnp.sum(jnp.where(eye, row, 0.0), axis=1, keepdims=True)


def _col_to_row(col, eye):
    return jnp.sum(jnp.where(eye, col, 0.0), axis=0, keepdims=True)


def _att_dec_kernel(z_ref, c_ref, a_ref, b_ref, *refs):
    kc_refs = refs[0:N_PAT]
    vc_refs = refs[N_PAT:2 * N_PAT]
    kt_refs = refs[2 * N_PAT:3 * N_PAT]
    vt_refs = refs[3 * N_PAT:4 * N_PAT]
    ao_ref = refs[4 * N_PAT]
    kt_out_refs = refs[4 * N_PAT + 1:5 * N_PAT + 1]
    vt_out_refs = refs[5 * N_PAT + 1:6 * N_PAT + 1]
    rope =(c_ref[...], a_ref[...], b_ref[...])
    eye = (lax.broadcasted_iota(jnp.int32, (ATT_DIM, ATT_DIM), 0)
           == lax.broadcasted_iota(jnp.int32, (ATT_DIM, ATT_DIM), 1))
    last_lane = lax.broadcasted_iota(jnp.int32, (ATT_DIM, ATT_HALF), 1) == ATT_HALF - 1

    head_rows = []
    for h in range(ATT_HEADS):
        hf, lanes = h // HEADS_PER_HALF, slice((h % HEADS_PER_HALF) * ATT_DIM,
                                               (h % HEADS_PER_HALF + 1) * ATT_DIM)
        outs, lses = [], []
        for gi, (win, dil) in enumerate(ATT_PATTERNS):
            off = gi * ATT_GROUP + hf * ATT_HALF
            zrow = lambda col: z_ref[:, col + off:col + off + ATT_HALF]
            q = (_att_rope(zrow(COL_AQ), *rope) * (ATT_DIM ** -0.5))[:, lanes]
            k_new = _att_rope(zrow(COL_AK), *rope)[:, lanes]
            v_new = zrow(COL_AV)[:, lanes]
            k_col, v_col = _row_to_col(k_new, eye), _row_to_col(v_new, eye)
            keys, vals = kc_refs[gi][h], vc_refs[gi][h]
            w = keys.shape[1]
            pos = lax.broadcasted_iota(jnp.int32, (1, w), 1)
            s_old = jnp.sum(keys * _row_to_col(q, eye), axis=0, keepdims=True)
            s_old = jnp.where(pos % dil == 0, s_old, MASK_VALUE)
            s_new = jnp.sum(q * k_new, axis=1, keepdims=True)
            m = jnp.maximum(jnp.max(s_old, axis=1, keepdims=True), s_new)
            e_old = jnp.exp(s_old - m)
            e_new = jnp.exp(s_new - m)
            l = jnp.sum(e_old, axis=1, keepdims=True) + e_new
            pv = jnp.sum(vals * e_old, axis=1, keepdims=True) + e_new * v_col
            outs.append(pv / l)
            lses.append(m + jnp.log(l))
            kt_out_refs[gi][h] = jnp.where(last_lane, k_col, kt_refs[gi][h])
            vt_out_refs[gi][h] = jnp.where(last_lane, v_col, vt_refs[gi][h])

        m = functools.reduce(jnp.maximum, lses)
        ws = [jnp.exp(l - m) for l in lses]
        den = functools.reduce(lambda x, y: x + y, ws)
        num = functools.reduce(lambda x, y: x + y, [w * o for w, o in zip(ws, outs)])
        head_rows.append(_col_to_row(num / den, eye))
    ao_ref[...] = jnp.concatenate(head_rows, axis=1)


def _attention_decode(z, tabs, k_caches, v_caches, k_next, v_next, layer):
    nb = z.shape[0]
    n_fixed = 4
    cspec = lambda c: pl.BlockSpec((None, None) + c.shape[2:], lambda i: (layer, i, 0, 0, 0))
    tail = lambda c: pl.BlockSpec((None, None) + c.shape[2:4] + (ATT_HALF,),
                                  lambda i: (layer, i, 0, 0, c.shape[4] // ATT_HALF - 1))
    tspec = pl.BlockSpec((1, ATT_HALF), lambda i: (0, 0))
    caches = list(k_caches) + list(v_caches)
    nexts = list(k_next) + list(v_next)
    return pl.pallas_call(
        _att_dec_kernel,
        out_shape=tuple([jax.ShapeDtypeStruct((nb, 1, ATT_GROUP), F32)]
                        + [jax.ShapeDtypeStruct(c.shape, c.dtype) for c in nexts]),
        grid=(nb,),
        in_specs=[pl.BlockSpec((None, 1, z.shape[1]), lambda i: (i, 0, 0)), tspec, tspec, tspec]
                 + [cspec(c) for c in caches] + [tail(c) for c in nexts],
        out_specs=tuple([pl.BlockSpec((None, 1, ATT_GROUP), lambda i: (i, 0, 0))]
                        + [tail(c) for c in nexts]),
        input_output_aliases={n_fixed + len(caches) + j: 1 + j for j in range(len(nexts))},
        compiler_params=_params("arbitrary"),
        name="dilated_attention_decode",
    )(z.reshape(nb, 1, z.shape[1]), *tabs, *caches, *nexts)


def _pool_kernel(u_ref, w_ref, sc_ref, po_ref, a_ref, b_ref, *, seq):
    g = pl.program_id(1)
    body = pl.ds(POOL_PAD, seq)

    def window_mean_minus_token(win):
        x = u_ref[...]
        a_ref[0:POOL_PAD, :] = jnp.zeros((POOL_PAD, POOL_GROUP), F32)
        b_ref[0:POOL_PAD, :] = jnp.zeros((POOL_PAD, POOL_GROUP), F32)
        a_ref[body, :] = x
        src, dst = a_ref, b_ref
        k = 1
        while k < win:
            dst[body, :] = src[body, :] + src[pl.ds(POOL_PAD - k, seq), :]
            src, dst = dst, src
            k *= 2
        t = lax.broadcasted_iota(jnp.int32, (seq, POOL_GROUP), 0)
        cnt = jnp.minimum(t + 1, win).astype(F32)
        pooled = src[body, :] / cnt - x
        y = _dot(pooled.astype(BF16), w_ref[...].astype(BF16)) * sc_ref[...]
        po_ref[...] = y.astype(BF16)

    for gi, win in enumerate(POOL_WINDOWS):
        @pl.when(g == gi)
        def _(win=win):
            window_mean_minus_token(win)


def _pool_prompt(z3, w_pool, pool_scale, layer):
    b, s, _ = z3.shape
    ng = len(POOL_WINDOWS)
    return pl.pallas_call(
        functools.partial(_pool_kernel, seq=s),
        out_shape=jax.ShapeDtypeStruct((b, s, POOL_WIDTH), BF16),
        grid=(b, ng),
        in_specs=[
            pl.BlockSpec((None, s, POOL_GROUP), lambda i, g: (i, 0, COL_PU // POOL_GROUP + g)),
            pl.BlockSpec((None, None, POOL_GROUP, POOL_GROUP), lambda i, g: (layer, g, 0, 0)),
            pl.BlockSpec((None, 1, POOL_GROUP), lambda i, g: (layer, 0, g)),
        ],
        out_specs=pl.BlockSpec((None, s, POOL_GROUP), lambda i, g: (i, 0, g)),
        scratch_shapes=[pltpu.VMEM((POOL_PAD + s, POOL_GROUP), F32),
                        pltpu.VMEM((POOL_PAD + s, POOL_GROUP), F32)],
        compiler_params=_params("parallel", "arbitrary"),
        name="pool_mixer",
    )(z3, w_pool, pool_scale)


def _pool_dec_kernel(z_ref, buf_ref, w_ref, sc_ref, po_ref, bo_ref, pooled_ref, *, nb):
    row = lax.broadcasted_iota(jnp.int32, (POOL_BUF, POOL_GROUP), 0)
    for b in range(nb):
        u = z_ref[b:b + 1, COL_PU:COL_PU + POOL_WIDTH]
        old = buf_ref[b]
        for gi, win in enumerate(POOL_WINDOWS):
            cols = slice(gi * POOL_GROUP, (gi + 1) * POOL_GROUP)
            tail = jnp.where(row >= POOL_BUF - (win - 1), old[:, cols], 0.0)
            total = jnp.sum(tail, axis=0, keepdims=True) + u[:, cols]
            pooled_ref[b:b + 1, cols] = total / float(win) - u[:, cols]
        bo_ref[b, 0:POOL_BUF - 1, :] = old[1:POOL_BUF, :]
        bo_ref[b, POOL_BUF - 1:POOL_BUF, :] = u
    for gi in range(len(POOL_WINDOWS)):
        cols = slice(gi * POOL_GROUP, (gi + 1) * POOL_GROUP)
        y = _dot(pooled_ref[:, cols].astype(BF16), w_ref[gi].astype(BF16))
        po_ref[:, cols] = y * sc_ref[:, cols]


def _pool_decode(z, cache_pool, w_pool, pool_scale, layer):
    nb = z.shape[0]
    ng = len(POOL_WINDOWS)
    bshape = (nb, POOL_BUF, POOL_WIDTH)
    return pl.pallas_call(
        functools.partial(_pool_dec_kernel, nb=nb),
        out_shape=(jax.ShapeDtypeStruct((nb, POOL_WIDTH), F32),
                   jax.ShapeDtypeStruct(bshape, F32)),
        grid=(1,),
        in_specs=[
            pl.BlockSpec(z.shape, lambda i: (0, 0)),
            pl.BlockSpec((None,) + bshape, lambda i: (layer, 0, 0, 0)),
            pl.BlockSpec((None, ng, POOL_GROUP, POOL_GROUP), lambda i: (layer, 0, 0, 0)),
            pl.BlockSpec((None, 1, POOL_WIDTH), lambda i: (layer, 0, 0)),
        ],
        out_specs=(pl.BlockSpec((nb, POOL_WIDTH), lambda i: (0, 0)),
                   pl.BlockSpec(bshape, lambda i: (0, 0, 0))),
        scratch_shapes=[pltpu.VMEM((nb, POOL_WIDTH), F32)],
        compiler_params=_params("arbitrary"),
        name="pool_mixer_decode",
    )(z, cache_pool, w_pool, pool_scale)


FFN_COLS = 512
RET_CHUNKS_PER_STEP = 8


def _prompt_tiles(seq):
    return {
        "ffn_rows": math.gcd(seq, 1024),
        "inproj_rows": math.gcd(seq, 256),
        "outproj_rows": math.gcd(seq, 512),
    }


def _trunk_prompt(x, mod, p, bank, final_g):
    b, s, d = x.shape
    depth = p["w_in"].shape[0]
    pos = jnp.arange(s, dtype=F32)
    ret_tabs = _ret_rope_tables(pos)
    att_tabs = _att_rope_tables(pos)
    keep = tuple(min(win, s) for win, _ in ATT_PATTERNS)
    t = _prompt_tiles(s)
    xf = x.reshape(b * s, d)
    rets, pools = [], []
    ks = [[] for _ in ATT_PATTERNS]
    vs = [[] for _ in ATT_PATTERNS]
    for l in range(depth):
        while ("ffn1", l) not in bank:
            yield
        xf, *bank["ffn2", l] = _ffn(xf, mod, l, 0, s, p["norm_g"], *bank["ffn1", l], None,
                                    t["ffn_rows"], FFN_COLS,
                                    cast_next=(p["w2_gate"], p["w2_up"], p["w2_down"], l))
        while ("w_in", l) not in bank:
            yield
        z = _inproj(xf, mod, l, s, p["norm_g"], bank["w_in", l], ret_tabs + att_tabs, t["inproj_rows"])
        z3 = z.reshape(b, s, IN_WIDTH)
        ro, ret_s = _retention_prompt(z3, p["ret_norm_g"], l, RET_CHUNKS_PER_STEP)
        ao = _attention_prompt(z3)
        po = _pool_prompt(z3, p["w_pool"], p["pool_scale"], l)
        xf = _outproj(xf, mod, l, s, ro.reshape(b * s, -1), ao.reshape(b * s, -1),
                      po.reshape(b * s, -1), p["w_out"], t["outproj_rows"])
        if l + 1 < depth:
            xf, *bank["ffn1", l + 1] = _ffn(xf, mod, l, 2, s, p["norm_g"], *bank["ffn2", l], None,
                                            t["ffn_rows"], FFN_COLS,
                                            cast_next=(p["w1_gate"], p["w1_up"], p["w1_down"], l + 1))
        else:
            xf = _ffn(xf, mod, l, 2, s, p["norm_g"], *bank["ffn2", l], final_g,
                      t["ffn_rows"], FFN_COLS)
        rets.append(ret_s)
        pools.append(z3[:, s - POOL_BUF:, COL_PU:])
        for g in range(N_PAT):
            for col, dst in ((COL_AK, ks), (COL_AV, vs)):
                c0 = col + g * ATT_GROUP
                dst[g].append(z3[:, s - keep[g]:, c0:c0 + ATT_GROUP]
                              .reshape(b, keep[g], ATT_HEADS, ATT_DIM))
    y = xf.reshape(b, s, d)
    return (y, jnp.stack(rets), [jnp.stack(k) for k in ks], [jnp.stack(v) for v in vs],
            jnp.stack(pools))


def _trunk_decode(x, mod, pos0, caches, p, bank, final_g):
    nb, s, d = x.shape
    assert s == 1, "decode trunk handles one new token per batch row"
    depth = p["w_in"].shape[0]
    state_ret, cks, cvs, cpool = caches
    pos = pos0 + jnp.arange(s, dtype=F32)
    ret_tabs = _ret_rope_tables(pos)
    att_tabs = _att_rope_tables(pos)
    feature_major = lambda c: jnp.transpose(c, (0, 1, 3, 4, 2))
    position_major = lambda c: jnp.transpose(c, (0, 1, 4, 2, 3))
    cks = [feature_major(c) for c in cks]
    cvs = [feature_major(c) for c in cvs]
    advanced = _cache_shift(cks + cvs)
    k_next, v_next = list(advanced[:N_PAT]), list(advanced[N_PAT:])
    xf = x.reshape(nb, d)
    rets, pools = [], []
    for l in range(depth):
        if l == 0:
            xf, *bank["ffn1", l] = _ffn(xf, mod, l, 0, 1, p["norm_g"], p["w1_gate"], p["w1_up"],
                                        p["w1_down"], None, nb, FFN_COLS, emit_bf16=True)
        else:
            while ("ffn1", l) not in bank:
                yield
            xf = _ffn(xf, mod, l, 0, 1, p["norm_g"], *bank["ffn1", l], None, nb, FFN_COLS)
        z, bank["w_in", l] = _inproj_rows(xf, mod, l, p["norm_g"], p["w_in"])
        ro, ret_s = _retention_decode(z, state_ret, *ret_tabs, p["ret_norm_g"], l)
        att = _attention_decode(z, att_tabs, cks, cvs, k_next, v_next, l)
        ao, k_next, v_next = att[0].reshape(nb, ATT_GROUP), list(att[1:1 + N_PAT]), list(att[1 + N_PAT:])
        po, pool_new = _pool_decode(z, cpool, p["w_pool"], p["pool_scale"], l)
        xf = _outproj(xf, mod, l, 1, ro, ao, po, p["w_out"], nb)
        while ("ffn2", l) not in bank:
            yield
        xf = _ffn(xf, mod, l, 2, 1, p["norm_g"], *bank["ffn2", l],
                  final_g if l == depth - 1 else None, nb, FFN_COLS)
        rets.append(ret_s)
        pools.append(pool_new)
    y = xf.reshape(nb, s, d)
    return (y, jnp.stack(rets), [position_major(k) for k in k_next],
            [position_major(v) for v in v_next], jnp.stack(pools))


def _run_trunks(*trunks):
    results = [None] * len(trunks)
    running = dict(enumerate(trunks))
    while running:
        for i, trunk in list(running.items()):
            try:
                next(trunk)
            except StopIteration as done:
                results[i] = done.value
                del running[i]
    return results


def kernel(x_prompt, x_sample, state_ret, cache_k_w128, cache_v_w128, cache_k_w512, cache_v_w512,
           cache_k_w2048, cache_v_w2048, cache_pool, c_prompt, c_sample, w_ada, b_ada, norm_g, w_in,
           ret_norm_g, w_pool, pool_scale, w_out, w1_gate, w1_up, w1_down, w2_gate, w2_up, w2_down,
           final_norm_g):
    depth, d = norm_g.shape[0], norm_g.shape[-1]
    n_pr, n_dec = c_prompt.shape[0], c_sample.shape[0]
    p = {
        "norm_g": norm_g.reshape(depth, N_SUB, 1, d),
        "w_in": w_in,
        "ret_norm_g": ret_norm_g.reshape(depth, 1, RET_WIDTH),
        "w_pool": w_pool,
        "pool_scale": pool_scale.reshape(depth, 1, POOL_WIDTH),
        "w_out": w_out,
        "w1_gate": w1_gate, "w1_up": w1_up, "w1_down": w1_down,
        "w2_gate": w2_gate, "w2_up": w2_up, "w2_down": w2_down,
    }
    final_g = final_norm_g.reshape(1, d)

    pad = (-(n_dec + n_pr)) % SUBLANES
    c_all = jnp.concatenate([c_sample, c_prompt, jnp.zeros((pad, d), F32)], axis=0)
    mod_dec, mod_pr = _ada(c_all, n_dec, n_pr, w_ada, b_ada)

    caches = (state_ret,
              (cache_k_w128, cache_k_w512, cache_k_w2048),
              (cache_v_w128, cache_v_w512, cache_v_w2048),
              cache_pool)
    bank = {}
    (y_s, ret_s, ks, vs, pool_s), (y_p, ret_p, kp, vp, pool_p) = _run_trunks(
        _trunk_decode(x_sample, mod_dec, float(PAST_LEN), caches, p, bank, final_g),
        _trunk_prompt(x_prompt, mod_pr, p, bank, final_g))
    return (y_p, y_s, ret_p, ret_s,
            kp[0], ks[0], vp[0], vs[0],
            kp[1], ks[1], vp[1], vs[1],
            kp[2], ks[2], vp[2], vs[2],
            pool_p, pool_s)
```

```python
import functools
import math

import jax
import jax.numpy as jnp
from jax import lax
from jax.experimental import pallas as pl
from jax.experimental.pallas import tpu as pltpu

F32 = jnp.float32
BF16 = jnp.bfloat16

RET_HEADS = 6
RET_DIM = 128
RET_WIDTH = RET_HEADS * RET_DIM
RET_CHUNK = 128
RET_THETA = 10000.0
ATT_HEADS = 4
ATT_DIM = 64
ATT_GROUP = ATT_HEADS * ATT_DIM
ATT_PATTERNS = ((128, 1), (512, 4), (2048, 16))
N_PAT = len(ATT_PATTERNS)
ATT_WIDTH = N_PAT * ATT_GROUP
ROPE_THETA = 500000.0
ROPE_DIMS = ATT_DIM // 4
ROPE_HALF = ROPE_DIMS // 2
QUERY_BLOCK = 128
POOL_WINDOWS = (2, 4, 8, 16)
POOL_GROUP = 128
POOL_WIDTH = len(POOL_WINDOWS) * POOL_GROUP
POOL_BUF = max(POOL_WINDOWS) - 1
POOL_PAD = 16
N_SUB = 3
PAST_LEN = 16384
HALF_STEP = 0.5
EPS = 1e-6
MASK_VALUE = -1e30

COL_RQ, COL_RK, COL_RV, COL_RG = 0, RET_WIDTH, 2 * RET_WIDTH, 3 * RET_WIDTH
COL_AQ = 4 * RET_WIDTH
COL_AK = COL_AQ + ATT_WIDTH
COL_AV = COL_AK + ATT_WIDTH
COL_PU = COL_AV + ATT_WIDTH
IN_WIDTH = COL_PU + POOL_WIDTH

V7X_VMEM_BYTES = 64 * 1024 * 1024
VMEM_LIMIT = V7X_VMEM_BYTES - 4 * 1024 * 1024
SUBLANES = 8
LANES = 128
ROW_CHUNK = 128
RESULT_COLS = 512

LOG_GAMMA = tuple(math.log1p(-(2.0 ** (-5.0 - h))) for h in range(RET_HEADS))


def _params(*sem):
    return pltpu.CompilerParams(dimension_semantics=sem, vmem_limit_bytes=VMEM_LIMIT)


def _dot(a, b):
    return jnp.dot(a, b, preferred_element_type=F32)


def _dot_nt(a, b):
    return lax.dot_general(a, b, (((1,), (1,)), ((), ())), preferred_element_type=F32)


def _bf16_round(x):
    return x.astype(BF16).astype(F32)


def _ada_kernel(c_ref, w_ref, b_ref, od_ref, op_ref, *, n_dec, n_pr):
    c = c_ref[...]
    a = (c * jax.nn.sigmoid(c)).astype(BF16)
    res = _dot(a, w_ref[...].astype(BF16)) + b_ref[...]
    od_ref[...] = res[0:n_dec]
    for b in range(n_pr):
        op_ref[b] = res[n_dec + b:n_dec + b + 1]


def _ada(c_all, n_dec, n_pr, w_ada, b_ada, tn=1024):
    depth, d, n = w_ada.shape
    rows = c_all.shape[0]
    per = d // tn
    return pl.pallas_call(
        functools.partial(_ada_kernel, n_dec=n_dec, n_pr=n_pr),
        out_shape=(jax.ShapeDtypeStruct((depth, N_SUB * 3, n_dec, d), F32),
                   jax.ShapeDtypeStruct((depth, N_SUB * 3, n_pr, 1, d), F32)),
        grid=(depth, n // tn),
        in_specs=[
            pl.BlockSpec((rows, d), lambda l, j: (0, 0)),
            pl.BlockSpec((None, d, tn), lambda l, j: (l, 0, j)),
            pl.BlockSpec((None, 1, tn), lambda l, j: (l, 0, j)),
        ],
        out_specs=(
            pl.BlockSpec((None, None, n_dec, tn), lambda l, j: (l, j // per, 0, j % per)),
            pl.BlockSpec((None, None, n_pr, 1, tn), lambda l, j: (l, j // per, 0, 0, j % per)),
        ),
        compiler_params=_params("arbitrary", "arbitrary"),
        name="ada_mod",
    )(c_all, w_ada, b_ada.reshape(depth, 1, n))


def _rmsnorm(x, g):
    ms = jnp.mean(x * x, axis=-1, keepdims=True)
    return x * lax.rsqrt(ms + EPS) * g


def _row_chunks(tm):
    rc = min(tm, ROW_CHUNK)
    return rc, tm // rc


def _mod_rows(ref, r, rc):
    return ref[...] if ref.shape[0] == 1 else ref[pl.ds(r, rc), :]


def _prenorm_to(h_ref, x_ref, g_ref, sh_ref, sc_ref, unrolled=False):
    rc, n = _row_chunks(x_ref.shape[0])

    def body(i, carry):
        r = i * rc if unrolled else pl.multiple_of(i * rc, rc)
        y = _rmsnorm(x_ref[pl.ds(r, rc), :], g_ref[...])
        h = y * (1.0 + _mod_rows(sc_ref, r, rc)) + _mod_rows(sh_ref, r, rc)
        h_ref[pl.ds(r, rc), :] = h.astype(BF16)
        return carry

    if unrolled:
        for i in range(n):
            body(i, 0)
    else:
        lax.fori_loop(0, n, body, 0)


def _mod_specs(mod, layer, sub, rows_per_batch, tm, grid_rank):
    d = mod.shape[-1]
    specs = []
    for k in range(3):
        j = sub * 3 + k
        if mod.ndim == 5:
            if grid_rank == 2:
                idx = (lambda j: lambda i, f: (layer, j, (i * tm) // rows_per_batch, 0, 0))(j)
            else:
                idx = (lambda j: lambda i: (layer, j, (i * tm) // rows_per_batch, 0, 0))(j)
            specs.append(pl.BlockSpec((None, None, None, 1, d), idx))
        else:
            if grid_rank == 2:
                idx = (lambda j: lambda i, f: (layer, j, 0, 0))(j)
            else:
                idx = (lambda j: lambda i: (layer, j, 0, 0))(j)
            specs.append(pl.BlockSpec((None, None, tm, d), idx))
    return specs


def _ffn_kernel(x_ref, sh_ref, sc_ref, gt_ref, g_ref, wg_ref, wu_ref, wd_ref, *rest,
                n_f, final_norm, emit_bf16, cast_next):
    rest = list(rest)
    fg_ref = rest.pop(0) if final_norm else None
    cast_in = [rest.pop(0) for _ in range(3)] if cast_next else []
    o_ref = rest.pop(0)
    wb_refs = [rest.pop(0) for _ in range(3)] if emit_bf16 else None
    cast_out = [rest.pop(0) for _ in range(3)] if cast_next else []
    h_ref, = rest
    f = pl.program_id(1)

    assert n_f >= 2
    fuse_residual = not final_norm

    def step(first, last):
        if first:
            _prenorm_to(h_ref, x_ref, g_ref, sh_ref, sc_ref, unrolled=True)
        wg = wg_ref[...].astype(BF16)
        wu = wu_ref[...].astype(BF16)
        wd = wd_ref[...].astype(BF16)
        if emit_bf16:
            wb_refs[0][...] = wg
            wb_refs[1][...] = wu
            wb_refs[2][...] = wd
        for src, dst in zip(cast_in, cast_out):
            dst[...] = src[...].astype(BF16)
        h = h_ref[...]
        gate = _dot(h, wg)
        up = _dot(h, wu)
        act = (gate * jax.nn.sigmoid(gate) * up).astype(BF16)
        d = o_ref.shape[1]
        dc = min(d, RESULT_COLS)
        for c0 in range(0, d, dc):
            cols = slice(c0, c0 + dc)
            y = _dot(act, wd[:, cols])
            if first:
                o_ref[:, cols] = y
            elif last and fuse_residual:
                o_ref[:, cols] = x_ref[:, cols] + HALF_STEP * gt_ref[:, cols] * (o_ref[:, cols] + y)
            else:
                o_ref[:, cols] += y

    pl.when(f == 0)(lambda: step(True, False))
    pl.when((f > 0) & (f < n_f - 1))(lambda: step(False, False))
    pl.when(f == n_f - 1)(lambda: step(False, True))

    if not fuse_residual:
        @pl.when(f == n_f - 1)
        def _():
            rc, n = _row_chunks(x_ref.shape[0])

            def body(i, carry):
                r = pl.multiple_of(i * rc, rc)
                rows = pl.ds(r, rc)
                out = x_ref[rows, :] + HALF_STEP * _mod_rows(gt_ref, r, rc) * o_ref[rows, :]
                o_ref[rows, :] = _rmsnorm(out, fg_ref[...])
                return carry

            lax.fori_loop(0, n, body, 0)


def _weight_spec(w, layer, block, index):
    if w.ndim == 3:
        return pl.BlockSpec((None,) + block, lambda i, j: (layer,) + index(i, j))
    return pl.BlockSpec(block, index)


def _row_slices(rows, max_steps):
    steps = max(s for s in range(1, max_steps + 1) if rows % s == 0 and (rows // s) % 16 == 0)
    return steps, rows // steps


def _ffn(x, mod, layer, sub, rows_per_batch, norm_g, wg, wu, wd, final_g, tm, tf, emit_bf16=False,
         cast_next=None):
    m, d = x.shape
    d_ff = wg.shape[-1]
    assert m % tm == 0 and d_ff % tf == 0
    assert not emit_bf16 or m == tm, "each weight block must be visited exactly once"
    n_f = d_ff // tf
    final_norm = final_g is not None
    col_block = lambda i, j: (0, j)
    row_block = lambda i, j: (j, 0)
    in_specs = [pl.BlockSpec((tm, d), lambda i, j: (i, 0))]
    in_specs += _mod_specs(mod, layer, sub, rows_per_batch, tm, 2)
    in_specs += [
        pl.BlockSpec((None, None, 1, d), lambda i, j: (layer, sub, 0, 0)),
        _weight_spec(wg, layer, (d, tf), col_block),
        _weight_spec(wu, layer, (d, tf), col_block),
        _weight_spec(wd, layer, (tf, d), row_block),
    ]
    args = [x, mod, mod, mod, norm_g, wg, wu, wd]
    if final_norm:
        in_specs.append(pl.BlockSpec((1, d), lambda i, j: (0, 0)))
        args.append(final_g)
    out_shape = [jax.ShapeDtypeStruct((m, d), F32)]
    out_specs = [pl.BlockSpec((tm, d), lambda i, j: (i, 0))]
    if emit_bf16:
        out_shape += [jax.ShapeDtypeStruct((d, d_ff), BF16), jax.ShapeDtypeStruct((d, d_ff), BF16),
                      jax.ShapeDtypeStruct((d_ff, d), BF16)]
        out_specs += [pl.BlockSpec((d, tf), col_block), pl.BlockSpec((d, tf), col_block),
                      pl.BlockSpec((tf, d), row_block)]
    if cast_next is not None:
        *next_w, layer_next = cast_next
        n_steps = (m // tm) * n_f
        for w in next_w:
            rows, cols = w.shape[1:]
            steps, blk = _row_slices(rows, n_steps)
            at = (lambda steps: lambda i, j: jnp.minimum(i * n_f + j, steps - 1))(steps)
            in_specs.append(pl.BlockSpec((None, blk, cols),
                                         (lambda at: lambda i, j: (layer_next, at(i, j), 0))(at)))
            args.append(w)
            out_shape.append(jax.ShapeDtypeStruct((rows, cols), BF16))
            out_specs.append(pl.BlockSpec((blk, cols), (lambda at: lambda i, j: (at(i, j), 0))(at)))
    many = emit_bf16 or cast_next is not None
    out = pl.pallas_call(
        functools.partial(_ffn_kernel, n_f=n_f, final_norm=final_norm, emit_bf16=emit_bf16,
                          cast_next=cast_next is not None),
        out_shape=tuple(out_shape),
        grid=(m // tm, n_f),
        in_specs=in_specs,
        out_specs=tuple(out_specs),
        scratch_shapes=[pltpu.VMEM((tm, d), BF16)],
        compiler_params=_params("arbitrary", "arbitrary"),
        name="ffn",
    )(*args)
    return out if many else out[0]


def _rotate_projection_block(y, col, ret_tabs, att_tabs):
    if col < COL_RV:
        y = _ret_rope(y, *ret_tabs)
        return y * (RET_DIM ** -0.5) if col >= COL_RK else y
    if COL_AQ <= col < COL_AV:
        y = _att_rope(y, *att_tabs)
        return y * (ATT_DIM ** -0.5) if col < COL_AK else y
    return y


def _inproj_kernel(x_ref, sh_ref, sc_ref, g_ref, w_ref, rc_ref, rs_ref, ac_ref, aa_ref, ab_ref,
                   o_ref, h_ref):
    _prenorm_to(h_ref, x_ref, g_ref, sh_ref, sc_ref, unrolled=True)
    h = h_ref[...]
    ret_tabs = (rc_ref[...], rs_ref[...])
    att_tabs = (ac_ref[...], aa_ref[...], ab_ref[...])
    n = o_ref.shape[1]
    for c0 in range(0, n, RESULT_COLS):
        c1 = min(c0 + RESULT_COLS, n)
        y = _dot(h, w_ref[:, c0:c1])
        for b0 in range(0, c1 - c0, LANES):
            o_ref[:, c0 + b0:c0 + b0 + LANES] = _rotate_projection_block(
                y[:, b0:b0 + LANES], c0 + b0, ret_tabs, att_tabs)


def _inproj(x, mod, layer, rows_per_batch, norm_g, w_in_bf16, rope_tabs, tm):
    m, d = x.shape
    n = w_in_bf16.shape[-1]
    assert m % tm == 0 and rows_per_batch % tm == 0 and n % LANES == 0
    sh, sc, _ = _mod_specs(mod, layer, 1, rows_per_batch, tm, 1)
    tiles_per_batch = rows_per_batch // tm
    return pl.pallas_call(
        _inproj_kernel,
        out_shape=jax.ShapeDtypeStruct((m, n), F32),
        grid=(m // tm,),
        in_specs=[
            pl.BlockSpec((tm, d), lambda i: (i, 0)),
            sh, sc,
            pl.BlockSpec((None, None, 1, d), lambda i: (layer, 1, 0, 0)),
            pl.BlockSpec((d, n), lambda i: (0, 0), pipeline_mode=pl.Buffered(1)),
        ] + [pl.BlockSpec((tm, LANES), lambda i: (i % tiles_per_batch, 0))] * len(rope_tabs),
        out_specs=pl.BlockSpec((tm, n), lambda i: (i, 0)),
        scratch_shapes=[pltpu.VMEM((tm, d), BF16)],
        compiler_params=_params("parallel"),
        name="in_proj",
    )(x, mod, mod, norm_g, w_in_bf16, *rope_tabs)


def _inproj_rows_kernel(x_ref, sh_ref, sc_ref, g_ref, w_ref, o_ref, *rest, tk, emit):
    wb_ref, h_ref = rest if emit else (None,) + rest
    k = pl.program_id(0)

    @pl.when(k == 0)
    def _():
        h = _rmsnorm(x_ref[...], g_ref[...]) * (1.0 + sc_ref[...]) + sh_ref[...]
        for c in range(h_ref.shape[0]):
            h_ref[c] = h[:, c * tk:(c + 1) * tk].astype(BF16)
        o_ref[...] = jnp.zeros_like(o_ref)

    wb = w_ref[...].astype(BF16)
    if emit:
        wb_ref[...] = wb
    o_ref[...] += _dot(h_ref[k], wb)


def _inproj_rows(x, mod, layer, norm_g, w_in, tk=512):
    m, d = x.shape
    n = w_in.shape[-1]
    emit = w_in.ndim == 3
    assert d % tk == 0
    sh, sc, _ = _mod_specs(mod, layer, 1, 1, m, 1)
    w_spec = (pl.BlockSpec((None, tk, n), lambda k: (layer, k, 0)) if emit
              else pl.BlockSpec((tk, n), lambda k: (k, 0)))
    out_shape = [jax.ShapeDtypeStruct((m, n), F32)]
    out_specs = [pl.BlockSpec((m, n), lambda k: (0, 0))]
    if emit:
        out_shape.append(jax.ShapeDtypeStruct((d, n), BF16))
        out_specs.append(pl.BlockSpec((tk, n), lambda k: (k, 0)))
    out = pl.pallas_call(
        functools.partial(_inproj_rows_kernel, tk=tk, emit=emit),
        out_shape=tuple(out_shape),
        grid=(d // tk,),
        in_specs=[
            pl.BlockSpec((m, d), lambda k: (0, 0)),
            sh, sc,
            pl.BlockSpec((None, None, 1, d), lambda k: (layer, 1, 0, 0)),
            w_spec,
        ],
        out_specs=tuple(out_specs),
        scratch_shapes=[pltpu.VMEM((d // tk, m, tk), BF16)],
        compiler_params=_params("arbitrary"),
        name="in_proj_rows",
    )(x, mod, mod, norm_g, w_in)
    return out if emit else out[0]


def _outproj_kernel(x_ref, gt_ref, ro_ref, ao_ref, po_ref, w_ref, o_ref, wb_ref):
    @pl.when(pl.program_id(0) == 0)
    def _():
        wb_ref[...] = w_ref[...].astype(BF16)

    mix = jnp.concatenate([ro_ref[...].astype(BF16), ao_ref[...].astype(BF16),
                           po_ref[...].astype(BF16)], axis=1)
    o_ref[...] = x_ref[...] + gt_ref[...] * _dot(mix, wb_ref[...])


def _outproj(x, mod, layer, rows_per_batch, ro, ao, po, w_out, tm):
    m, d = x.shape
    k = w_out.shape[1]
    _, _, gt = _mod_specs(mod, layer, 1, rows_per_batch, tm, 1)
    return pl.pallas_call(
        _outproj_kernel,
        out_shape=jax.ShapeDtypeStruct((m, d), F32),
        grid=(m // tm,),
        in_specs=[
            pl.BlockSpec((tm, d), lambda i: (i, 0)),
            gt,
            pl.BlockSpec((tm, RET_WIDTH), lambda i: (i, 0)),
            pl.BlockSpec((tm, ATT_GROUP), lambda i: (i, 0)),
            pl.BlockSpec((tm, POOL_WIDTH), lambda i: (i, 0)),
            pl.BlockSpec((None, k, d), lambda i: (layer, 0, 0), pipeline_mode=pl.Buffered(1)),
        ],
        out_specs=pl.BlockSpec((tm, d), lambda i: (i, 0)),
        scratch_shapes=[pltpu.VMEM((k, d), BF16)],
        compiler_params=_params("arbitrary"),
        name="out_proj",
    )(x, mod, ro, ao, po, w_out)


def _ret_rope_tables(pos):
    half = RET_DIM // 2
    freq = jnp.power(jnp.float32(RET_THETA), -jnp.arange(half, dtype=F32) / half)
    ang = pos[:, None] * freq[None, :]
    cos, sin = jnp.cos(ang), jnp.sin(ang)
    return jnp.concatenate([cos, cos], axis=-1), jnp.concatenate([-sin, sin], axis=-1)


def _att_rope_tables(pos):
    freq = jnp.power(jnp.float32(ROPE_THETA), -jnp.arange(ROPE_HALF, dtype=F32) / ROPE_HALF)
    ang = pos[:, None] * freq[None, :]
    cos, sin = jnp.cos(ang), jnp.sin(ang)
    s = pos.shape[0]
    rest = ATT_DIM - ROPE_DIMS
    c = jnp.concatenate([cos, cos, jnp.ones((s, rest), F32)], axis=-1)
    a = jnp.concatenate([-sin, jnp.zeros((s, ATT_DIM - ROPE_HALF), F32)], axis=-1)
    b = jnp.concatenate([jnp.zeros((s, ROPE_HALF), F32), sin, jnp.zeros((s, rest), F32)], axis=-1)
    tile = lambda t: jnp.tile(t, (1, LANES // ATT_DIM))
    return tile(c), tile(a), tile(b)


def _ret_rope(x, cos_t, sin_t):
    return x * cos_t + pltpu.roll(x, RET_DIM // 2, axis=1) * sin_t


def _att_rope(x, c, a, b):
    n = x.shape[-1]
    return x * c + pltpu.roll(x, n - ROPE_HALF, axis=1) * a + pltpu.roll(x, ROPE_HALF, axis=1) * b


def _head_norm(o, g):
    mu = jnp.mean(o, axis=-1, keepdims=True)
    oc = o - mu
    var = jnp.mean(oc * oc, axis=-1, keepdims=True)
    return oc * lax.rsqrt(var + EPS) * g


def _ret_kernel(zq_ref, zk_ref, zv_ref, zg_ref, gn_ref, *rest, chunk, per_step, cast):
    if cast:
        cw_in, ro_ref, so_ref, cw_out, din_ref, dq_ref, dk_ref = rest
        cw_out[...] = cw_in[...].astype(BF16)
    else:
        ro_ref, so_ref, din_ref, dq_ref, dk_ref = rest
    c = pl.program_id(1)

    @pl.when(c == 0)
    def _():
        so_ref[...] = jnp.zeros_like(so_ref)
        row = lax.broadcasted_iota(jnp.int32, (chunk, chunk), 0).astype(F32)
        col = lax.broadcasted_iota(jnp.int32, (chunk, chunk), 1).astype(F32)
        diff = row - col
        rowd = lax.broadcasted_iota(jnp.int32, (chunk, RET_DIM), 0).astype(F32)
        for h in range(RET_HEADS):
            lg = LOG_GAMMA[h]
            din_ref[h] = jnp.where(diff >= 0, jnp.exp(jnp.maximum(diff, 0.0) * lg), 0.0)
            dq_ref[h] = jnp.exp((rowd + 1.0) * lg)
            dk_ref[h] = jnp.exp((chunk - 1.0 - rowd) * lg)

    for h in range(RET_HEADS):
        cols = slice(h * RET_DIM, (h + 1) * RET_DIM)
        s_cur = so_ref[h]
        for j in range(per_step):
            rows = slice(j * chunk, (j + 1) * chunk)
            k = zk_ref[rows, cols]
            qb = zq_ref[rows, cols].astype(BF16)
            kb = k.astype(BF16)
            vb = zv_ref[rows, cols].astype(BF16)
            a = _dot_nt(qb, kb) * din_ref[h]
            o = _dot(a.astype(BF16), vb) + _dot(qb, s_cur.astype(BF16)) * dq_ref[h]
            kd_t = (k * dk_ref[h]).T.astype(BF16)
            s_cur = math.exp(chunk * LOG_GAMMA[h]) * s_cur + _dot(kd_t, vb)

            on = _head_norm(o, gn_ref[:, cols])
            g = zg_ref[rows, cols]
            ro_ref[rows, cols] = (g * jax.nn.sigmoid(g) * on).astype(BF16)
        so_ref[h] = s_cur


def _retention_prompt(z3, ret_norm_g, layer, per_step, cast_next=None):
    b, s, _ = z3.shape
    chunk = math.gcd(s, RET_CHUNK)
    per_step = math.gcd(s // chunk, per_step)
    rows = chunk * per_step
    n_c = s // rows
    zspec = lambda cb: pl.BlockSpec((None, rows, RET_WIDTH), lambda i, c: (i, c, cb))
    in_specs = [
        zspec(COL_RQ // RET_WIDTH), zspec(COL_RK // RET_WIDTH),
        zspec(COL_RV // RET_WIDTH), zspec(COL_RG // RET_WIDTH),
        pl.BlockSpec((None, 1, RET_WIDTH), lambda i, c: (layer, 0, 0)),
    ]
    args = [z3, z3, z3, z3, ret_norm_g]
    out_shape = [jax.ShapeDtypeStruct((b, s, RET_WIDTH), BF16),
                 jax.ShapeDtypeStruct((b, RET_HEADS, RET_DIM, RET_DIM), F32)]
    out_specs = [pl.BlockSpec((None, rows, RET_WIDTH), lambda i, c: (i, c, 0)),
                 pl.BlockSpec((None, RET_HEADS, RET_DIM, RET_DIM), lambda i, c: (i, 0, 0, 0))]
    if cast_next is not None:
        w, layer_next = cast_next
        w_rows, w_cols = w.shape[1:]
        steps, blk = _row_slices(w_rows, b * n_c)
        at = lambda i, c: jnp.minimum(i * n_c + c, steps - 1)
        in_specs.append(pl.BlockSpec((None, blk, w_cols), lambda i, c: (layer_next, at(i, c), 0)))
        args.append(w)
        out_shape.append(jax.ShapeDtypeStruct((w_rows, w_cols), BF16))
        out_specs.append(pl.BlockSpec((blk, w_cols), lambda i, c: (at(i, c), 0)))
    return pl.pallas_call(
        functools.partial(_ret_kernel, chunk=chunk, per_step=per_step, cast=cast_next is not None),
        out_shape=tuple(out_shape),
        grid=(b, n_c),
        in_specs=in_specs,
        out_specs=tuple(out_specs),
        scratch_shapes=[
            pltpu.VMEM((RET_HEADS, chunk, chunk), F32),
            pltpu.VMEM((RET_HEADS, chunk, RET_DIM), F32),
            pltpu.VMEM((RET_HEADS, chunk, RET_DIM), F32),
        ],
        compiler_params=_params("arbitrary", "arbitrary"),
        name="retention",
    )(*args)


def _ret_dec_kernel(z_ref, s0_ref, cos_ref, sin_ref, gn_ref, ro_ref, so_ref, *, nb):
    cos_t = cos_ref[...]
    sin_t = sin_ref[...]
    row = lax.broadcasted_iota(jnp.int32, (nb, RET_DIM), 0)
    for h in range(RET_HEADS):
        gamma = math.exp(LOG_GAMMA[h])
        q = _ret_rope(z_ref[:, COL_RQ + h * RET_DIM:COL_RQ + (h + 1) * RET_DIM], cos_t, sin_t)
        k = _ret_rope(z_ref[:, COL_RK + h * RET_DIM:COL_RK + (h + 1) * RET_DIM], cos_t, sin_t)
        k = k * (RET_DIM ** -0.5)
        v = z_ref[:, COL_RV + h * RET_DIM:COL_RV + (h + 1) * RET_DIM]
        qr, kr, vr = _bf16_round(q), _bf16_round(k), _bf16_round(v)
        qk = jnp.sum(qr * kr, axis=-1, keepdims=True)
        o = _bf16_round(qk) * vr
        cross = jnp.zeros((nb, RET_DIM), F32)
        for b in range(nb):
            s_old = s0_ref[b, h]
            res = _dot(qr.astype(BF16), s_old.astype(BF16))
            cross = cross + jnp.where(row == b, res, 0.0)
            k_col = jnp.broadcast_to(kr[b:b + 1, :], (RET_DIM, RET_DIM)).T
            so_ref[b, h] = gamma * s_old + k_col * vr[b:b + 1, :]
        o = o + cross * gamma
        cols = slice(h * RET_DIM, (h + 1) * RET_DIM)
        on = _head_norm(o, gn_ref[:, cols])
        g = z_ref[:, COL_RG + h * RET_DIM:COL_RG + (h + 1) * RET_DIM]
        ro_ref[:, cols] = g * jax.nn.sigmoid(g) * on


def _retention_decode(z, state, cos_t, sin_t, ret_norm_g, layer):
    nb = z.shape[0]
    sshape = (nb, RET_HEADS, RET_DIM, RET_DIM)
    return pl.pallas_call(
        functools.partial(_ret_dec_kernel, nb=nb),
        out_shape=(jax.ShapeDtypeStruct((nb, RET_WIDTH), F32),
                   jax.ShapeDtypeStruct(sshape, F32)),
        grid=(1,),
        in_specs=[
            pl.BlockSpec(z.shape, lambda i: (0, 0)),
            pl.BlockSpec((None,) + sshape, lambda i: (layer, 0, 0, 0, 0)),
            pl.BlockSpec((1, RET_DIM), lambda i: (0, 0)),
            pl.BlockSpec((1, RET_DIM), lambda i: (0, 0)),
            pl.BlockSpec((None, 1, RET_WIDTH), lambda i: (layer, 0, 0)),
        ],
        out_specs=(
            pl.BlockSpec((nb, RET_WIDTH), lambda i: (0, 0)),
            pl.BlockSpec(sshape, lambda i: (0, 0, 0, 0)),
        ),
        compiler_params=_params("arbitrary"),
        name="retention_decode",
    )(z, state, cos_t, sin_t, ret_norm_g)


ATT_HALF = 128
ATT_MAX_STRIDE = 4
N_HALF = ATT_GROUP // ATT_HALF
HEADS_PER_HALF = ATT_HALF // ATT_DIM


def _half_cols(hf):
    return slice(hf * ATT_HALF, (hf + 1) * ATT_HALF)


def _att_kernel(*refs, seq):
    zqkv = (refs[0:N_HALF], refs[N_HALF:2 * N_HALF], refs[2 * N_HALF:3 * N_HALF])
    ao_ref, o_ref, lse_ref, st_ref, ost_ref = refs[3 * N_HALF:]
    g = pl.program_id(1)
    qb = QUERY_BLOCK
    n_blocks = seq // qb

    tq = lax.broadcasted_iota(jnp.int32, (qb, qb), 0)
    tk = lax.broadcasted_iota(jnp.int32, (qb, qb), 1)
    cur_valid = tk <= tq
    prev_valid = tk >= tq
    lane = lax.broadcasted_iota(jnp.int32, (qb, ATT_HALF), 1)
    head_masks = [(lane // ATT_DIM) == hh for hh in range(HEADS_PER_HALF)]

    def group_body(gi, dil):
        d1 = min(dil, ATT_MAX_STRIDE)
        d2 = dil // d1
        staged = d2 > 1
        assert d2 <= ATT_MAX_STRIDE and d1 * d2 == dil
        sub_len = seq // d1

        def strided(start, n, stride):
            return pl.ds(start, n, stride=stride) if stride > 1 else pl.ds(start, n)

        if staged:
            for hf in range(N_HALF):
                for r1 in range(d1):
                    src = strided(r1, sub_len, d1)
                    for which in range(3):
                        st_ref[which, hf, r1] = zqkv[which][hf][src, :]

        def load(which, hf, cls, blk):
            if staged:
                rows = strided(cls // d1 + d2 * qb * blk, qb, d2)
                return st_ref[which, hf, cls % d1, rows, :].astype(BF16)
            rows = strided(cls + dil * qb * blk, qb, dil)
            return zqkv[which][hf][rows, :].astype(BF16)

        def store(which, hf, cls, blk, val):
            if staged:
                rows = strided(cls // d1 + d2 * qb * blk, qb, d2)
                ost_ref[which, hf, cls % d1, rows, :] = val
            else:
                rows = strided(cls + dil * qb * blk, qb, dil)
                (o_ref, lse_ref)[which][gi, hf, rows, :] = val

        def block_body(t, carry):
            cls = t % dil
            blk = t // dil
            pblk = jnp.maximum(blk - 1, 0)
            valid = jnp.concatenate([prev_valid & (blk > 0), cur_valid], axis=1)
            for hf in range(N_HALF):
                q = load(0, hf, cls, blk)
                keys = jnp.concatenate([load(1, hf, cls, pblk), load(1, hf, cls, blk)], axis=0)
                vals = jnp.concatenate([load(2, hf, cls, pblk), load(2, hf, cls, blk)], axis=0)
                vals_ones = jnp.concatenate([vals, jnp.ones_like(vals)], axis=1)
                num = jnp.zeros((qb, ATT_HALF), F32)
                den = jnp.ones((qb, ATT_HALF), F32)
                lse_acc = jnp.zeros((qb, ATT_HALF), F32)
                for hm in head_masks:
                    qh = jnp.where(hm, q, jnp.zeros_like(q))
                    s = jnp.where(valid, _dot_nt(qh, keys), MASK_VALUE)
                    m = jnp.max(s, axis=-1, keepdims=True)
                    e = jnp.exp(s - m).astype(BF16)
                    r = _dot(e, vals_ones)
                    l = r[:, ATT_HALF:]
                    num = jnp.where(hm, r[:, :ATT_HALF], num)
                    den = jnp.where(hm, l, den)
                    lse_acc = jnp.where(hm, m + jnp.log(l), lse_acc)
                store(0, hf, cls, blk, num / den)
                store(1, hf, cls, blk, lse_acc)
            return carry

        lax.fori_loop(0, n_blocks, block_body, 0, unroll=4)

        if staged:
            for hf in range(N_HALF):
                for r1 in range(d1):
                    dst = strided(r1, sub_len, d1)
                    o_ref[gi, hf, dst, :] = ost_ref[0, hf, r1]
                    lse_ref[gi, hf, dst, :] = ost_ref[1, hf, r1]

    for gi, (_, dil) in enumerate(ATT_PATTERNS):
        @pl.when(g == gi)
        def _(gi=gi, dil=dil):
            group_body(gi, dil)

    @pl.when(g == N_PAT - 1)
    def _():
        def merge_body(i, carry):
            r = pl.multiple_of(i * qb, qb)
            rows = pl.ds(r, qb)
            for hf in range(N_HALF):
                lses = [lse_ref[gi, hf, rows, :] for gi in range(N_PAT)]
                m = functools.reduce(jnp.maximum, lses)
                ws = [jnp.exp(l - m) for l in lses]
                den = functools.reduce(lambda x, y: x + y, ws)
                num = functools.reduce(lambda x, y: x + y,
                                       [w * o_ref[gi, hf, rows, :] for gi, w in enumerate(ws)])
                ao_ref[rows, _half_cols(hf)] = (num / den).astype(BF16)
            return carry

        lax.fori_loop(0, n_blocks, merge_body, 0)


def _attention_prompt(z3):
    b, s, _ = z3.shape
    assert all(s % (dil * QUERY_BLOCK) == 0 for _, dil in ATT_PATTERNS)
    zspec = lambda col, hf: pl.BlockSpec(
        (None, s, ATT_HALF), lambda i, g: (i, 0, col // ATT_HALF + N_HALF * g + hf))
    zspecs = [zspec(col, hf) for col in (COL_AQ, COL_AK, COL_AV) for hf in range(N_HALF)]
    return pl.pallas_call(
        functools.partial(_att_kernel, seq=s),
        out_shape=jax.ShapeDtypeStruct((b, s, ATT_GROUP), BF16),
        grid=(b, N_PAT),
        in_specs=zspecs,
        out_specs=pl.BlockSpec((None, s, ATT_GROUP), lambda i, g: (i, 0, 0)),
        scratch_shapes=[
            pltpu.VMEM((N_PAT, N_HALF, s, ATT_HALF), F32),
            pltpu.VMEM((N_PAT, N_HALF, s, ATT_HALF), F32),
            pltpu.VMEM((3, N_HALF, ATT_MAX_STRIDE, s // ATT_MAX_STRIDE, ATT_HALF), F32),
            pltpu.VMEM((2, N_HALF, ATT_MAX_STRIDE, s // ATT_MAX_STRIDE, ATT_HALF), F32),
        ],
        compiler_params=_params("parallel", "arbitrary"),
        name="dilated_attention",
    )(*([z3] * (3 * N_HALF)))


def _cache_shift_kernel(*refs):
    n = len(refs) // 2
    for src, dst in zip(refs[:n], refs[n:]):
        h, dim, w = src.shape
        dst[...] = pltpu.roll(src[...].reshape(h * dim, w), w - 1, axis=1).reshape(h, dim, w)


def _cache_shift(caches):
    depth, nb = caches[0].shape[:2]
    specs = [pl.BlockSpec((None, None) + c.shape[2:], lambda l, i: (l, i, 0, 0, 0)) for c in caches]
    return pl.pallas_call(
        _cache_shift_kernel,
        out_shape=tuple(jax.ShapeDtypeStruct(c.shape, c.dtype) for c in caches),
        grid=(depth, nb),
        in_specs=specs,
        out_specs=tuple(specs),
        compiler_params=_params("arbitrary", "arbitrary"),
        name="cache_shift",
    )(*caches)


def _row_to_col(row, eye):
    return jnp.sum(jnp.where(eye, row, 0.0), axis=1, keepdims=True)


def _col_to_row(col, eye):
    return jnp.sum(jnp.where(eye, col, 0.0), axis=0, keepdims=True)


def _att_dec_kernel(z_ref, c_ref, a_ref, b_ref, *refs):
    kc_refs = refs[0:N_PAT]
    vc_refs = refs[N_PAT:2 * N_PAT]
    kt_refs = refs[2 * N_PAT:3 * N_PAT]
    vt_refs = refs[3 * N_PAT:4 * N_PAT]
    ao_ref = refs[4 * N_PAT]
    kt_out_refs = refs[4 * N_PAT + 1:5 * N_PAT + 1]
    vt_out_refs = refs[5 * N_PAT + 1:6 * N_PAT + 1]
    rope =(c_ref[...], a_ref[...], b_ref[...])
    eye = (lax.broadcasted_iota(jnp.int32, (ATT_DIM, ATT_DIM), 0)
           == lax.broadcasted_iota(jnp.int32, (ATT_DIM, ATT_DIM), 1))
    last_lane = lax.broadcasted_iota(jnp.int32, (ATT_DIM, ATT_HALF), 1) == ATT_HALF - 1

    head_rows = []
    for h in range(ATT_HEADS):
        hf, lanes = h // HEADS_PER_HALF, slice((h % HEADS_PER_HALF) * ATT_DIM,
                                               (h % HEADS_PER_HALF + 1) * ATT_DIM)
        outs, lses = [], []
        for gi, (win, dil) in enumerate(ATT_PATTERNS):
            off = gi * ATT_GROUP + hf * ATT_HALF
            zrow = lambda col: z_ref[:, col + off:col + off + ATT_HALF]
            q = (_att_rope(zrow(COL_AQ), *rope) * (ATT_DIM ** -0.5))[:, lanes]
            k_new = _att_rope(zrow(COL_AK), *rope)[:, lanes]
            v_new = zrow(COL_AV)[:, lanes]
            k_col, v_col = _row_to_col(k_new, eye), _row_to_col(v_new, eye)
            keys, vals = kc_refs[gi][h], vc_refs[gi][h]
            w = keys.shape[1]
            pos = lax.broadcasted_iota(jnp.int32, (1, w), 1)
            s_old = jnp.sum(keys * _row_to_col(q, eye), axis=0, keepdims=True)
            s_old = jnp.where(pos % dil == 0, s_old, MASK_VALUE)
            s_new = jnp.sum(q * k_new, axis=1, keepdims=True)
            m = jnp.maximum(jnp.max(s_old, axis=1, keepdims=True), s_new)
            e_old = jnp.exp(s_old - m)
            e_new = jnp.exp(s_new - m)
            l = jnp.sum(e_old, axis=1, keepdims=True) + e_new
            pv = jnp.sum(vals * e_old, axis=1, keepdims=True) + e_new * v_col
            outs.append(pv / l)
            lses.append(m + jnp.log(l))
            kt_out_refs[gi][h] = jnp.where(last_lane, k_col, kt_refs[gi][h])
            vt_out_refs[gi][h] = jnp.where(last_lane, v_col, vt_refs[gi][h])

        m = functools.reduce(jnp.maximum, lses)
        ws = [jnp.exp(l - m) for l in lses]
        den = functools.reduce(lambda x, y: x + y, ws)
        num = functools.reduce(lambda x, y: x + y, [w * o for w, o in zip(ws, outs)])
        head_rows.append(_col_to_row(num / den, eye))
    ao_ref[...] = jnp.concatenate(head_rows, axis=1)


def _attention_decode(z, tabs, k_caches, v_caches, k_next, v_next, layer):
    nb = z.shape[0]
    n_fixed = 4
    cspec = lambda c: pl.BlockSpec((None, None) + c.shape[2:], lambda i: (layer, i, 0, 0, 0))
    tail = lambda c: pl.BlockSpec((None, None) + c.shape[2:4] + (ATT_HALF,),
                                  lambda i: (layer, i, 0, 0, c.shape[4] // ATT_HALF - 1))
    tspec = pl.BlockSpec((1, ATT_HALF), lambda i: (0, 0))
    caches = list(k_caches) + list(v_caches)
    nexts = list(k_next) + list(v_next)
    return pl.pallas_call(
        _att_dec_kernel,
        out_shape=tuple([jax.ShapeDtypeStruct((nb, 1, ATT_GROUP), F32)]
                        + [jax.ShapeDtypeStruct(c.shape, c.dtype) for c in nexts]),
        grid=(nb,),
        in_specs=[pl.BlockSpec((None, 1, z.shape[1]), lambda i: (i, 0, 0)), tspec, tspec, tspec]
                 + [cspec(c) for c in caches] + [tail(c) for c in nexts],
        out_specs=tuple([pl.BlockSpec((None, 1, ATT_GROUP), lambda i: (i, 0, 0))]
                        + [tail(c) for c in nexts]),
        input_output_aliases={n_fixed + len(caches) + j: 1 + j for j in range(len(nexts))},
        compiler_params=_params("arbitrary"),
        name="dilated_attention_decode",
    )(z.reshape(nb, 1, z.shape[1]), *tabs, *caches, *nexts)


def _pool_kernel(u_ref, w_ref, sc_ref, po_ref, a_ref, b_ref, *, seq):
    g = pl.program_id(1)
    body = pl.ds(POOL_PAD, seq)

    def window_mean_minus_token(win):
        x = u_ref[...]
        a_ref[0:POOL_PAD, :] = jnp.zeros((POOL_PAD, POOL_GROUP), F32)
        b_ref[0:POOL_PAD, :] = jnp.zeros((POOL_PAD, POOL_GROUP), F32)
        a_ref[body, :] = x
        src, dst = a_ref, b_ref
        k = 1
        while k < win:
            dst[body, :] = src[body, :] + src[pl.ds(POOL_PAD - k, seq), :]
            src, dst = dst, src
            k *= 2
        t = lax.broadcasted_iota(jnp.int32, (seq, POOL_GROUP), 0)
        cnt = jnp.minimum(t + 1, win).astype(F32)
        pooled = src[body, :] / cnt - x
        y = _dot(pooled.astype(BF16), w_ref[...].astype(BF16)) * sc_ref[...]
        po_ref[...] = y.astype(BF16)

    for gi, win in enumerate(POOL_WINDOWS):
        @pl.when(g == gi)
        def _(win=win):
            window_mean_minus_token(win)


def _pool_prompt(z3, w_pool, pool_scale, layer):
    b, s, _ = z3.shape
    ng = len(POOL_WINDOWS)
    return pl.pallas_call(
        functools.partial(_pool_kernel, seq=s),
        out_shape=jax.ShapeDtypeStruct((b, s, POOL_WIDTH), BF16),
        grid=(b, ng),
        in_specs=[
            pl.BlockSpec((None, s, POOL_GROUP), lambda i, g: (i, 0, COL_PU // POOL_GROUP + g)),
            pl.BlockSpec((None, None, POOL_GROUP, POOL_GROUP), lambda i, g: (layer, g, 0, 0)),
            pl.BlockSpec((None, 1, POOL_GROUP), lambda i, g: (layer, 0, g)),
        ],
        out_specs=pl.BlockSpec((None, s, POOL_GROUP), lambda i, g: (i, 0, g)),
        scratch_shapes=[pltpu.VMEM((POOL_PAD + s, POOL_GROUP), F32),
                        pltpu.VMEM((POOL_PAD + s, POOL_GROUP), F32)],
        compiler_params=_params("parallel", "arbitrary"),
        name="pool_mixer",
    )(z3, w_pool, pool_scale)


def _pool_dec_kernel(z_ref, buf_ref, w_ref, sc_ref, po_ref, bo_ref, pooled_ref, *, nb):
    row = lax.broadcasted_iota(jnp.int32, (POOL_BUF, POOL_GROUP), 0)
    for b in range(nb):
        u = z_ref[b:b + 1, COL_PU:COL_PU + POOL_WIDTH]
        old = buf_ref[b]
        for gi, win in enumerate(POOL_WINDOWS):
            cols = slice(gi * POOL_GROUP, (gi + 1) * POOL_GROUP)
            tail = jnp.where(row >= POOL_BUF - (win - 1), old[:, cols], 0.0)
            total = jnp.sum(tail, axis=0, keepdims=True) + u[:, cols]
            pooled_ref[b:b + 1, cols] = total / float(win) - u[:, cols]
        bo_ref[b, 0:POOL_BUF - 1, :] = old[1:POOL_BUF, :]
        bo_ref[b, POOL_BUF - 1:POOL_BUF, :] = u
    for gi in range(len(POOL_WINDOWS)):
        cols = slice(gi * POOL_GROUP, (gi + 1) * POOL_GROUP)
        y = _dot(pooled_ref[:, cols].astype(BF16), w_ref[gi].astype(BF16))
        po_ref[:, cols] = y * sc_ref[:, cols]


def _pool_decode(z, cache_pool, w_pool, pool_scale, layer):
    nb = z.shape[0]
    ng = len(POOL_WINDOWS)
    bshape = (nb, POOL_BUF, POOL_WIDTH)
    return pl.pallas_call(
        functools.partial(_pool_dec_kernel, nb=nb),
        out_shape=(jax.ShapeDtypeStruct((nb, POOL_WIDTH), F32),
                   jax.ShapeDtypeStruct(bshape, F32)),
        grid=(1,),
        in_specs=[
            pl.BlockSpec(z.shape, lambda i: (0, 0)),
            pl.BlockSpec((None,) + bshape, lambda i: (layer, 0, 0, 0)),
            pl.BlockSpec((None, ng, POOL_GROUP, POOL_GROUP), lambda i: (layer, 0, 0, 0)),
            pl.BlockSpec((None, 1, POOL_WIDTH), lambda i: (layer, 0, 0)),
        ],
        out_specs=(pl.BlockSpec((nb, POOL_WIDTH), lambda i: (0, 0)),
                   pl.BlockSpec(bshape, lambda i: (0, 0, 0))),
        scratch_shapes=[pltpu.VMEM((nb, POOL_WIDTH), F32)],
        compiler_params=_params("arbitrary"),
        name="pool_mixer_decode",
    )(z, cache_pool, w_pool, pool_scale)


FFN_COLS = 512
RET_CHUNKS_PER_STEP = 8


def _prompt_tiles(seq):
    return {
        "ffn_rows": math.gcd(seq, 1024),
        "inproj_rows": math.gcd(seq, 256),
        "outproj_rows": math.gcd(seq, 512),
    }


def _trunk_prompt(x, mod, p, bank, final_g):
    b, s, d = x.shape
    depth = p["w_in"].shape[0]
    pos = jnp.arange(s, dtype=F32)
    ret_tabs = _ret_rope_tables(pos)
    att_tabs = _att_rope_tables(pos)
    keep = tuple(min(win, s) for win, _ in ATT_PATTERNS)
    t = _prompt_tiles(s)
    xf = x.reshape(b * s, d)
    rets, pools = [], []
    ks = [[] for _ in ATT_PATTERNS]
    vs = [[] for _ in ATT_PATTERNS]
    for l in range(depth):
        while ("ffn1", l) not in bank:
            yield
        xf, *bank["ffn2", l] = _ffn(xf, mod, l, 0, s, p["norm_g"], *bank["ffn1", l], None,
                                    t["ffn_rows"], FFN_COLS,
                                    cast_next=(p["w2_gate"], p["w2_up"], p["w2_down"], l))
        while ("w_in", l) not in bank:
            yield
        z = _inproj(xf, mod, l, s, p["norm_g"], bank["w_in", l], ret_tabs + att_tabs, t["inproj_rows"])
        z3 = z.reshape(b, s, IN_WIDTH)
        if l + 1 < depth:
            ro, ret_s, bank["w_in", l + 1] = _retention_prompt(
                z3, p["ret_norm_g"], l, RET_CHUNKS_PER_STEP, cast_next=(p["w_in"], l + 1))
        else:
            ro, ret_s = _retention_prompt(z3, p["ret_norm_g"], l, RET_CHUNKS_PER_STEP)
        ao = _attention_prompt(z3)
        po = _pool_prompt(z3, p["w_pool"], p["pool_scale"], l)
        xf = _outproj(xf, mod, l, s, ro.reshape(b * s, -1), ao.reshape(b * s, -1),
                      po.reshape(b * s, -1), p["w_out"], t["outproj_rows"])
        if l + 1 < depth:
            xf, *bank["ffn1", l + 1] = _ffn(xf, mod, l, 2, s, p["norm_g"], *bank["ffn2", l], None,
                                            t["ffn_rows"], FFN_COLS,
                                            cast_next=(p["w1_gate"], p["w1_up"], p["w1_down"], l + 1))
        else:
            xf = _ffn(xf, mod, l, 2, s, p["norm_g"], *bank["ffn2", l], final_g,
                      t["ffn_rows"], FFN_COLS)
        rets.append(ret_s)
        pools.append(z3[:, s - POOL_BUF:, COL_PU:])
        for g in range(N_PAT):
            for col, dst in ((COL_AK, ks), (COL_AV, vs)):
                c0 = col + g * ATT_GROUP
                dst[g].append(z3[:, s - keep[g]:, c0:c0 + ATT_GROUP]
                              .reshape(b, keep[g], ATT_HEADS, ATT_DIM))
    y = xf.reshape(b, s, d)
    return (y, jnp.stack(rets), [jnp.stack(k) for k in ks], [jnp.stack(v) for v in vs],
            jnp.stack(pools))


def _trunk_decode(x, mod, pos0, caches, p, bank, final_g):
    nb, s, d = x.shape
    assert s == 1, "decode trunk handles one new token per batch row"
    depth = p["w_in"].shape[0]
    state_ret, cks, cvs, cpool = caches
    pos = pos0 + jnp.arange(s, dtype=F32)
    ret_tabs = _ret_rope_tables(pos)
    att_tabs = _att_rope_tables(pos)
    feature_major = lambda c: jnp.transpose(c, (0, 1, 3, 4, 2))
    position_major = lambda c: jnp.transpose(c, (0, 1, 4, 2, 3))
    cks = [feature_major(c) for c in cks]
    cvs = [feature_major(c) for c in cvs]
    advanced = _cache_shift(cks + cvs)
    k_next, v_next = list(advanced[:N_PAT]), list(advanced[N_PAT:])
    xf = x.reshape(nb, d)
    rets, pools = [], []
    for l in range(depth):
        if l == 0:
            xf, *bank["ffn1", l] = _ffn(xf, mod, l, 0, 1, p["norm_g"], p["w1_gate"], p["w1_up"],
                                        p["w1_down"], None, nb, FFN_COLS, emit_bf16=True)
        else:
            while ("ffn1", l) not in bank:
                yield
            xf = _ffn(xf, mod, l, 0, 1, p["norm_g"], *bank["ffn1", l], None, nb, FFN_COLS)
        if l == 0:
            z, bank["w_in", l] = _inproj_rows(xf, mod, l, p["norm_g"], p["w_in"])
        else:
            z = _inproj_rows(xf, mod, l, p["norm_g"], bank["w_in", l])
        ro, ret_s = _retention_decode(z, state_ret, *ret_tabs, p["ret_norm_g"], l)
        att = _attention_decode(z, att_tabs, cks, cvs, k_next, v_next, l)
        ao, k_next, v_next = att[0].reshape(nb, ATT_GROUP), list(att[1:1 + N_PAT]), list(att[1 + N_PAT:])
        po, pool_new = _pool_decode(z, cpool, p["w_pool"], p["pool_scale"], l)
        xf = _outproj(xf, mod, l, 1, ro, ao, po, p["w_out"], nb)
        while ("ffn2", l) not in bank:
            yield
        xf = _ffn(xf, mod, l, 2, 1, p["norm_g"], *bank["ffn2", l],
                  final_g if l == depth - 1 else None, nb, FFN_COLS)
        rets.append(ret_s)
        pools.append(pool_new)
    y = xf.reshape(nb, s, d)
    return (y, jnp.stack(rets), [position_major(k) for k in k_next],
            [position_major(v) for v in v_next], jnp.stack(pools))


def _run_trunks(*trunks):
    results = [None] * len(trunks)
    running = dict(enumerate(trunks))
    while running:
        for i, trunk in list(running.items()):
            try:
                next(trunk)
            except StopIteration as done:
                results[i] = done.value
                del running[i]
    return results


def kernel(x_prompt, x_sample, state_ret, cache_k_w128, cache_v_w128, cache_k_w512, cache_v_w512,
           cache_k_w2048, cache_v_w2048, cache_pool, c_prompt, c_sample, w_ada, b_ada, norm_g, w_in,
           ret_norm_g, w_pool, pool_scale, w_out, w1_gate, w1_up, w1_down, w2_gate, w2_up, w2_down,
           final_norm_g):
    depth, d = norm_g.shape[0], norm_g.shape[-1]
    n_pr, n_dec = c_prompt.shape[0], c_sample.shape[0]
    p = {
        "norm_g": norm_g.reshape(depth, N_SUB, 1, d),
        "w_in": w_in,
        "ret_norm_g": ret_norm_g.reshape(depth, 1, RET_WIDTH),
        "w_pool": w_pool,
        "pool_scale": pool_scale.reshape(depth, 1, POOL_WIDTH),
        "w_out": w_out,
        "w1_gate": w1_gate, "w1_up": w1_up, "w1_down": w1_down,
        "w2_gate": w2_gate, "w2_up": w2_up, "w2_down": w2_down,
    }
    final_g = final_norm_g.reshape(1, d)

    pad = (-(n_dec + n_pr)) % SUBLANES
    c_all = jnp.concatenate([c_sample, c_prompt, jnp.zeros((pad, d), F32)], axis=0)
    mod_dec, mod_pr = _ada(c_all, n_dec, n_pr, w_ada, b_ada)

    caches = (state_ret,
              (cache_k_w128, cache_k_w512, cache_k_w2048),
              (cache_v_w128, cache_v_w512, cache_v_w2048),
              cache_pool)
    bank = {}
    (y_s, ret_s, ks, vs, pool_s), (y_p, ret_p, kp, vp, pool_p) = _run_trunks(
        _trunk_decode(x_sample, mod_dec, float(PAST_LEN), caches, p, bank, final_g),
        _trunk_prompt(x_prompt, mod_pr, p, bank, final_g))
    return (y_p, y_s, ret_p, ret_s,
            kp[0], ks[0], vp[0], vs[0],
            kp[1], ks[1], vp[1], vs[1],
            kp[2], ks[2], vp[2], vs[2],
            pool_p, pool_s)
```
